```python
import math
import jax, jax.numpy as jnp
from jax import lax
import numpy as np

D_MODEL = 2048
BATCH = 2
SEQ = 4096
DEPTH = 4

ATTN_HEADS = 8
ATTN_KV_HEADS = 2
HEAD_DIM = 128
WINDOW = 128
ROPE_THETA = 10000.0
DN_HEADS = 4
DN_HEAD_DIM = 128
DN_CONV = 4
DN_CHUNK = 64
S5_GROUPS = 32
S5_GROUP_CH = 16
S5_STATE = 64
ATTN_WIDTH = ATTN_HEADS * HEAD_DIM
ATTN_KV_WIDTH = ATTN_KV_HEADS * HEAD_DIM
DN_WIDTH = DN_HEADS * DN_HEAD_DIM
S5_WIDTH = S5_GROUPS * S5_GROUP_CH
MIX_WIDTH = ATTN_WIDTH + DN_WIDTH + S5_WIDTH
IN_SPLITS = (ATTN_WIDTH, ATTN_KV_WIDTH, ATTN_KV_WIDTH, 3 * DN_WIDTH, DN_WIDTH, DN_HEADS, DN_HEADS, S5_WIDTH)
IN_WIDTH = sum(IN_SPLITS)
D_FF = 5632
FFN_RES_WEIGHT = 0.5
NORM_EPS = 1e-6

kernel_name = "hymba_style_swa_deltanet_s5_macaron"


def rms_norm(x, gain):
    xf = x.astype(jnp.float32)
    y = xf * lax.rsqrt(jnp.mean(xf * xf, axis=-1, keepdims=True) + NORM_EPS)
    return (y * gain.astype(jnp.float32)).astype(x.dtype)


def l2_norm(x):
    return x * lax.rsqrt(jnp.sum(x * x, axis=-1, keepdims=True) + NORM_EPS)


def swiglu(h, w_gate, w_up, w_down):
    return (jax.nn.silu(h @ w_gate) * (h @ w_up)) @ w_down


def rope_tables(seq):
    half = HEAD_DIM // 2
    inv_freq = ROPE_THETA ** (-jnp.arange(half, dtype=jnp.float32) / half)
    ang = jnp.arange(seq, dtype=jnp.float32)[:, None] * inv_freq[None, :]
    return jnp.cos(ang), jnp.sin(ang)


def apply_rope(x, cos, sin):
    half = HEAD_DIM // 2
    xf = x.astype(jnp.float32)
    x1, x2 = xf[..., :half], xf[..., half:]
    c = cos[None, :, None, :]
    s = sin[None, :, None, :]
    return jnp.concatenate([x1 * c - x2 * s, x2 * c + x1 * s], axis=-1).astype(x.dtype)


def sliding_window_attention(q, k, v, sinks):
    b, s, hq, d = q.shape
    hkv = k.shape[2]
    grp = hq // hkv
    nb = s // WINDOW
    qb = q.reshape(b, nb, WINDOW, hkv, grp, d)
    kb = k.reshape(b, nb, WINDOW, hkv, d)
    vb = v.reshape(b, nb, WINDOW, hkv, d)
    pad = ((0, 0), (1, 0), (0, 0), (0, 0), (0, 0))
    kk = jnp.concatenate([jnp.pad(kb, pad)[:, :-1], kb], axis=2)
    vv = jnp.concatenate([jnp.pad(vb, pad)[:, :-1], vb], axis=2)
    scores = jnp.einsum('bnqhgd,bnkhd->bnhgqk', qb, kk, preferred_element_type=jnp.float32) * (d ** -0.5)
    qi = jnp.arange(WINDOW)[:, None] + WINDOW
    kj = jnp.arange(2 * WINDOW)[None, :]
    rel = qi - kj
    band = (rel >= 0) & (rel < WINDOW)
    first = (jnp.arange(nb) == 0)[:, None, None] & (kj < WINDOW)[None]
    mask = band[None] & jnp.logical_not(first)
    scores = jnp.where(mask[None, :, None, None], scores, -jnp.inf)
    sink = sinks.astype(jnp.float32).reshape(hkv, grp)[None, None, :, :, None, None]
    m = jnp.maximum(jnp.max(scores, axis=-1, keepdims=True), sink)
    p = jnp.exp(scores - m)
    p = p / (jnp.sum(p, axis=-1, keepdims=True) + jnp.exp(sink - m))
    out = jnp.einsum('bnhgqk,bnkhd->bnqhgd', p.astype(vv.dtype), vv)
    return out.reshape(b, s, hq * d)


def causal_depthwise_conv(u, w):
    taps = w.shape[0]
    return lax.conv_general_dilated(u, w[:, None, :], window_strides=(1,), padding=[(taps - 1, 0)],
                                    dimension_numbers=('NWC', 'WIO', 'NWC'), feature_group_count=u.shape[-1])


def gated_delta_rule(q, k, v, g, beta):
    b, s, h, dk = q.shape
    dv = v.shape[-1]
    c = DN_CHUNK
    n = s // c

    def chunks(t):
        return t.reshape(b, n, c, h, -1).transpose(0, 1, 3, 2, 4)

    q = chunks(q) * (dk ** -0.5)
    k = chunks(k)
    v = chunks(v)
    beta = chunks(beta[..., None])[..., 0]
    g = jnp.cumsum(chunks(g[..., None])[..., 0], axis=-1)
    causal = jnp.tril(jnp.ones((c, c), bool))
    strict = jnp.tril(jnp.ones((c, c), bool), -1)
    decay = jnp.exp(jnp.where(causal, g[..., :, None] - g[..., None, :], -jnp.inf))
    k_beta = k * beta[..., None]
    lower = jnp.where(strict, jnp.einsum('bnhid,bnhjd->bnhij', k_beta, k) * decay, 0.0) + jnp.eye(c, dtype=jnp.float32)
    rhs = jnp.concatenate([v * beta[..., None], k_beta * jnp.exp(g)[..., None]], axis=-1)
    uw = lax.linalg.triangular_solve(lower, rhs, left_side=True, lower=True, unit_diagonal=True)
    u, w = uw[..., :dv], uw[..., dv:]
    attn = jnp.where(causal, jnp.einsum('bnhid,bnhjd->bnhij', q, k) * decay, 0.0)
    q_dec = q * jnp.exp(g)[..., None]
    g_last = g[..., -1]
    k_dec = k * jnp.exp(g_last[..., None] - g)[..., None]

    def step(state, inp):
        q_c, k_c, u_c, w_c, a_c, gl = inp
        v_new = u_c - jnp.einsum('bhck,bhkv->bhcv', w_c, state)
        o = jnp.einsum('bhck,bhkv->bhcv', q_c, state) + jnp.einsum('bhij,bhjv->bhiv', a_c, v_new)
        state = state * jnp.exp(gl)[..., None, None] + jnp.einsum('bhck,bhcv->bhkv', k_c, v_new)
        return state, o

    mv = lambda t: jnp.moveaxis(t, 1, 0)
    s0 = jnp.zeros((b, h, dk, dv), jnp.float32)
    _, o = lax.scan(step, s0, (mv(q_dec), mv(k_dec), mv(u), mv(w), mv(attn), mv(g_last)))
    return o.transpose(1, 0, 3, 2, 4).reshape(b, s, h, dv)


def gated_deltanet(qkv_raw, z_gate, b_raw, a_raw, conv_w, a_log, dt_bias, norm_w):
    b, s, _ = qkv_raw.shape
    qkv = jax.nn.silu(causal_depthwise_conv(qkv_raw, conv_w)).astype(jnp.float32)
    q, k, v = jnp.split(qkv, 3, axis=-1)
    shp = (b, s, DN_HEADS, DN_HEAD_DIM)
    q = l2_norm(q.reshape(shp))
    k = l2_norm(k.reshape(shp))
    v = v.reshape(shp)
    beta = jax.nn.sigmoid(b_raw.astype(jnp.float32))
    g = -jnp.exp(a_log.astype(jnp.float32)) * jax.nn.softplus(a_raw.astype(jnp.float32) + dt_bias.astype(jnp.float32))
    o = gated_delta_rule(q, k, v, g, beta)
    o = o * lax.rsqrt(jnp.mean(o * o, axis=-1, keepdims=True) + NORM_EPS) * norm_w.astype(jnp.float32)
    o = o * jax.nn.silu(z_gate.astype(jnp.float32).reshape(shp))
    return o.reshape(b, s, DN_WIDTH).astype(qkv_raw.dtype)


def s5_mixer(u, a_re, a_im, log_dt, b_re, b_im, c_re, c_im, d_skip, glu_w, glu_b):
    b, s, _ = u.shape
    uf = u.astype(jnp.float32).reshape(b, s, S5_GROUPS, S5_GROUP_CH)
    lam = lax.complex(a_re.astype(jnp.float32), a_im.astype(jnp.float32))
    dt = jnp.exp(log_dt.astype(jnp.float32))[:, None]
    a_bar = jnp.exp(lam * dt)
    b_c = lax.complex(b_re.astype(jnp.float32), b_im.astype(jnp.float32))
    b_bar = ((a_bar - 1.0) / lam)[..., None] * b_c
    bu = jnp.einsum('bsgh,gph->bsgp', uf.astype(jnp.complex64), b_bar)
    a_seq = jnp.broadcast_to(a_bar, bu.shape)

    def combine(e1, e2):
        a1, x1 = e1
        a2, x2 = e2
        return a1 * a2, a2 * x1 + x2

    _, states = lax.associative_scan(combine, (a_seq, bu), axis=1)
    c_c = lax.complex(c_re.astype(jnp.float32), c_im.astype(jnp.float32))
    y = jnp.real(jnp.einsum('bsgp,ghp->bsgh', states, c_c))
    y = y + d_skip.astype(jnp.float32).reshape(S5_GROUPS, S5_GROUP_CH) * uf
    y = jax.nn.gelu(y.reshape(b, s, S5_WIDTH)).astype(u.dtype)
    return y * jax.nn.sigmoid(y @ glu_w + glu_b)


def setup_inputs(seed: int = 0) -> dict:
    key = jax.random.key(seed)
    ks = iter(jax.random.split(key, 40))
    f32 = jnp.float32

    def nrm(shape, scale):
        return scale * jax.random.normal(next(ks), shape, f32)

    def gain(width=D_MODEL):
        return 1.0 + nrm((DEPTH, width), 0.02)

    x = nrm((BATCH, SEQ, D_MODEL), 1.0)
    ff1_norm_pre = gain()
    ff1_w_gate = nrm((DEPTH, D_MODEL, D_FF), D_MODEL ** -0.5)
    ff1_w_up = nrm((DEPTH, D_MODEL, D_FF), D_MODEL ** -0.5)
    ff1_w_down = nrm((DEPTH, D_FF, D_MODEL), D_FF ** -0.5)
    ff1_norm_post = gain()
    mix_norm_pre = gain()
    w_in = nrm((DEPTH, D_MODEL, IN_WIDTH), D_MODEL ** -0.5)
    attn_sinks = nrm((DEPTH, ATTN_HEADS), 0.5)
    dn_conv_w = nrm((DEPTH, DN_CONV, 3 * DN_WIDTH), DN_CONV ** -0.5)
    dn_a_log = jnp.log(jax.random.uniform(next(ks), (DEPTH, DN_HEADS), f32, 1.0, 16.0))
    dn_dt = jnp.exp(jax.random.uniform(next(ks), (DEPTH, DN_HEADS), f32, math.log(1e-3), math.log(1e-1)))
    dn_dt_bias = dn_dt + jnp.log(-jnp.expm1(-dn_dt))
    dn_norm_w = gain(DN_HEAD_DIM)
    s5_a_re = -0.5 + nrm((DEPTH, S5_GROUPS, S5_STATE), 0.01)
    s5_a_im = math.pi * jnp.arange(S5_STATE, dtype=f32)[None, None, :] + nrm((DEPTH, S5_GROUPS, S5_STATE), 0.01)
    s5_log_dt = jax.random.uniform(next(ks), (DEPTH, S5_GROUPS), f32, math.log(1e-3), math.log(1e-1))
    s5_b_re = nrm((DEPTH, S5_GROUPS, S5_STATE, S5_GROUP_CH), (2 * S5_GROUP_CH) ** -0.5)
    s5_b_im = nrm((DEPTH, S5_GROUPS, S5_STATE, S5_GROUP_CH), (2 * S5_GROUP_CH) ** -0.5)
    s5_c_re = nrm((DEPTH, S5_GROUPS, S5_GROUP_CH, S5_STATE), (2 * S5_STATE) ** -0.5)
    s5_c_im = nrm((DEPTH, S5_GROUPS, S5_GROUP_CH, S5_STATE), (2 * S5_STATE) ** -0.5)
    s5_d = nrm((DEPTH, S5_WIDTH), 1.0)
    s5_glu_w = nrm((DEPTH, S5_WIDTH, S5_WIDTH), S5_WIDTH ** -0.5)
    s5_glu_b = nrm((DEPTH, S5_WIDTH), 0.01)
    w_out = nrm((DEPTH, MIX_WIDTH, D_MODEL), MIX_WIDTH ** -0.5)
    mix_norm_post = gain()
    ff2_norm_pre = gain()
    ff2_w_gate = nrm((DEPTH, D_MODEL, D_FF), D_MODEL ** -0.5)
    ff2_w_up = nrm((DEPTH, D_MODEL, D_FF), D_MODEL ** -0.5)
    ff2_w_down = nrm((DEPTH, D_FF, D_MODEL), D_FF ** -0.5)
    ff2_norm_post = gain()
    return {"x": x, "ff1_norm_pre": ff1_norm_pre, "ff1_w_gate": ff1_w_gate, "ff1_w_up": ff1_w_up,
            "ff1_w_down": ff1_w_down, "ff1_norm_post": ff1_norm_post, "mix_norm_pre": mix_norm_pre,
            "w_in": w_in, "attn_sinks": attn_sinks, "dn_conv_w": dn_conv_w, "dn_a_log": dn_a_log,
            "dn_dt_bias": dn_dt_bias, "dn_norm_w": dn_norm_w, "s5_a_re": s5_a_re, "s5_a_im": s5_a_im,
            "s5_log_dt": s5_log_dt, "s5_b_re": s5_b_re, "s5_b_im": s5_b_im, "s5_c_re": s5_c_re,
            "s5_c_im": s5_c_im, "s5_d": s5_d, "s5_glu_w": s5_glu_w, "s5_glu_b": s5_glu_b, "w_out": w_out,
            "mix_norm_post": mix_norm_post, "ff2_norm_pre": ff2_norm_pre, "ff2_w_gate": ff2_w_gate,
            "ff2_w_up": ff2_w_up, "ff2_w_down": ff2_w_down, "ff2_norm_post": ff2_norm_post}


def reference(x, ff1_norm_pre, ff1_w_gate, ff1_w_up, ff1_w_down, ff1_norm_post, mix_norm_pre, w_in,
              attn_sinks, dn_conv_w, dn_a_log, dn_dt_bias, dn_norm_w, s5_a_re, s5_a_im, s5_log_dt,
              s5_b_re, s5_b_im, s5_c_re, s5_c_im, s5_d, s5_glu_w, s5_glu_b, w_out, mix_norm_post,
              ff2_norm_pre, ff2_w_gate, ff2_w_up, ff2_w_down, ff2_norm_post):
    b, s, _ = x.shape
    cos, sin = rope_tables(s)
    offsets = np.cumsum(IN_SPLITS)[:-1].tolist()
    for l in range(DEPTH):
        h = rms_norm(x, ff1_norm_pre[l])
        x = x + FFN_RES_WEIGHT * rms_norm(swiglu(h, ff1_w_gate[l], ff1_w_up[l], ff1_w_down[l]), ff1_norm_post[l])
        h = rms_norm(x, mix_norm_pre[l])
        z = h @ w_in[l]
        aq, ak, av, dn_qkv, dn_z, dn_b, dn_a, s5_u = jnp.split(z, offsets, axis=-1)
        aq = apply_rope(aq.reshape(b, s, ATTN_HEADS, HEAD_DIM), cos, sin)
        ak = apply_rope(ak.reshape(b, s, ATTN_KV_HEADS, HEAD_DIM), cos, sin)
        av = av.reshape(b, s, ATTN_KV_HEADS, HEAD_DIM)
        y_attn = sliding_window_attention(aq, ak, av, attn_sinks[l])
        y_dn = gated_deltanet(dn_qkv, dn_z, dn_b, dn_a, dn_conv_w[l], dn_a_log[l],
                              dn_dt_bias[l], dn_norm_w[l])
        y_s5 = s5_mixer(s5_u, s5_a_re[l], s5_a_im[l], s5_log_dt[l], s5_b_re[l], s5_b_im[l],
                        s5_c_re[l], s5_c_im[l], s5_d[l], s5_glu_w[l], s5_glu_b[l])
        mixed = jnp.concatenate([y_attn, y_dn, y_s5], axis=-1) @ w_out[l]
        x = x + rms_norm(mixed, mix_norm_post[l])
        h = rms_norm(x, ff2_norm_pre[l])
        x = x + FFN_RES_WEIGHT * rms_norm(swiglu(h, ff2_w_gate[l], ff2_w_up[l], ff2_w_down[l]), ff2_norm_post[l])
    return x
```

```python
import functools
import math

import jax
import jax.numpy as jnp
from jax import lax
from jax.experimental import pallas as pl
from jax.experimental.pallas import tpu as pltpu

F32 = jnp.float32
BF16 = jnp.bfloat16
HIGHEST = lax.Precision.HIGHEST

NORM_EPS = 1e-6
FFN_RES_WEIGHT = 0.5
ROPE_THETA = 10000.0

HEAD_DIM = 128
WINDOW = 128
ATTN_HEADS = 8
ATTN_KV_HEADS = 2
ATTN_GROUP = ATTN_HEADS // ATTN_KV_HEADS
DN_HEADS = 4
DN_CONV = 4
DN_CHUNK = 64
S5_GROUPS = 32
S5_GROUP_CH = 16
S5_STATE = 64
ATTN_WIDTH = ATTN_HEADS * HEAD_DIM
ATTN_KV_WIDTH = ATTN_KV_HEADS * HEAD_DIM
DN_WIDTH = DN_HEADS * HEAD_DIM
S5_WIDTH = S5_GROUPS * S5_GROUP_CH
S5_LANES = S5_GROUPS * S5_STATE
MIX_WIDTH = ATTN_WIDTH + DN_WIDTH + S5_WIDTH

LANE = 128
Z_WIDTH = ATTN_WIDTH + 2 * ATTN_KV_WIDTH + 4 * DN_WIDTH + S5_WIDTH + LANE
Z_BA_BLOCK = (Z_WIDTH - LANE) // LANE
SRC_BA = ATTN_WIDTH + 2 * ATTN_KV_WIDTH + 4 * DN_WIDTH

SEQ_TILE = 512
S5_SEG = SEQ_TILE // 8
VMEM_LIMIT = 56 * 1024 * 1024


def _params(*sem):
    return pltpu.CompilerParams(dimension_semantics=sem, vmem_limit_bytes=VMEM_LIMIT)


def _rms(x, gain):
    return x * lax.rsqrt(jnp.mean(x * x, axis=-1, keepdims=True) + NORM_EPS) * gain


def _sigmoid(x):
    return 1.0 / (1.0 + jnp.exp(-x))


def _ffn_kernel(x_ref, gpre_ref, wg_ref, wu_ref, wd_ref, gpost_ref, o_ref, h_ref):
    j = pl.program_id(1)

    @pl.when(j == 0)
    def _():
        h_ref[...] = _rms(x_ref[...], gpre_ref[...]).astype(BF16)

    h = h_ref[...]
    g = jnp.dot(h, wg_ref[...], preferred_element_type=F32)
    u = jnp.dot(h, wu_ref[...], preferred_element_type=F32)
    a = (g * _sigmoid(g) * u).astype(BF16)
    part = jnp.dot(a, wd_ref[...], preferred_element_type=F32)

    @pl.when(j == 0)
    def _():
        o_ref[...] = part

    @pl.when(j > 0)
    def _():
        o_ref[...] += part

    @pl.when(j == pl.num_programs(1) - 1)
    def _():
        o_ref[...] = x_ref[...] + FFN_RES_WEIGHT * _rms(o_ref[...], gpost_ref[...])


def _ffn(x, gpre, wg, wu, wd, gpost, *, tm, tf):
    n, d = x.shape
    f = wg.shape[1]
    return pl.pallas_call(
        _ffn_kernel,
        grid=(n // tm, f // tf),
        in_specs=[
            pl.BlockSpec((tm, d), lambda i, j: (i, 0)),
            pl.BlockSpec((1, d), lambda i, j: (0, 0)),
            pl.BlockSpec((d, tf), lambda i, j: (0, j)),
            pl.BlockSpec((d, tf), lambda i, j: (0, j)),
            pl.BlockSpec((tf, d), lambda i, j: (j, 0)),
            pl.BlockSpec((1, d), lambda i, j: (0, 0)),
        ],
        out_specs=pl.BlockSpec((tm, d), lambda i, j: (i, 0)),
        out_shape=jax.ShapeDtypeStruct((n, d), F32),
        scratch_shapes=[pltpu.VMEM((tm, d), BF16)],
        compiler_params=_params("parallel", "arbitrary"),
        name="ffn",
    )(x, gpre, wg, wu, wd, gpost)


def _inproj_kernel(x_ref, g_ref, w_ref, o_ref, h_ref):
    @pl.when(pl.program_id(1) == 0)
    def _():
        h_ref[...] = _rms(x_ref[...], g_ref[...]).astype(BF16)

    o_ref[...] = jnp.dot(h_ref[...], w_ref[...], preferred_element_type=F32)


def _inproj(x, gain, w, *, tm, tn):
    n, d = x.shape
    zw = w.shape[1]
    return pl.pallas_call(
        _inproj_kernel,
        grid=(n // tm, zw // tn),
        in_specs=[
            pl.BlockSpec((tm, d), lambda i, j: (i, 0)),
            pl.BlockSpec((1, d), lambda i, j: (0, 0)),
            pl.BlockSpec((d, tn), lambda i, j: (0, j)),
        ],
        out_specs=pl.BlockSpec((tm, tn), lambda i, j: (i, j)),
        out_shape=jax.ShapeDtypeStruct((n, zw), F32),
        scratch_shapes=[pltpu.VMEM((tm, d), BF16)],
        compiler_params=_params("parallel", "arbitrary"),
        name="inproj",
    )(x, gain, w)


def _outproj_kernel(x_ref, ya_ref, yd_ref, ys_ref, w_ref, g_ref, o_ref):
    ys = [ys_ref[j].astype(BF16) for j in range(ys_ref.shape[0])]
    y = jnp.concatenate([ya_ref[...], yd_ref[...]] + ys, axis=-1)
    mixed = jnp.dot(y, w_ref[...], preferred_element_type=F32)
    o_ref[...] = x_ref[...] + _rms(mixed, g_ref[...])


def _outproj(x, ya, yd, ys, w, gain, *, tm):
    n, d = x.shape
    return pl.pallas_call(
        _outproj_kernel,
        grid=(n // tm,),
        in_specs=[
            pl.BlockSpec((tm, d), lambda i: (i, 0)),
            pl.BlockSpec((tm, ya.shape[1]), lambda i: (i, 0)),
            pl.BlockSpec((tm, yd.shape[1]), lambda i: (i, 0)),
            pl.BlockSpec((ys.shape[0], tm, ys.shape[2]), lambda i: (0, i, 0)),
            pl.BlockSpec(w.shape, lambda i: (0, 0)),
            pl.BlockSpec((1, d), lambda i: (0, 0)),
        ],
        out_specs=pl.BlockSpec((tm, d), lambda i: (i, 0)),
        out_shape=jax.ShapeDtypeStruct((n, d), F32),
        compiler_params=_params("parallel"),
        name="outproj",
    )(x, ya, yd, ys, w, gain)


def _rope(x, cc, ss):
    return x * cc + pltpu.roll(x, HEAD_DIM // 2, axis=1) * ss


def _attn_kernel(sink_ref, q_ref, kp_ref, kc_ref, vp_ref, vc_ref, ccp_ref, ccc_ref, ssp_ref, ssc_ref, o_ref):
    h = pl.program_id(1)
    n = pl.program_id(2)
    w = WINDOW
    grp = ATTN_GROUP
    ccc = ccc_ref[...]
    ssc = ssc_ref[...]
    q = q_ref[...]
    qs = jnp.concatenate([_rope(q[:, g * HEAD_DIM:(g + 1) * HEAD_DIM], ccc, ssc) for g in range(grp)], axis=0)
    qs = (qs * (HEAD_DIM ** -0.5)).astype(BF16)
    kk = jnp.concatenate([_rope(kp_ref[...], ccp_ref[...], ssp_ref[...]), _rope(kc_ref[...], ccc, ssc)], axis=0)
    vv = jnp.concatenate([vp_ref[...], vc_ref[...]], axis=0).astype(BF16)
    scores = lax.dot_general(qs, kk.astype(BF16), (((1,), (1,)), ((), ())), preferred_element_type=F32)
    qi = lax.broadcasted_iota(jnp.int32, (w, 2 * w), 0) + w
    kj = lax.broadcasted_iota(jnp.int32, (w, 2 * w), 1)
    rel = qi - kj
    mask = (rel >= 0) & (rel < w) & ((kj >= w) | (n > 0))
    mask = jnp.concatenate([mask] * grp, axis=0)
    scores = jnp.where(mask, scores, -jnp.inf)
    row_head = lax.broadcasted_iota(jnp.int32, (grp * w, 1), 0) // w
    sink = jnp.zeros((grp * w, 1), F32)
    for g in range(grp):
        sink = jnp.where(row_head == g, sink_ref[h * grp + g], sink)
    m = jnp.maximum(jnp.max(scores, axis=-1, keepdims=True), sink)
    p = jnp.exp(scores - m)
    denom = jnp.sum(p, axis=-1, keepdims=True) + jnp.exp(sink - m)
    p = (p / denom).astype(BF16)
    out = jnp.dot(p, vv, preferred_element_type=F32)
    o_ref[...] = jnp.concatenate([out[g * w:(g + 1) * w] for g in range(grp)], axis=1).astype(o_ref.dtype)


def _attention(z3, sinks, cc, ss):
    b, s, _ = z3.shape
    w = WINDOW
    nb = s // w
    qw = ATTN_GROUP * HEAD_DIM
    kblk = ATTN_WIDTH // HEAD_DIM
    vblk = (ATTN_WIDTH + ATTN_KV_WIDTH) // HEAD_DIM
    prev = lambda n: jnp.maximum(n - 1, 0)
    return pl.pallas_call(
        _attn_kernel,
        grid=(b, ATTN_KV_HEADS, nb),
        in_specs=[
            pl.BlockSpec(memory_space=pltpu.SMEM),
            pl.BlockSpec((None, w, qw), lambda bi, h, n: (bi, n, h)),
            pl.BlockSpec((None, w, HEAD_DIM), lambda bi, h, n: (bi, prev(n), kblk + h)),
            pl.BlockSpec((None, w, HEAD_DIM), lambda bi, h, n: (bi, n, kblk + h)),
            pl.BlockSpec((None, w, HEAD_DIM), lambda bi, h, n: (bi, prev(n), vblk + h)),
            pl.BlockSpec((None, w, HEAD_DIM), lambda bi, h, n: (bi, n, vblk + h)),
            pl.BlockSpec((w, HEAD_DIM), lambda bi, h, n: (prev(n), 0)),
            pl.BlockSpec((w, HEAD_DIM), lambda bi, h, n: (n, 0)),
            pl.BlockSpec((w, HEAD_DIM), lambda bi, h, n: (prev(n), 0)),
            pl.BlockSpec((w, HEAD_DIM), lambda bi, h, n: (n, 0)),
        ],
        out_specs=pl.BlockSpec((None, w, qw), lambda bi, h, n: (bi, n, h)),
        out_shape=jax.ShapeDtypeStruct((b, s, ATTN_WIDTH), BF16),
        compiler_params=_params("parallel", "parallel", "arbitrary"),
        name="swa",
    )(sinks, z3, z3, z3, z3, z3, cc, cc, ss, ss)


def _silu(x):
    return x * _sigmoid(x)


def _softplus(x):
    return jnp.maximum(x, 0.0) + jnp.log(1.0 + jnp.exp(-jnp.abs(x)))


def _dot_nt(a, b, precision=None):
    return lax.dot_general(a, b, (((1,), (1,)), ((), ())), preferred_element_type=F32, precision=precision)


def _dot_tn(a, b, precision=None):
    return lax.dot_general(a, b, (((0,), (0,)), ((), ())), preferred_element_type=F32, precision=precision)


def _dn_kernel(q_ref, k_ref, v_ref, zg_ref, ba_ref, cw_ref, alog_ref, dtb_ref, nw_ref, o_ref,
               ext_ref, qn_ref, kn_ref, vn_ref, g_ref, beta_ref, s_ref):
    t = pl.program_id(1)
    tile = q_ref.shape[0]
    c = DN_CHUNK
    d = HEAD_DIM
    nh = DN_HEADS
    wdt = nh * d

    @pl.when(t == 0)
    def _():
        ext_ref[0:8, :] = jnp.zeros((8, 3 * wdt), F32)
        s_ref[...] = jnp.zeros_like(s_ref)

    ext_ref[8:8 + tile, 0:wdt] = q_ref[...]
    ext_ref[8:8 + tile, wdt:2 * wdt] = k_ref[...]
    ext_ref[8:8 + tile, 2 * wdt:3 * wdt] = v_ref[...]
    conv = jnp.zeros((tile, 3 * wdt), F32)
    for tap in range(DN_CONV):
        off = 8 - (DN_CONV - 1) + tap
        conv = conv + ext_ref[off:off + tile, :] * cw_ref[tap:tap + 1, :]
    ext_ref[0:8, :] = ext_ref[tile:tile + 8, :]
    qkv = _silu(conv)
    for hd in range(nh):
        qh = qkv[:, hd * d:(hd + 1) * d]
        kh = qkv[:, wdt + hd * d:wdt + (hd + 1) * d]
        qn_ref[:, hd * d:(hd + 1) * d] = qh * lax.rsqrt(jnp.sum(qh * qh, axis=-1, keepdims=True) + NORM_EPS) * (d ** -0.5)
        kn_ref[:, hd * d:(hd + 1) * d] = kh * lax.rsqrt(jnp.sum(kh * kh, axis=-1, keepdims=True) + NORM_EPS)
    vn_ref[...] = qkv[:, 2 * wdt:3 * wdt]
    ba = ba_ref[...]
    beta_ref[...] = _sigmoid(ba)
    g_ref[...] = -jnp.exp(alog_ref[...]) * _softplus(ba + dtb_ref[...])

    row = lax.broadcasted_iota(jnp.int32, (c, c), 0)
    col = lax.broadcasted_iota(jnp.int32, (c, c), 1)
    tril = (row >= col).astype(F32)
    strict = (row > col).astype(F32)
    eye = (row == col).astype(F32)
    nw = nw_ref[...]

    def chunk(ci, carry):
        r0 = pl.multiple_of(ci * c, c)
        gall = g_ref[pl.ds(r0, c), :]
        ball = beta_ref[pl.ds(r0, c), :]
        for hd in range(nh):
            lanes = slice(hd * d, (hd + 1) * d)
            q = qn_ref[pl.ds(r0, c), lanes]
            k = kn_ref[pl.ds(r0, c), lanes]
            v = vn_ref[pl.ds(r0, c), lanes]
            beta = ball[:, hd:hd + 1]
            gb = jnp.broadcast_to(gall[:, nh + hd:nh + hd + 1], (c, c))
            gc = jnp.dot(tril, gb, preferred_element_type=F32, precision=HIGHEST)
            dlt = jnp.dot(tril, gb * strict, preferred_element_type=F32, precision=HIGHEST)
            decay = jnp.where(row >= col, jnp.exp(dlt), 0.0)
            gcol = gc[:, 0:1]
            glast = gc[c - 1:c, 0:1]
            eg = jnp.exp(gcol)
            kb = k * beta
            nmat = _dot_nt(kb, k, HIGHEST) * decay * strict
            inv = eye - nmat
            pw = nmat
            for _ in range(int(math.log2(c)) - 1):
                pw = jnp.dot(pw, pw, preferred_element_type=F32, precision=HIGHEST)
                inv = inv + jnp.dot(inv, pw, preferred_element_type=F32, precision=HIGHEST)
            u = jnp.dot(inv, v * beta, preferred_element_type=F32, precision=HIGHEST)
            wm = jnp.dot(inv, kb * eg, preferred_element_type=F32, precision=HIGHEST)
            attn = _dot_nt(q, k, HIGHEST) * decay
            state = s_ref[hd]
            v_new = u - jnp.dot(wm, state, preferred_element_type=F32, precision=HIGHEST)
            o = (jnp.dot(q * eg, state, preferred_element_type=F32, precision=HIGHEST)
                 + jnp.dot(attn, v_new, preferred_element_type=F32, precision=HIGHEST))
            k_dec = k * jnp.exp(glast - gcol)
            s_ref[hd] = state * jnp.exp(glast) + _dot_tn(k_dec, v_new, HIGHEST)
            o = o * lax.rsqrt(jnp.mean(o * o, axis=-1, keepdims=True) + NORM_EPS) * nw
            o = o * _silu(zg_ref[pl.ds(r0, c), lanes])
            o_ref[pl.ds(r0, c), lanes] = o.astype(o_ref.dtype)
        return carry

    lax.fori_loop(0, tile // c, chunk, 0)


def _deltanet(z3, conv_w, alog_pad, dtb_pad, norm_w, *, tile):
    b, s, _ = z3.shape
    wdt = DN_WIDTH
    qblk = (ATTN_WIDTH + 2 * ATTN_KV_WIDTH) // wdt
    seq = lambda off: pl.BlockSpec((None, tile, wdt), lambda bi, t: (bi, t, qblk + off))
    const = lambda shape: pl.BlockSpec(shape, lambda bi, t: (0, 0))
    return pl.pallas_call(
        _dn_kernel,
        grid=(b, s // tile),
        in_specs=[
            seq(0), seq(1), seq(2), seq(3),
            pl.BlockSpec((None, tile, LANE), lambda bi, t: (bi, t, Z_BA_BLOCK)),
            const(conv_w.shape), const((1, LANE)), const((1, LANE)), const((1, HEAD_DIM)),
        ],
        out_specs=pl.BlockSpec((None, tile, wdt), lambda bi, t: (bi, t, 0)),
        out_shape=jax.ShapeDtypeStruct((b, s, wdt), BF16),
        scratch_shapes=[
            pltpu.VMEM((tile + 8, 3 * wdt), F32),
            pltpu.VMEM((tile, wdt), F32),
            pltpu.VMEM((tile, wdt), F32),
            pltpu.VMEM((tile, wdt), F32),
            pltpu.VMEM((tile, LANE), F32),
            pltpu.VMEM((tile, LANE), F32),
            pltpu.VMEM((DN_HEADS, HEAD_DIM, HEAD_DIM), F32),
        ],
        compiler_params=_params("parallel", "arbitrary"),
        name="deltanet",
    )(z3, z3, z3, z3, z3, conv_w, alog_pad, dtb_pad, norm_w)


def _s5_prep_kernel(are_ref, aim_ref, ldt_ref, bre_ref, bim_ref, apr_ref, api_ref, bbr_ref, bbi_ref):
    lr = are_ref[...]
    li = aim_ref[...]
    dt = jnp.exp(ldt_ref[...])
    step = (lax.broadcasted_iota(jnp.int32, (S5_SEG, 1), 0) + 1).astype(F32)
    mag = jnp.exp(step * (lr * dt))
    ang = step * (li * dt)
    apr_ref[...] = mag * jnp.cos(ang)
    api_ref[...] = mag * jnp.sin(ang)
    m1 = jnp.exp(lr * dt)
    nr = m1 * jnp.cos(li * dt) - 1.0
    ni = m1 * jnp.sin(li * dt)
    den = lr * lr + li * li
    cr = (nr * lr + ni * li) / den
    ci = (ni * lr - nr * li) / den
    br = bre_ref[...]
    bi = bim_ref[...]
    bbr_ref[...] = cr * br - ci * bi
    bbi_ref[...] = cr * bi + ci * br


def _s5_prep(a_re, a_im, log_dt, b_re, b_im):
    row = lambda a: a.reshape(1, S5_LANES)
    ldt = jnp.repeat(log_dt, S5_STATE).reshape(1, S5_LANES)
    tr = lambda a: a.reshape(S5_LANES, S5_GROUP_CH).T
    shp = jax.ShapeDtypeStruct
    return pl.pallas_call(
        _s5_prep_kernel,
        out_shape=(shp((S5_SEG, S5_LANES), F32), shp((S5_SEG, S5_LANES), F32),
                   shp((S5_GROUP_CH, S5_LANES), F32), shp((S5_GROUP_CH, S5_LANES), F32)),
        name="s5_prep",
    )(row(a_re), row(a_im), ldt, tr(b_re), tr(b_im))


def _s5_kernel(u0_ref, u1_ref, u2_ref, u3_ref, bre_ref, bim_ref, cre_ref, cim_ref, apr_ref, api_ref, d_ref,
               gw_ref, gb_ref, o_ref, up_ref, xr_ref, xi_ref, cr_ref, ci_ref, st_ref, y_ref):
    t = pl.program_id(1)
    tile = u0_ref.shape[0]
    seg = tile // 8
    strip = 512

    @pl.when(t == 0)
    def _():
        st_ref[...] = jnp.zeros_like(st_ref)

    for j, u_ref in enumerate((u0_ref, u1_ref, u2_ref, u3_ref)):
        for k in range(seg):
            up_ref[8 * k:8 * k + 8, j * LANE:(j + 1) * LANE] = u_ref[pl.ds(k, 8, stride=seg), :]
    ub = up_ref[...].astype(BF16)
    xr_ref[...] = jnp.dot(ub, bre_ref[...], preferred_element_type=F32)
    xi_ref[...] = jnp.dot(ub, bim_ref[...], preferred_element_type=F32)

    for s0 in range(0, S5_LANES, strip):
        lanes = slice(s0, s0 + strip)
        ar = jnp.broadcast_to(apr_ref[0:1, lanes], (8, strip))
        ai = jnp.broadcast_to(api_ref[0:1, lanes], (8, strip))

        def scan(k, carry):
            pr, pi = carry
            r0 = pl.multiple_of(k * 8, 8)
            nr = ar * pr - ai * pi + xr_ref[pl.ds(r0, 8), lanes]
            ni = ar * pi + ai * pr + xi_ref[pl.ds(r0, 8), lanes]
            xr_ref[pl.ds(r0, 8), lanes] = nr
            xi_ref[pl.ds(r0, 8), lanes] = ni
            return nr, ni

        zero = jnp.zeros((8, strip), F32)
        fr, fi = lax.fori_loop(0, seg, scan, (zero, zero))

        a64r = apr_ref[seg - 1:seg, lanes]
        a64i = api_ref[seg - 1:seg, lanes]
        c_r = st_ref[0:1, lanes]
        c_i = st_ref[1:2, lanes]
        for r in range(8):
            cr_ref[r:r + 1, lanes] = c_r
            ci_ref[r:r + 1, lanes] = c_i
            n_r = a64r * c_r - a64i * c_i + fr[r:r + 1]
            n_i = a64r * c_i + a64i * c_r + fi[r:r + 1]
            c_r, c_i = n_r, n_i
        st_ref[0:1, lanes] = c_r
        st_ref[1:2, lanes] = c_i

        cin_r = cr_ref[:, lanes]
        cin_i = ci_ref[:, lanes]

        def fix(k, carry):
            r0 = pl.multiple_of(k * 8, 8)
            pr = apr_ref[pl.ds(k, 1), lanes]
            pi = api_ref[pl.ds(k, 1), lanes]
            xr_ref[pl.ds(r0, 8), lanes] += pr * cin_r - pi * cin_i
            xi_ref[pl.ds(r0, 8), lanes] += pr * cin_i + pi * cin_r
            return carry

        lax.fori_loop(0, seg, fix, 0)

    y = (jnp.dot(xr_ref[...].astype(BF16), cre_ref[...], preferred_element_type=F32)
         - jnp.dot(xi_ref[...].astype(BF16), cim_ref[...], preferred_element_type=F32))
    y = y + d_ref[...] * up_ref[...]
    y = 0.5 * y * (1.0 + jnp.tanh(math.sqrt(2.0 / math.pi) * (y + 0.044715 * (y * y * y))))
    gate = jnp.dot(y.astype(BF16), gw_ref[...], preferred_element_type=F32) + gb_ref[...]
    y_ref[...] = y * _sigmoid(gate)
    for j in range(S5_WIDTH // LANE):
        for k in range(seg):
            o_ref[j, pl.ds(k, 8, stride=seg), :] = y_ref[8 * k:8 * k + 8, j * LANE:(j + 1) * LANE]


def _s5(z3, b_re, b_im, c_re, c_im, ap_re, ap_im, d_skip, glu_w, glu_b, *, tile):
    b, s, _ = z3.shape
    nblk = S5_WIDTH // LANE
    ublk = (ATTN_WIDTH + 2 * ATTN_KV_WIDTH + 4 * DN_WIDTH) // LANE
    const = lambda a: pl.BlockSpec(a.shape, lambda bi, t: (0, 0))
    ucol = lambda j: pl.BlockSpec((None, tile, LANE), lambda bi, t: (bi, t, ublk + j))
    return pl.pallas_call(
        _s5_kernel,
        grid=(b, s // tile),
        in_specs=[
            ucol(0), ucol(1), ucol(2), ucol(3),
            const(b_re), const(b_im), const(c_re), const(c_im), const(ap_re), const(ap_im),
            const(d_skip), const(glu_w), const(glu_b),
        ],
        out_specs=pl.BlockSpec((nblk, None, tile, LANE), lambda bi, t: (0, bi, t, 0)),
        out_shape=jax.ShapeDtypeStruct((nblk, b, s, LANE), F32),
        scratch_shapes=[
            pltpu.VMEM((tile, S5_WIDTH), F32),
            pltpu.VMEM((tile, S5_LANES), F32),
            pltpu.VMEM((tile, S5_LANES), F32),
            pltpu.VMEM((8, S5_LANES), F32),
            pltpu.VMEM((8, S5_LANES), F32),
            pltpu.VMEM((8, S5_LANES), F32),
            pltpu.VMEM((tile, S5_WIDTH), F32),
        ],
        compiler_params=_params("parallel", "arbitrary"),
        name="s5",
    )(z3, z3, z3, z3, b_re, b_im, c_re, c_im, ap_re, ap_im, d_skip, glu_w, glu_b)


def _block_diag_in(bb):
    eye = jnp.eye(S5_GROUPS, dtype=bb.dtype)
    t = bb.reshape(S5_GROUP_CH, S5_GROUPS, S5_STATE)
    return jnp.einsum('hgp,fg->fhgp', t, eye).reshape(S5_WIDTH, S5_LANES)


def _block_diag_out(cc):
    eye = jnp.eye(S5_GROUPS, dtype=cc.dtype)
    return jnp.einsum('ghp,gf->gpfh', cc, eye).reshape(S5_LANES, S5_WIDTH)


def _rope_tables(seq):
    half = HEAD_DIM // 2
    inv_freq = ROPE_THETA ** (-jnp.arange(half, dtype=F32) / half)
    ang = jnp.arange(seq, dtype=F32)[:, None] * inv_freq[None, :]
    cos, sin = jnp.cos(ang), jnp.sin(ang)
    return jnp.concatenate([cos, cos], axis=-1), jnp.concatenate([-sin, sin], axis=-1)


def _reorder_w_in(w_in):
    s5_src = SRC_BA + 2 * DN_HEADS
    pad = jnp.zeros(w_in.shape[:-1] + (LANE - 2 * DN_HEADS,), w_in.dtype)
    return jnp.concatenate([w_in[..., :SRC_BA], w_in[..., s5_src:], w_in[..., SRC_BA:s5_src], pad], axis=-1).astype(BF16)


def _lane_pad(v, offset):
    return jnp.zeros((v.shape[0], 1, LANE), F32).at[:, 0, offset:offset + v.shape[1]].set(v)


def kernel(x, ff1_norm_pre, ff1_w_gate, ff1_w_up, ff1_w_down, ff1_norm_post, mix_norm_pre, w_in,
           attn_sinks, dn_conv_w, dn_a_log, dn_dt_bias, dn_norm_w, s5_a_re, s5_a_im, s5_log_dt,
           s5_b_re, s5_b_im, s5_c_re, s5_c_im, s5_d, s5_glu_w, s5_glu_b, w_out, mix_norm_post,
           ff2_norm_pre, ff2_w_gate, ff2_w_up, ff2_w_down, ff2_norm_post):
    b, s, d = x.shape
    depth = w_in.shape[0]
    n = b * s
    tm = min(512, n)
    tf = 512
    tile = min(SEQ_TILE, s)

    bf = lambda a: a.astype(BF16)
    ff1 = (bf(ff1_w_gate), bf(ff1_w_up), bf(ff1_w_down))
    ff2 = (bf(ff2_w_gate), bf(ff2_w_up), bf(ff2_w_down))
    w_in_r = _reorder_w_in(w_in)
    w_out_b = bf(w_out)
    glu_w_b = bf(s5_glu_w)
    cc, ss = _rope_tables(s)
    alog_pad = _lane_pad(dn_a_log, DN_HEADS)
    dtb_pad = _lane_pad(dn_dt_bias, DN_HEADS)
    row = lambda a, l: a[l].reshape(1, -1)

    xf = x.reshape(n, d)
    for l in range(depth):
        xf = _ffn(xf, row(ff1_norm_pre, l), ff1[0][l], ff1[1][l], ff1[2][l], row(ff1_norm_post, l), tm=tm, tf=tf)

        z3 = _inproj(xf, row(mix_norm_pre, l), w_in_r[l], tm=tm, tn=Z_WIDTH // 3).reshape(b, s, Z_WIDTH)
        y_attn = _attention(z3, attn_sinks[l], cc, ss)
        y_dn = _deltanet(z3, dn_conv_w[l], alog_pad[l], dtb_pad[l], row(dn_norm_w, l), tile=tile)
        ap_re, ap_im, bb_re, bb_im = _s5_prep(s5_a_re[l], s5_a_im[l], s5_log_dt[l], s5_b_re[l], s5_b_im[l])
        y_s5 = _s5(z3, bf(_block_diag_in(bb_re)), bf(_block_diag_in(bb_im)),
                   bf(_block_diag_out(s5_c_re[l])), bf(_block_diag_out(s5_c_im[l])),
                   ap_re, ap_im, row(s5_d, l), glu_w_b[l], row(s5_glu_b, l), tile=tile)
        xf = _outproj(xf, y_attn.reshape(n, -1), y_dn.reshape(n, -1), y_s5.reshape(-1, n, LANE),
                      w_out_b[l], row(mix_norm_post, l), tm=tm)

        xf = _ffn(xf, row(ff2_norm_pre, l), ff2[0][l], ff2[1][l], ff2[2][l], row(ff2_norm_post, l), tm=tm, tf=tf)
    return xf.reshape(b, s, d)
```

```python
import functools
import math

import jax
import jax.numpy as jnp
from jax import lax
from jax.experimental import pallas as pl
from jax.experimental.pallas import tpu as pltpu

F32 = jnp.float32
BF16 = jnp.bfloat16
HIGHEST = lax.Precision.HIGHEST

NORM_EPS = 1e-6
FFN_RES_WEIGHT = 0.5
ROPE_THETA = 10000.0

HEAD_DIM = 128
WINDOW = 128
ATTN_HEADS = 8
ATTN_KV_HEADS = 2
ATTN_GROUP = ATTN_HEADS // ATTN_KV_HEADS
DN_HEADS = 4
DN_CONV = 4
DN_CHUNK = 64
S5_GROUPS = 32
S5_GROUP_CH = 16
S5_STATE = 64
ATTN_WIDTH = ATTN_HEADS * HEAD_DIM
ATTN_KV_WIDTH = ATTN_KV_HEADS * HEAD_DIM
DN_WIDTH = DN_HEADS * HEAD_DIM
S5_WIDTH = S5_GROUPS * S5_GROUP_CH
S5_LANES = S5_GROUPS * S5_STATE
MIX_WIDTH = ATTN_WIDTH + DN_WIDTH + S5_WIDTH

LANE = 128
Z_WIDTH = ATTN_WIDTH + 2 * ATTN_KV_WIDTH + 4 * DN_WIDTH + S5_WIDTH + LANE
Z_BA_BLOCK = (Z_WIDTH - LANE) // LANE
SRC_BA = ATTN_WIDTH + 2 * ATTN_KV_WIDTH + 4 * DN_WIDTH

SEQ_TILE = 512
S5_SEG = SEQ_TILE // 8
VMEM_LIMIT = 56 * 1024 * 1024


def _params(*sem):
    return pltpu.CompilerParams(dimension_semantics=sem, vmem_limit_bytes=VMEM_LIMIT)


def _rms(x, gain):
    return x * lax.rsqrt(jnp.mean(x * x, axis=-1, keepdims=True) + NORM_EPS) * gain


def _sigmoid(x):
    return 1.0 / (1.0 + jnp.exp(-x))


def _ffn_kernel(x_ref, gpre_ref, wg_ref, wu_ref, wd_ref, gpost_ref, o_ref, h_ref):
    j = pl.program_id(1)

    @pl.when(j == 0)
    def _():
        h_ref[...] = _rms(x_ref[...], gpre_ref[...]).astype(BF16)
        o_ref[...] = jnp.zeros_like(o_ref)

    h = h_ref[...]
    g = jnp.dot(h, wg_ref[...], preferred_element_type=F32)
    u = jnp.dot(h, wu_ref[...], preferred_element_type=F32)
    a = (g * _sigmoid(g) * u).astype(BF16)
    o_ref[...] += jnp.dot(a, wd_ref[...], preferred_element_type=F32)

    @pl.when(j == pl.num_programs(1) - 1)
    def _():
        o_ref[...] = x_ref[...] + FFN_RES_WEIGHT * _rms(o_ref[...], gpost_ref[...])


def _ffn(x, gpre, wg, wu, wd, gpost, layer, *, tm, tf):
    n, d = x.shape
    f = wg.shape[2]
    return pl.pallas_call(
        _ffn_kernel,
        grid=(n // tm, f // tf),
        in_specs=[
            pl.BlockSpec((tm, d), lambda i, j: (i, 0)),
            pl.BlockSpec((1, d), lambda i, j: (0, 0)),
            pl.BlockSpec((None, d, tf), lambda i, j: (layer, 0, j)),
            pl.BlockSpec((None, d, tf), lambda i, j: (layer, 0, j)),
            pl.BlockSpec((None, tf, d), lambda i, j: (layer, j, 0)),
            pl.BlockSpec((1, d), lambda i, j: (0, 0)),
        ],
        out_specs=pl.BlockSpec((tm, d), lambda i, j: (i, 0)),
        out_shape=jax.ShapeDtypeStruct((n, d), F32),
        scratch_shapes=[pltpu.VMEM((tm, d), BF16)],
        compiler_params=_params("parallel", "arbitrary"),
        name="ffn",
    )(x, gpre, wg, wu, wd, gpost)


def _inproj_kernel(x_ref, g_ref, w_ref, o_ref, h_ref):
    @pl.when(pl.program_id(1) == 0)
    def _():
        h_ref[...] = _rms(x_ref[...], g_ref[...]).astype(BF16)

    o_ref[...] = jnp.dot(h_ref[...], w_ref[...], preferred_element_type=F32)


def _inproj(x, gain, w, layer, *, tm, tn):
    n, d = x.shape
    zw = w.shape[2]
    return pl.pallas_call(
        _inproj_kernel,
        grid=(n // tm, zw // tn),
        in_specs=[
            pl.BlockSpec((tm, d), lambda i, j: (i, 0)),
            pl.BlockSpec((1, d), lambda i, j: (0, 0)),
            pl.BlockSpec((None, d, tn), lambda i, j: (layer, 0, j)),
        ],
        out_specs=pl.BlockSpec((tm, tn), lambda i, j: (i, j)),
        out_shape=jax.ShapeDtypeStruct((n, zw), F32),
        scratch_shapes=[pltpu.VMEM((tm, d), BF16)],
        compiler_params=_params("parallel", "arbitrary"),
        name="inproj",
    )(x, gain, w)


def _outproj_kernel(x_ref, ya_ref, yd_ref, ys_ref, w_ref, g_ref, o_ref):
    ys = [ys_ref[j].astype(BF16) for j in range(ys_ref.shape[0])]
    y = jnp.concatenate([ya_ref[...], yd_ref[...]] + ys, axis=-1)
    mixed = jnp.dot(y, w_ref[...], preferred_element_type=F32)
    o_ref[...] = x_ref[...] + _rms(mixed, g_ref[...])


def _outproj(x, ya, yd, ys, w, gain, layer, *, tm):
    n, d = x.shape
    return pl.pallas_call(
        _outproj_kernel,
        grid=(n // tm,),
        in_specs=[
            pl.BlockSpec((tm, d), lambda i: (i, 0)),
            pl.BlockSpec((tm, ya.shape[1]), lambda i: (i, 0)),
            pl.BlockSpec((tm, yd.shape[1]), lambda i: (i, 0)),
            pl.BlockSpec((ys.shape[0], tm, ys.shape[2]), lambda i: (0, i, 0)),
            pl.BlockSpec((None,) + w.shape[1:], lambda i: (layer, 0, 0)),
            pl.BlockSpec((1, d), lambda i: (0, 0)),
        ],
        out_specs=pl.BlockSpec((tm, d), lambda i: (i, 0)),
        out_shape=jax.ShapeDtypeStruct((n, d), F32),
        compiler_params=_params("parallel"),
        name="outproj",
    )(x, ya, yd, ys, w, gain)


def _rope(x, cc, ss):
    return x * cc + pltpu.roll(x, HEAD_DIM // 2, axis=1) * ss


def _attn_kernel(sink_ref, q_ref, kp_ref, kc_ref, vp_ref, vc_ref, ccp_ref, ccc_ref, ssp_ref, ssc_ref, o_ref):
    h = pl.program_id(1)
    n = pl.program_id(2)
    w = WINDOW
    grp = ATTN_GROUP
    ccc = ccc_ref[...]
    ssc = ssc_ref[...]
    q = q_ref[...]
    qs = jnp.concatenate([_rope(q[:, g * HEAD_DIM:(g + 1) * HEAD_DIM], ccc, ssc) for g in range(grp)], axis=0)
    qs = (qs * (HEAD_DIM ** -0.5)).astype(BF16)
    kk = jnp.concatenate([_rope(kp_ref[...], ccp_ref[...], ssp_ref[...]), _rope(kc_ref[...], ccc, ssc)], axis=0)
    vv = jnp.concatenate([vp_ref[...], vc_ref[...]], axis=0).astype(BF16)
    scores = lax.dot_general(qs, kk.astype(BF16), (((1,), (1,)), ((), ())), preferred_element_type=F32)
    qi = lax.broadcasted_iota(jnp.int32, (w, 2 * w), 0) + w
    kj = lax.broadcasted_iota(jnp.int32, (w, 2 * w), 1)
    rel = qi - kj
    mask = (rel >= 0) & (rel < w) & ((kj >= w) | (n > 0))
    mask = jnp.concatenate([mask] * grp, axis=0)
    scores = jnp.where(mask, scores, -jnp.inf)
    row_head = lax.broadcasted_iota(jnp.int32, (grp * w, 1), 0) // w
    sink = jnp.zeros((grp * w, 1), F32)
    for g in range(grp):
        sink = jnp.where(row_head == g, sink_ref[h * grp + g], sink)
    m = jnp.maximum(jnp.max(scores, axis=-1, keepdims=True), sink)
    p = jnp.exp(scores - m)
    denom = jnp.sum(p, axis=-1, keepdims=True) + jnp.exp(sink - m)
    p = (p / denom).astype(BF16)
    out = jnp.dot(p, vv, preferred_element_type=F32)
    o_ref[...] = jnp.concatenate([out[g * w:(g + 1) * w] for g in range(grp)], axis=1).astype(o_ref.dtype)


def _attention(z3, sinks, cc, ss):
    b, s, _ = z3.shape
    w = WINDOW
    nb = s // w
    qw = ATTN_GROUP * HEAD_DIM
    kblk = ATTN_WIDTH // HEAD_DIM
    vblk = (ATTN_WIDTH + ATTN_KV_WIDTH) // HEAD_DIM
    prev = lambda n: jnp.maximum(n - 1, 0)
    return pl.pallas_call(
        _attn_kernel,
        grid=(b, ATTN_KV_HEADS, nb),
        in_specs=[
            pl.BlockSpec(memory_space=pltpu.SMEM),
            pl.BlockSpec((None, w, qw), lambda bi, h, n: (bi, n, h)),
            pl.BlockSpec((None, w, HEAD_DIM), lambda bi, h, n: (bi, prev(n), kblk + h)),
            pl.BlockSpec((None, w, HEAD_DIM), lambda bi, h, n: (bi, n, kblk + h)),
            pl.BlockSpec((None, w, HEAD_DIM), lambda bi, h, n: (bi, prev(n), vblk + h)),
            pl.BlockSpec((None, w, HEAD_DIM), lambda bi, h, n: (bi, n, vblk + h)),
            pl.BlockSpec((w, HEAD_DIM), lambda bi, h, n: (prev(n), 0)),
            pl.BlockSpec((w, HEAD_DIM), lambda bi, h, n: (n, 0)),
            pl.BlockSpec((w, HEAD_DIM), lambda bi, h, n: (prev(n), 0)),
            pl.BlockSpec((w, HEAD_DIM), lambda bi, h, n: (n, 0)),
        ],
        out_specs=pl.BlockSpec((None, w, qw), lambda bi, h, n: (bi, n, h)),
        out_shape=jax.ShapeDtypeStruct((b, s, ATTN_WIDTH), BF16),
        compiler_params=_params("parallel", "parallel", "arbitrary"),
        name="swa",
    )(sinks, z3, z3, z3, z3, z3, cc, cc, ss, ss)


def _silu(x):
    return x * _sigmoid(x)


def _softplus(x):
    return jnp.maximum(x, 0.0) + jnp.log(1.0 + jnp.exp(-jnp.abs(x)))


def _dot_nt(a, b, precision=None):
    return lax.dot_general(a, b, (((1,), (1,)), ((), ())), preferred_element_type=F32, precision=precision)


def _dot_tn(a, b, precision=None):
    return lax.dot_general(a, b, (((0,), (0,)), ((), ())), preferred_element_type=F32, precision=precision)


def _dn_kernel(q_ref, k_ref, v_ref, zg_ref, ba_ref, cw_ref, alog_ref, dtb_ref, nw_ref, o_ref,
               ext_ref, qn_ref, kn_ref, vn_ref, gc_ref, gct_ref, beta_ref, s_ref):
    t = pl.program_id(1)
    tile = q_ref.shape[0]
    c = DN_CHUNK
    d = HEAD_DIM
    nh = DN_HEADS
    wdt = nh * d

    @pl.when(t == 0)
    def _():
        ext_ref[0:8, :] = jnp.zeros((8, 3 * wdt), F32)
        s_ref[...] = jnp.zeros_like(s_ref)

    ext_ref[8:8 + tile, 0:wdt] = q_ref[...]
    ext_ref[8:8 + tile, wdt:2 * wdt] = k_ref[...]
    ext_ref[8:8 + tile, 2 * wdt:3 * wdt] = v_ref[...]
    conv = jnp.zeros((tile, 3 * wdt), F32)
    for tap in range(DN_CONV):
        off = 8 - (DN_CONV - 1) + tap
        conv = conv + ext_ref[off:off + tile, :] * cw_ref[tap:tap + 1, :]
    ext_ref[0:8, :] = ext_ref[tile:tile + 8, :]
    qkv = _silu(conv)
    for hd in range(nh):
        qh = qkv[:, hd * d:(hd + 1) * d]
        kh = qkv[:, wdt + hd * d:wdt + (hd + 1) * d]
        qn_ref[:, hd * d:(hd + 1) * d] = qh * lax.rsqrt(jnp.sum(qh * qh, axis=-1, keepdims=True) + NORM_EPS) * (d ** -0.5)
        kn_ref[:, hd * d:(hd + 1) * d] = kh * lax.rsqrt(jnp.sum(kh * kh, axis=-1, keepdims=True) + NORM_EPS)
    vn_ref[...] = qkv[:, 2 * wdt:3 * wdt]
    ba = ba_ref[...]
    beta_ref[...] = _sigmoid(ba)
    g = -jnp.exp(alog_ref[...]) * _softplus(ba + dtb_ref[...])
    ti = lax.broadcasted_iota(jnp.int32, (tile, tile), 0)
    tj = lax.broadcasted_iota(jnp.int32, (tile, tile), 1)
    csum = ((ti // c == tj // c) & (ti >= tj)).astype(F32)
    gc_all = jnp.dot(csum, g, preferred_element_type=F32, precision=HIGHEST)
    gc_ref[...] = gc_all
    gct_ref[...] = gc_all.T[0:8, :]

    row = lax.broadcasted_iota(jnp.int32, (c, c), 0)
    col = lax.broadcasted_iota(jnp.int32, (c, c), 1)
    causal = row >= col
    strict = row > col
    eye = (row == col).astype(F32)
    nw = nw_ref[...]
    mm = lambda a, b: jnp.dot(a.astype(BF16), b.astype(BF16), preferred_element_type=F32)

    def local_stages(chunks, items):
        for ci in chunks:
            r0 = ci * c
            for hd in range(nh):
                lanes = slice(hd * d, (hd + 1) * d)
                q = qn_ref[r0:r0 + c, lanes]
                k = kn_ref[r0:r0 + c, lanes]
                beta = beta_ref[r0:r0 + c, hd:hd + 1]
                gcol = gc_ref[r0:r0 + c, nh + hd:nh + hd + 1]
                grow = gct_ref[nh + hd:nh + hd + 1, r0:r0 + c]
                glast = gcol[c - 1:c, :]
                decay = jnp.exp(jnp.where(causal, gcol - grow, -jnp.inf))
                eg = jnp.exp(gcol)
                kb = k * beta
                items.append(dict(ci=ci, hd=hd, r0=r0, lanes=lanes, decay=decay, kbf=k.astype(BF16),
                                  qbf=q.astype(BF16), kb_bf=kb.astype(BF16),
                                  rhs=jnp.concatenate([vn_ref[r0:r0 + c, lanes] * beta, kb * eg], axis=1).astype(BF16),
                                  qd=(q * eg).astype(BF16), k_dec=(k * jnp.exp(glast - gcol)).astype(BF16),
                                  egl=jnp.exp(glast)))
        for it in items:
            it["kk"] = _dot_nt(it["kb_bf"], it["kbf"])
            it["qk"] = _dot_nt(it["qbf"], it["kbf"])
        yield
        for it in items:
            nmat = jnp.where(strict, it["kk"] * it["decay"], 0.0)
            it["inv"] = eye - nmat
            it["pw"] = nmat.astype(BF16)
            it["attn"] = (it["qk"] * it["decay"]).astype(BF16)
        for it in items:
            it["pw"] = jnp.dot(it["pw"], it["pw"], preferred_element_type=F32).astype(BF16)
        yield
        for _ in range(int(math.log2(c)) - 2):
            for it in items:
                it["upd"] = jnp.dot(it["inv"].astype(BF16), it["pw"], preferred_element_type=F32)
                it["pw"] = jnp.dot(it["pw"], it["pw"], preferred_element_type=F32).astype(BF16)
            yield
            for it in items:
                it["inv"] = it["inv"] + it["upd"]
        for it in items:
            it["upd"] = jnp.dot(it["inv"].astype(BF16), it["pw"], preferred_element_type=F32)
        yield
        for it in items:
            it["inv"] = (it["inv"] + it["upd"]).astype(BF16)
        for it in items:
            uw = jnp.dot(it["inv"], it["rhs"], preferred_element_type=F32)
            it["u"] = uw[:, :d]
            it["wq"] = jnp.concatenate([uw[:, d:].astype(BF16), it["qd"]], axis=0)
        yield

    def sweep_stages(items, states):
        for ci in sorted({it["ci"] for it in items}):
            group = [it for it in items if it["ci"] == ci]
            for it in group:
                it["ws_qs"] = jnp.dot(it["wq"], states[it["hd"]].astype(BF16), preferred_element_type=F32)
            yield
            for it in group:
                it["v_new"] = (it["u"] - it["ws_qs"][:c]).astype(BF16)
            for it in group:
                it["av"] = jnp.dot(it["attn"], it["v_new"], preferred_element_type=F32)
                it["kv"] = _dot_tn(it["k_dec"], it["v_new"])
            yield
            for it in group:
                states[it["hd"]] = states[it["hd"]] * it["egl"] + it["kv"]
                o = it["ws_qs"][c:] + it["av"]
                o = o * lax.rsqrt(jnp.mean(o * o, axis=-1, keepdims=True) + NORM_EPS) * nw
                o = o * _silu(zg_ref[it["r0"]:it["r0"] + c, it["lanes"]])
                o_ref[it["r0"]:it["r0"] + c, it["lanes"]] = o.astype(o_ref.dtype)

    def interleave(*gens):
        live = list(gens)
        while live:
            for gen in list(live):
                if next(gen, "end") == "end":
                    live.remove(gen)

    states = [s_ref[hd] for hd in range(nh)]
    group_chunks = 2
    groups = [list(range(g0, g0 + group_chunks)) for g0 in range(0, tile // c, group_chunks)]
    ready = []
    interleave(local_stages(groups[0], ready))
    for nxt in groups[1:]:
        upcoming = []
        interleave(local_stages(nxt, upcoming), sweep_stages(ready, states))
        ready = upcoming
    interleave(sweep_stages(ready, states))
    for hd in range(nh):
        s_ref[hd] = states[hd]


def _deltanet(z3, conv_w, alog_pad, dtb_pad, norm_w, *, tile):
    b, s, _ = z3.shape
    wdt = DN_WIDTH
    qblk = (ATTN_WIDTH + 2 * ATTN_KV_WIDTH) // wdt
    seq = lambda off: pl.BlockSpec((None, tile, wdt), lambda bi, t: (bi, t, qblk + off))
    const = lambda shape: pl.BlockSpec(shape, lambda bi, t: (0, 0))
    return pl.pallas_call(
        _dn_kernel,
        grid=(b, s // tile),
        in_specs=[
            seq(0), seq(1), seq(2), seq(3),
            pl.BlockSpec((None, tile, LANE), lambda bi, t: (bi, t, Z_BA_BLOCK)),
            const(conv_w.shape), const((1, LANE)), const((1, LANE)), const((1, HEAD_DIM)),
        ],
        out_specs=pl.BlockSpec((None, tile, wdt), lambda bi, t: (bi, t, 0)),
        out_shape=jax.ShapeDtypeStruct((b, s, wdt), BF16),
        scratch_shapes=[
            pltpu.VMEM((tile + 8, 3 * wdt), F32),
            pltpu.VMEM((tile, wdt), F32),
            pltpu.VMEM((tile, wdt), F32),
            pltpu.VMEM((tile, wdt), F32),
            pltpu.VMEM((tile, LANE), F32),
            pltpu.VMEM((8, tile), F32),
            pltpu.VMEM((tile, LANE), F32),
            pltpu.VMEM((DN_HEADS, HEAD_DIM, HEAD_DIM), F32),
        ],
        compiler_params=_params("parallel", "arbitrary"),
        name="deltanet",
    )(z3, z3, z3, z3, z3, conv_w, alog_pad, dtb_pad, norm_w)


def _s5_prep_kernel(are_ref, aim_ref, ldt_ref, bre_ref, bim_ref, apr_ref, api_ref, bbr_ref, bbi_ref):
    lr = are_ref[...]
    li = aim_ref[...]
    dt = jnp.exp(ldt_ref[...])
    step = (lax.broadcasted_iota(jnp.int32, (S5_SEG, 1), 0) + 1).astype(F32)
    mag = jnp.exp(step * (lr * dt))
    ang = step * (li * dt)
    apr_ref[...] = mag * jnp.cos(ang)
    api_ref[...] = mag * jnp.sin(ang)
    m1 = jnp.exp(lr * dt)
    nr = m1 * jnp.cos(li * dt) - 1.0
    ni = m1 * jnp.sin(li * dt)
    den = lr * lr + li * li
    cr = (nr * lr + ni * li) / den
    ci = (ni * lr - nr * li) / den
    br = bre_ref[...]
    bi = bim_ref[...]
    bbr_ref[...] = cr * br - ci * bi
    bbi_ref[...] = cr * bi + ci * br


def _s5_prep(a_re, a_im, log_dt, b_re, b_im):
    row = lambda a: a.reshape(1, S5_LANES)
    ldt = jnp.repeat(log_dt, S5_STATE).reshape(1, S5_LANES)
    tr = lambda a: a.reshape(S5_LANES, S5_GROUP_CH).T
    shp = jax.ShapeDtypeStruct
    return pl.pallas_call(
        _s5_prep_kernel,
        out_shape=(shp((S5_SEG, S5_LANES), F32), shp((S5_SEG, S5_LANES), F32),
                   shp((S5_GROUP_CH, S5_LANES), F32), shp((S5_GROUP_CH, S5_LANES), F32)),
        name="s5_prep",
    )(row(a_re), row(a_im), ldt, tr(b_re), tr(b_im))


def _s5_kernel(u0_ref, u1_ref, u2_ref, u3_ref, bre_ref, bim_ref, cre_ref, cim_ref, apr_ref, api_ref, d_ref,
               gw_ref, gb_ref, o_ref, up_ref, xr_ref, xi_ref, cr_ref, ci_ref, st_ref, y_ref):
    t = pl.program_id(1)
    tile = u0_ref.shape[0]
    seg = tile // 8
    strip = 512

    @pl.when(t == 0)
    def _():
        st_ref[...] = jnp.zeros_like(st_ref)

    for j, u_ref in enumerate((u0_ref, u1_ref, u2_ref, u3_ref)):
        for k in range(seg):
            up_ref[8 * k:8 * k + 8, j * LANE:(j + 1) * LANE] = u_ref[pl.ds(k, 8, stride=seg), :]
    ub = up_ref[...].astype(BF16)
    xr_ref[...] = jnp.dot(ub, bre_ref[...], preferred_element_type=F32)
    xi_ref[...] = jnp.dot(ub, bim_ref[...], preferred_element_type=F32)

    for s0 in range(0, S5_LANES, strip):
        lanes = slice(s0, s0 + strip)
        ar = jnp.broadcast_to(apr_ref[0:1, lanes], (8, strip))
        ai = jnp.broadcast_to(api_ref[0:1, lanes], (8, strip))

        def scan(k, carry):
            pr, pi = carry
            r0 = pl.multiple_of(k * 8, 8)
            nr = ar * pr - ai * pi + xr_ref[pl.ds(r0, 8), lanes]
            ni = ar * pi + ai * pr + xi_ref[pl.ds(r0, 8), lanes]
            xr_ref[pl.ds(r0, 8), lanes] = nr
            xi_ref[pl.ds(r0, 8), lanes] = ni
            return nr, ni

        zero = jnp.zeros((8, strip), F32)
        fr, fi = lax.fori_loop(0, seg, scan, (zero, zero))

        a64r = apr_ref[seg - 1:seg, lanes]
        a64i = api_ref[seg - 1:seg, lanes]
        c_r = st_ref[0:1, lanes]
        c_i = st_ref[1:2, lanes]
        for r in range(8):
            cr_ref[r:r + 1, lanes] = c_r
            ci_ref[r:r + 1, lanes] = c_i
            n_r = a64r * c_r - a64i * c_i + fr[r:r + 1]
            n_i = a64r * c_i + a64i * c_r + fi[r:r + 1]
            c_r, c_i = n_r, n_i
        st_ref[0:1, lanes] = c_r
        st_ref[1:2, lanes] = c_i

        cin_r = cr_ref[:, lanes]
        cin_i = ci_ref[:, lanes]

        def fix(k, carry):
            r0 = pl.multiple_of(k * 8, 8)
            pr = apr_ref[pl.ds(k, 1), lanes]
            pi = api_ref[pl.ds(k, 1), lanes]
            xr_ref[pl.ds(r0, 8), lanes] += pr * cin_r - pi * cin_i
            xi_ref[pl.ds(r0, 8), lanes] += pr * cin_i + pi * cin_r
            return carry

        lax.fori_loop(0, seg, fix, 0)

    y = (jnp.dot(xr_ref[...].astype(BF16), cre_ref[...], preferred_element_type=F32)
         - jnp.dot(xi_ref[...].astype(BF16), cim_ref[...], preferred_element_type=F32))
    y = y + d_ref[...] * up_ref[...]
    y = 0.5 * y * (1.0 + jnp.tanh(math.sqrt(2.0 / math.pi) * (y + 0.044715 * (y * y * y))))
    gate = jnp.dot(y.astype(BF16), gw_ref[...], preferred_element_type=F32) + gb_ref[...]
    y_ref[...] = y * _sigmoid(gate)
    for j in range(S5_WIDTH // LANE):
        for k in range(seg):
            o_ref[j, pl.ds(k, 8, stride=seg), :] = y_ref[8 * k:8 * k + 8, j * LANE:(j + 1) * LANE]


def _s5(z3, b_re, b_im, c_re, c_im, ap_re, ap_im, d_skip, glu_w, glu_b, *, tile):
    b, s, _ = z3.shape
    nblk = S5_WIDTH // LANE
    ublk = (ATTN_WIDTH + 2 * ATTN_KV_WIDTH + 4 * DN_WIDTH) // LANE
    const = lambda a: pl.BlockSpec(a.shape, lambda bi, t: (0, 0))
    ucol = lambda j: pl.BlockSpec((None, tile, LANE), lambda bi, t: (bi, t, ublk + j))
    return pl.pallas_call(
        _s5_kernel,
        grid=(b, s // tile),
        in_specs=[
            ucol(0), ucol(1), ucol(2), ucol(3),
            const(b_re), const(b_im), const(c_re), const(c_im), const(ap_re), const(ap_im),
            const(d_skip), const(glu_w), const(glu_b),
        ],
        out_specs=pl.BlockSpec((nblk, None, tile, LANE), lambda bi, t: (0, bi, t, 0)),
        out_shape=jax.ShapeDtypeStruct((nblk, b, s, LANE), F32),
        scratch_shapes=[
            pltpu.VMEM((tile, S5_WIDTH), F32),
            pltpu.VMEM((tile, S5_LANES), F32),
            pltpu.VMEM((tile, S5_LANES), F32),
            pltpu.VMEM((8, S5_LANES), F32),
            pltpu.VMEM((8, S5_LANES), F32),
            pltpu.VMEM((8, S5_LANES), F32),
            pltpu.VMEM((tile, S5_WIDTH), F32),
        ],
        compiler_params=_params("parallel", "arbitrary"),
        name="s5",
    )(z3, z3, z3, z3, b_re, b_im, c_re, c_im, ap_re, ap_im, d_skip, glu_w, glu_b)


def _block_diag_in(bb):
    eye = jnp.eye(S5_GROUPS, dtype=bb.dtype)
    t = bb.reshape(S5_GROUP_CH, S5_GROUPS, S5_STATE)
    return jnp.einsum('hgp,fg->fhgp', t, eye).reshape(S5_WIDTH, S5_LANES)


def _block_diag_out(cc):
    eye = jnp.eye(S5_GROUPS, dtype=cc.dtype)
    return jnp.einsum('ghp,gf->gpfh', cc, eye).reshape(S5_LANES, S5_WIDTH)


def _rope_tables(seq):
    half = HEAD_DIM // 2
    inv_freq = ROPE_THETA ** (-jnp.arange(half, dtype=F32) / half)
    ang = jnp.arange(seq, dtype=F32)[:, None] * inv_freq[None, :]
    cos, sin = jnp.cos(ang), jnp.sin(ang)
    return jnp.concatenate([cos, cos], axis=-1), jnp.concatenate([-sin, sin], axis=-1)


def _reorder_w_in(w_in):
    s5_src = SRC_BA + 2 * DN_HEADS
    pad = jnp.zeros(w_in.shape[:-1] + (LANE - 2 * DN_HEADS,), w_in.dtype)
    return jnp.concatenate([w_in[..., :SRC_BA], w_in[..., s5_src:], w_in[..., SRC_BA:s5_src], pad], axis=-1).astype(BF16)


def _lane_pad(v, offset):
    return jnp.zeros((v.shape[0], 1, LANE), F32).at[:, 0, offset:offset + v.shape[1]].set(v)


def kernel(x, ff1_norm_pre, ff1_w_gate, ff1_w_up, ff1_w_down, ff1_norm_post, mix_norm_pre, w_in,
           attn_sinks, dn_conv_w, dn_a_log, dn_dt_bias, dn_norm_w, s5_a_re, s5_a_im, s5_log_dt,
           s5_b_re, s5_b_im, s5_c_re, s5_c_im, s5_d, s5_glu_w, s5_glu_b, w_out, mix_norm_post,
           ff2_norm_pre, ff2_w_gate, ff2_w_up, ff2_w_down, ff2_norm_post):
    b, s, d = x.shape
    depth = w_in.shape[0]
    n = b * s
    tm = min(512, n)
    tm_big = min(1024, n)
    tf = 512
    tile = min(SEQ_TILE, s)

    bf = lambda a: a.astype(BF16)
    ff1 = (bf(ff1_w_gate), bf(ff1_w_up), bf(ff1_w_down))
    ff2 = (bf(ff2_w_gate), bf(ff2_w_up), bf(ff2_w_down))
    w_in_r = _reorder_w_in(w_in)
    w_out_b = bf(w_out)
    glu_w_b = bf(s5_glu_w)
    cc, ss = _rope_tables(s)
    alog_pad = _lane_pad(dn_a_log, DN_HEADS)
    dtb_pad = _lane_pad(dn_dt_bias, DN_HEADS)
    row = lambda a, l: a[l].reshape(1, -1)

    xf = x.reshape(n, d)
    for l in range(depth):
        xf = _ffn(xf, row(ff1_norm_pre, l), *ff1, row(ff1_norm_post, l), l, tm=tm, tf=tf)

        z3 = _inproj(xf, row(mix_norm_pre, l), w_in_r, l, tm=tm_big, tn=Z_WIDTH // 3).reshape(b, s, Z_WIDTH)
        y_attn = _attention(z3, attn_sinks[l], cc, ss)
        y_dn = _deltanet(z3, dn_conv_w[l], alog_pad[l], dtb_pad[l], row(dn_norm_w, l), tile=tile)
        ap_re, ap_im, bb_re, bb_im = _s5_prep(s5_a_re[l], s5_a_im[l], s5_log_dt[l], s5_b_re[l], s5_b_im[l])
        y_s5 = _s5(z3, bf(_block_diag_in(bb_re)), bf(_block_diag_in(bb_im)),
                   bf(_block_diag_out(s5_c_re[l])), bf(_block_diag_out(s5_c_im[l])),
                   ap_re, ap_im, row(s5_d, l), glu_w_b[l], row(s5_glu_b, l), tile=tile)
        xf = _outproj(xf, y_attn.reshape(n, -1), y_dn.reshape(n, -1), y_s5.reshape(-1, n, LANE),
                      w_out_b, row(mix_norm_post, l), l, tm=tm)

        xf = _ffn(xf, row(ff2_norm_pre, l), *ff2, row(ff2_norm_post, l), l, tm=tm, tf=tf)
    return xf.reshape(b, s, d)
```

```python
import functools
import math

import jax
import jax.numpy as jnp
from jax import lax
from jax.experimental import pallas as pl
from jax.experimental.pallas import tpu as pltpu

F32 = jnp.float32
BF16 = jnp.bfloat16
HIGHEST = lax.Precision.HIGHEST

NORM_EPS = 1e-6
FFN_RES_WEIGHT = 0.5
ROPE_THETA = 10000.0

HEAD_DIM = 128
WINDOW = 128
ATTN_HEADS = 8
ATTN_KV_HEADS = 2
ATTN_GROUP = ATTN_HEADS // ATTN_KV_HEADS
DN_HEADS = 4
DN_CONV = 4
DN_CHUNK = 64
S5_GROUPS = 32
S5_GROUP_CH = 16
S5_STATE = 64
ATTN_WIDTH = ATTN_HEADS * HEAD_DIM
ATTN_KV_WIDTH = ATTN_KV_HEADS * HEAD_DIM
DN_WIDTH = DN_HEADS * HEAD_DIM
S5_WIDTH = S5_GROUPS * S5_GROUP_CH
S5_LANES = S5_GROUPS * S5_STATE
MIX_WIDTH = ATTN_WIDTH + DN_WIDTH + S5_WIDTH

LANE = 128
Z_WIDTH = ATTN_WIDTH + 2 * ATTN_KV_WIDTH + 4 * DN_WIDTH + S5_WIDTH + LANE
Z_BA_BLOCK = (Z_WIDTH - LANE) // LANE
SRC_BA = ATTN_WIDTH + 2 * ATTN_KV_WIDTH + 4 * DN_WIDTH

SEQ_TILE = 512
S5_SEG = SEQ_TILE // 8
VMEM_LIMIT = 56 * 1024 * 1024


def _params(*sem):
    return pltpu.CompilerParams(dimension_semantics=sem, vmem_limit_bytes=VMEM_LIMIT)


def _rms(x, gain):
    return x * lax.rsqrt(jnp.mean(x * x, axis=-1, keepdims=True) + NORM_EPS) * gain


def _sigmoid(x):
    return 1.0 / (1.0 + jnp.exp(-x))


def _ffn_kernel(x_ref, gpre_ref, wg_ref, wu_ref, wd_ref, gpost_ref, o_ref, h_ref):
    j = pl.program_id(1)

    @pl.when(j == 0)
    def _():
        h_ref[...] = _rms(x_ref[...], gpre_ref[...]).astype(BF16)
        o_ref[...] = jnp.zeros_like(o_ref)

    h = h_ref[...]
    g = jnp.dot(h, wg_ref[...], preferred_element_type=F32)
    u = jnp.dot(h, wu_ref[...], preferred_element_type=F32)
    a = (g * _sigmoid(g) * u).astype(BF16)
    o_ref[...] += jnp.dot(a, wd_ref[...], preferred_element_type=F32)

    @pl.when(j == pl.num_programs(1) - 1)
    def _():
        o_ref[...] = x_ref[...] + FFN_RES_WEIGHT * _rms(o_ref[...], gpost_ref[...])


def _ffn(x, gpre, wg, wu, wd, gpost, layer, *, tm, tf):
    n, d = x.shape
    f = wg.shape[2]
    return pl.pallas_call(
        _ffn_kernel,
        grid=(n // tm, f // tf),
        in_specs=[
            pl.BlockSpec((tm, d), lambda i, j: (i, 0)),
            pl.BlockSpec((1, d), lambda i, j: (0, 0)),
            pl.BlockSpec((None, d, tf), lambda i, j: (layer, 0, j)),
            pl.BlockSpec((None, d, tf), lambda i, j: (layer, 0, j)),
            pl.BlockSpec((None, tf, d), lambda i, j: (layer, j, 0)),
            pl.BlockSpec((1, d), lambda i, j: (0, 0)),
        ],
        out_specs=pl.BlockSpec((tm, d), lambda i, j: (i, 0)),
        out_shape=jax.ShapeDtypeStruct((n, d), F32),
        scratch_shapes=[pltpu.VMEM((tm, d), BF16)],
        compiler_params=_params("parallel", "arbitrary"),
        name="ffn",
    )(x, gpre, wg, wu, wd, gpost)


def _inproj_kernel(x_ref, g_ref, w_ref, o_ref, h_ref):
    @pl.when(pl.program_id(1) == 0)
    def _():
        h_ref[...] = _rms(x_ref[...], g_ref[...]).astype(BF16)

    o_ref[...] = jnp.dot(h_ref[...], w_ref[...], preferred_element_type=F32)


def _inproj(x, gain, w, layer, *, tm, tn):
    n, d = x.shape
    zw = w.shape[2]
    return pl.pallas_call(
        _inproj_kernel,
        grid=(n // tm, zw // tn),
        in_specs=[
            pl.BlockSpec((tm, d), lambda i, j: (i, 0)),
            pl.BlockSpec((1, d), lambda i, j: (0, 0)),
            pl.BlockSpec((None, d, tn), lambda i, j: (layer, 0, j)),
        ],
        out_specs=pl.BlockSpec((tm, tn), lambda i, j: (i, j)),
        out_shape=jax.ShapeDtypeStruct((n, zw), F32),
        scratch_shapes=[pltpu.VMEM((tm, d), BF16)],
        compiler_params=_params("parallel", "arbitrary"),
        name="inproj",
    )(x, gain, w)


def _outproj_kernel(x_ref, ya_ref, yd_ref, ys_ref, w_ref, g_ref, o_ref):
    ys = [ys_ref[j].astype(BF16) for j in range(ys_ref.shape[0])]
    y = jnp.concatenate([ya_ref[...], yd_ref[...]] + ys, axis=-1)
    mixed = jnp.dot(y, w_ref[...], preferred_element_type=F32)
    o_ref[...] = x_ref[...] + _rms(mixed, g_ref[...])


def _outproj(x, ya, yd, ys, w, gain, layer, *, tm):
    n, d = x.shape
    return pl.pallas_call(
        _outproj_kernel,
        grid=(n // tm,),
        in_specs=[
            pl.BlockSpec((tm, d), lambda i: (i, 0)),
            pl.BlockSpec((tm, ya.shape[1]), lambda i: (i, 0)),
            pl.BlockSpec((tm, yd.shape[1]), lambda i: (i, 0)),
            pl.BlockSpec((ys.shape[0], tm, ys.shape[2]), lambda i: (0, i, 0)),
            pl.BlockSpec((None,) + w.shape[1:], lambda i: (layer, 0, 0)),
            pl.BlockSpec((1, d), lambda i: (0, 0)),
        ],
        out_specs=pl.BlockSpec((tm, d), lambda i: (i, 0)),
        out_shape=jax.ShapeDtypeStruct((n, d), F32),
        compiler_params=_params("parallel"),
        name="outproj",
    )(x, ya, yd, ys, w, gain)


def _rope(x, cc, ss):
    return x * cc + pltpu.roll(x, HEAD_DIM // 2, axis=1) * ss


def _attn_kernel(sink_ref, q_ref, kp_ref, kc_ref, vp_ref, vc_ref, ccp_ref, ccc_ref, ssp_ref, ssc_ref, o_ref):
    h = pl.program_id(1)
    t = pl.program_id(2)
    w = WINDOW
    grp = ATTN_GROUP
    nblk = q_ref.shape[0] // w
    ccc = ccc_ref[...]
    ssc = ssc_ref[...]
    q = q_ref[...]
    scale = HEAD_DIM ** -0.5
    qr = [(_rope(q[:, g * HEAD_DIM:(g + 1) * HEAD_DIM], ccc, ssc) * scale).astype(BF16) for g in range(grp)]
    kk = jnp.concatenate([_rope(kp_ref[...], ccp_ref[...], ssp_ref[...]), _rope(kc_ref[...], ccc, ssc)],
                         axis=0).astype(BF16)
    vv = jnp.concatenate([vp_ref[...], vc_ref[...]], axis=0).astype(BF16)
    qi = lax.broadcasted_iota(jnp.int32, (w, 2 * w), 0) + w
    kj = lax.broadcasted_iota(jnp.int32, (w, 2 * w), 1)
    rel = qi - kj
    band = (rel >= 0) & (rel < w)
    first = band & ((kj >= w) | (t > 0))
    row_head = lax.broadcasted_iota(jnp.int32, (grp * w, 1), 0) // w
    sink = jnp.zeros((grp * w, 1), F32)
    for g in range(grp):
        sink = jnp.where(row_head == g, sink_ref[h * grp + g], sink)
    scores = []
    for blk in range(nblk):
        qs = jnp.concatenate([qr[g][blk * w:(blk + 1) * w] for g in range(grp)], axis=0)
        scores.append(_dot_nt(qs, kk[blk * w:(blk + 2) * w]))
    probs = []
    for blk in range(nblk):
        mask = jnp.concatenate([first if blk == 0 else band] * grp, axis=0)
        sc = jnp.where(mask, scores[blk], -jnp.inf)
        m = jnp.maximum(jnp.max(sc, axis=-1, keepdims=True), sink)
        p = jnp.exp(sc - m)
        denom = jnp.sum(p, axis=-1, keepdims=True) + jnp.exp(sink - m)
        probs.append((p / denom).astype(BF16))
    outs = [jnp.dot(probs[blk], vv[blk * w:(blk + 2) * w], preferred_element_type=F32) for blk in range(nblk)]
    for blk in range(nblk):
        o_ref[blk * w:(blk + 1) * w, :] = jnp.concatenate(
            [outs[blk][g * w:(g + 1) * w] for g in range(grp)], axis=1).astype(o_ref.dtype)


def _attention(z3, sinks, cc, ss, *, tile):
    b, s, _ = z3.shape
    w = WINDOW
    nblk = tile // w
    qw = ATTN_GROUP * HEAD_DIM
    kblk = ATTN_WIDTH // HEAD_DIM
    vblk = (ATTN_WIDTH + ATTN_KV_WIDTH) // HEAD_DIM
    prev = lambda t: jnp.maximum(t * nblk - 1, 0)
    return pl.pallas_call(
        _attn_kernel,
        grid=(b, ATTN_KV_HEADS, s // tile),
        in_specs=[
            pl.BlockSpec(memory_space=pltpu.SMEM),
            pl.BlockSpec((None, tile, qw), lambda bi, h, t: (bi, t, h)),
            pl.BlockSpec((None, w, HEAD_DIM), lambda bi, h, t: (bi, prev(t), kblk + h)),
            pl.BlockSpec((None, tile, HEAD_DIM), lambda bi, h, t: (bi, t, kblk + h)),
            pl.BlockSpec((None, w, HEAD_DIM), lambda bi, h, t: (bi, prev(t), vblk + h)),
            pl.BlockSpec((None, tile, HEAD_DIM), lambda bi, h, t: (bi, t, vblk + h)),
            pl.BlockSpec((w, HEAD_DIM), lambda bi, h, t: (prev(t), 0)),
            pl.BlockSpec((tile, HEAD_DIM), lambda bi, h, t: (t, 0)),
            pl.BlockSpec((w, HEAD_DIM), lambda bi, h, t: (prev(t), 0)),
            pl.BlockSpec((tile, HEAD_DIM), lambda bi, h, t: (t, 0)),
        ],
        out_specs=pl.BlockSpec((None, tile, qw), lambda bi, h, t: (bi, t, h)),
        out_shape=jax.ShapeDtypeStruct((b, s, ATTN_WIDTH), BF16),
        compiler_params=_params("parallel", "parallel", "arbitrary"),
        name="swa",
    )(sinks, z3, z3, z3, z3, z3, cc, cc, ss, ss)


def _silu(x):
    return x * _sigmoid(x)


def _softplus(x):
    return jnp.maximum(x, 0.0) + jnp.log(1.0 + jnp.exp(-jnp.abs(x)))


def _dot_nt(a, b, precision=None):
    return lax.dot_general(a, b, (((1,), (1,)), ((), ())), preferred_element_type=F32, precision=precision)


def _dot_tn(a, b, precision=None):
    return lax.dot_general(a, b, (((0,), (0,)), ((), ())), preferred_element_type=F32, precision=precision)


def _dn_kernel(q_ref, k_ref, v_ref, zg_ref, ba_ref, cw_ref, alog_ref, dtb_ref, nw_ref, o_ref,
               ext_ref, qn_ref, kn_ref, vn_ref, gc_ref, gct_ref, beta_ref, s_ref):
    t = pl.program_id(1)
    tile = q_ref.shape[0]
    c = DN_CHUNK
    d = HEAD_DIM
    nh = DN_HEADS
    wdt = nh * d

    @pl.when(t == 0)
    def _():
        ext_ref[0:8, :] = jnp.zeros((8, 3 * wdt), F32)
        s_ref[...] = jnp.zeros_like(s_ref)

    ext_ref[8:8 + tile, 0:wdt] = q_ref[...]
    ext_ref[8:8 + tile, wdt:2 * wdt] = k_ref[...]
    ext_ref[8:8 + tile, 2 * wdt:3 * wdt] = v_ref[...]
    conv = jnp.zeros((tile, 3 * wdt), F32)
    for tap in range(DN_CONV):
        off = 8 - (DN_CONV - 1) + tap
        conv = conv + ext_ref[off:off + tile, :] * cw_ref[tap:tap + 1, :]
    ext_ref[0:8, :] = ext_ref[tile:tile + 8, :]
    qkv = _silu(conv)
    for hd in range(nh):
        qh = qkv[:, hd * d:(hd + 1) * d]
        kh = qkv[:, wdt + hd * d:wdt + (hd + 1) * d]
        qn_ref[:, hd * d:(hd + 1) * d] = qh * lax.rsqrt(jnp.sum(qh * qh, axis=-1, keepdims=True) + NORM_EPS) * (d ** -0.5)
        kn_ref[:, hd * d:(hd + 1) * d] = kh * lax.rsqrt(jnp.sum(kh * kh, axis=-1, keepdims=True) + NORM_EPS)
    vn_ref[...] = qkv[:, 2 * wdt:3 * wdt]
    ba = ba_ref[...]
    beta_ref[...] = _sigmoid(ba)
    g = -jnp.exp(alog_ref[...]) * _softplus(ba + dtb_ref[...])
    ti = lax.broadcasted_iota(jnp.int32, (c, c), 0)
    tj = lax.broadcasted_iota(jnp.int32, (c, c), 1)
    csum = (ti >= tj).astype(F32)
    gc_all = jnp.concatenate([jnp.dot(csum, g[ci * c:(ci + 1) * c], preferred_element_type=F32, precision=HIGHEST)
                              for ci in range(tile // c)], axis=0)
    gc_ref[...] = gc_all
    gct_ref[...] = gc_all.T[0:8, :]

    row = lax.broadcasted_iota(jnp.int32, (c, c), 0)
    col = lax.broadcasted_iota(jnp.int32, (c, c), 1)
    causal = row >= col
    strict = row > col
    eye = (row == col).astype(F32)
    nw = nw_ref[...]

    def join_mask(half):
        return (row // (2 * half) == col // (2 * half)) & (row % (2 * half) >= half) & (col % (2 * half) < half)

    def local_stages(chunks, items):
        for ci in chunks:
            r0 = ci * c
            for hd in range(nh):
                lanes = slice(hd * d, (hd + 1) * d)
                q = qn_ref[r0:r0 + c, lanes]
                k = kn_ref[r0:r0 + c, lanes]
                beta = beta_ref[r0:r0 + c, hd:hd + 1]
                gcol = gc_ref[r0:r0 + c, nh + hd:nh + hd + 1]
                grow = gct_ref[nh + hd:nh + hd + 1, r0:r0 + c]
                glast = gcol[c - 1:c, :]
                decay = jnp.exp(jnp.where(causal, gcol - grow, -jnp.inf))
                eg = jnp.exp(gcol)
                kb = k * beta
                items.append(dict(ci=ci, hd=hd, r0=r0, lanes=lanes, decay=decay, kbf=k.astype(BF16),
                                  qbf=q.astype(BF16), kb_bf=kb.astype(BF16),
                                  rhs=jnp.concatenate([vn_ref[r0:r0 + c, lanes] * beta, kb * eg], axis=1).astype(BF16),
                                  qd=(q * eg).astype(BF16), k_dec=(k * jnp.exp(glast - gcol)).astype(BF16),
                                  egl=jnp.exp(glast)))
        for it in items:
            it["kk"] = _dot_nt(it["kb_bf"], it["kbf"])
            it["qk"] = _dot_nt(it["qbf"], it["kbf"])
        yield
        for it in items:
            it["nmat"] = jnp.where(strict, it["kk"] * it["decay"], 0.0)
            it["inv"] = eye - jnp.where(join_mask(1), it["nmat"], 0.0)
            it["attn"] = (it["qk"] * it["decay"]).astype(BF16)
        half = 2
        while half < c:
            for it in items:
                it["inv_bf"] = it["inv"].astype(BF16)
                join = jnp.where(join_mask(half), it["nmat"], 0.0).astype(BF16)
                it["bt"] = jnp.dot(join, it["inv_bf"], preferred_element_type=F32).astype(BF16)
            yield
            for it in items:
                it["upd"] = jnp.dot(it["inv_bf"], it["bt"], preferred_element_type=F32)
            yield
            for it in items:
                it["inv"] = it["inv"] - it["upd"]
            half *= 2
        for it in items:
            uw = jnp.dot(it["inv"].astype(BF16), it["rhs"], preferred_element_type=F32)
            it["u"] = uw[:, :d]
            it["wq"] = jnp.concatenate([uw[:, d:].astype(BF16), it["qd"]], axis=0)
        yield

    def sweep_stages(items, states):
        for ci in sorted({it["ci"] for it in items}):
            group = [it for it in items if it["ci"] == ci]
            for it in group:
                it["ws_qs"] = jnp.dot(it["wq"], states[it["hd"]].astype(BF16), preferred_element_type=F32)
            yield
            for it in group:
                it["v_new"] = (it["u"] - it["ws_qs"][:c]).astype(BF16)
            for it in group:
                it["av"] = jnp.dot(it["attn"], it["v_new"], preferred_element_type=F32)
                it["kv"] = _dot_tn(it["k_dec"], it["v_new"])
            yield
            for it in group:
                states[it["hd"]] = states[it["hd"]] * it["egl"] + it["kv"]
                o = it["ws_qs"][c:] + it["av"]
                o = o * lax.rsqrt(jnp.mean(o * o, axis=-1, keepdims=True) + NORM_EPS) * nw
                o = o * _silu(zg_ref[it["r0"]:it["r0"] + c, it["lanes"]])
                o_ref[it["r0"]:it["r0"] + c, it["lanes"]] = o.astype(o_ref.dtype)

    def interleave(*gens):
        live = list(gens)
        while live:
            for gen in list(live):
                if next(gen, "end") == "end":
                    live.remove(gen)

    states = [s_ref[hd] for hd in range(nh)]
    group_chunks = 2
    groups = [list(range(g0, g0 + group_chunks)) for g0 in range(0, tile // c, group_chunks)]
    ready = []
    interleave(local_stages(groups[0], ready))
    for nxt in groups[1:]:
        upcoming = []
        interleave(local_stages(nxt, upcoming), sweep_stages(ready, states))
        ready = upcoming
    interleave(sweep_stages(ready, states))
    for hd in range(nh):
        s_ref[hd] = states[hd]


def _deltanet(z3, conv_w, alog_pad, dtb_pad, norm_w, *, tile):
    b, s, _ = z3.shape
    wdt = DN_WIDTH
    qblk = (ATTN_WIDTH + 2 * ATTN_KV_WIDTH) // wdt
    seq = lambda off: pl.BlockSpec((None, tile, wdt), lambda bi, t: (bi, t, qblk + off))
    const = lambda shape: pl.BlockSpec(shape, lambda bi, t: (0, 0))
    return pl.pallas_call(
        _dn_kernel,
        grid=(b, s // tile),
        in_specs=[
            seq(0), seq(1), seq(2), seq(3),
            pl.BlockSpec((None, tile, LANE), lambda bi, t: (bi, t, Z_BA_BLOCK)),
            const(conv_w.shape), const((1, LANE)), const((1, LANE)), const((1, HEAD_DIM)),
        ],
        out_specs=pl.BlockSpec((None, tile, wdt), lambda bi, t: (bi, t, 0)),
        out_shape=jax.ShapeDtypeStruct((b, s, wdt), BF16),
        scratch_shapes=[
            pltpu.VMEM((tile + 8, 3 * wdt), F32),
            pltpu.VMEM((tile, wdt), F32),
            pltpu.VMEM((tile, wdt), F32),
            pltpu.VMEM((tile, wdt), F32),
            pltpu.VMEM((tile, LANE), F32),
            pltpu.VMEM((8, tile), F32),
            pltpu.VMEM((tile, LANE), F32),
            pltpu.VMEM((DN_HEADS, HEAD_DIM, HEAD_DIM), F32),
        ],
        compiler_params=_params("parallel", "arbitrary"),
        name="deltanet",
    )(z3, z3, z3, z3, z3, conv_w, alog_pad, dtb_pad, norm_w)


def _s5_prep_kernel(are_ref, aim_ref, ldt_ref, bre_ref, bim_ref, apr_ref, api_ref, bbr_ref, bbi_ref):
    lr = are_ref[...]
    li = aim_ref[...]
    dt = jnp.exp(ldt_ref[...])
    step = (lax.broadcasted_iota(jnp.int32, (S5_SEG, 1), 0) + 1).astype(F32)
    mag = jnp.exp(step * (lr * dt))
    ang = step * (li * dt)
    apr_ref[...] = mag * jnp.cos(ang)
    api_ref[...] = mag * jnp.sin(ang)
    m1 = jnp.exp(lr * dt)
    nr = m1 * jnp.cos(li * dt) - 1.0
    ni = m1 * jnp.sin(li * dt)
    den = lr * lr + li * li
    cr = (nr * lr + ni * li) / den
    ci = (ni * lr - nr * li) / den
    br = bre_ref[...]
    bi = bim_ref[...]
    bbr_ref[...] = cr * br - ci * bi
    bbi_ref[...] = cr * bi + ci * br


def _s5_prep(a_re, a_im, log_dt, b_re, b_im):
    row = lambda a: a.reshape(1, S5_LANES)
    ldt = jnp.repeat(log_dt, S5_STATE).reshape(1, S5_LANES)
    tr = lambda a: a.reshape(S5_LANES, S5_GROUP_CH).T
    shp = jax.ShapeDtypeStruct
    return pl.pallas_call(
        _s5_prep_kernel,
        out_shape=(shp((S5_SEG, S5_LANES), F32), shp((S5_SEG, S5_LANES), F32),
                   shp((S5_GROUP_CH, S5_LANES), F32), shp((S5_GROUP_CH, S5_LANES), F32)),
        name="s5_prep",
    )(row(a_re), row(a_im), ldt, tr(b_re), tr(b_im))


def _s5_kernel(u0_ref, u1_ref, u2_ref, u3_ref, bre_ref, bim_ref, cre_ref, cim_ref, apr_ref, api_ref, d_ref,
               gw_ref, gb_ref, o_ref, up_ref, xr_ref, xi_ref, cr_ref, ci_ref, st_ref, y_ref):
    t = pl.program_id(1)
    tile = u0_ref.shape[0]
    seg = tile // 8
    strip = 512

    @pl.when(t == 0)
    def _():
        st_ref[...] = jnp.zeros_like(st_ref)

    for j, u_ref in enumerate((u0_ref, u1_ref, u2_ref, u3_ref)):
        for k in range(seg):
            up_ref[8 * k:8 * k + 8, j * LANE:(j + 1) * LANE] = u_ref[pl.ds(k, 8, stride=seg), :]
    hw, hl = S5_WIDTH // 2, S5_LANES // 2
    for half in range(2):
        ub = up_ref[:, half * hw:(half + 1) * hw].astype(BF16)
        rows, cols = slice(half * hw, (half + 1) * hw), slice(half * hl, (half + 1) * hl)
        xr_ref[:, cols] = jnp.dot(ub, bre_ref[rows, cols], preferred_element_type=F32)
        xi_ref[:, cols] = jnp.dot(ub, bim_ref[rows, cols], preferred_element_type=F32)

    for s0 in range(0, S5_LANES, strip):
        lanes = slice(s0, s0 + strip)
        ar = jnp.broadcast_to(apr_ref[0:1, lanes], (8, strip))
        ai = jnp.broadcast_to(api_ref[0:1, lanes], (8, strip))

        def scan(k, carry):
            pr, pi = carry
            r0 = pl.multiple_of(k * 8, 8)
            nr = ar * pr - ai * pi + xr_ref[pl.ds(r0, 8), lanes]
            ni = ar * pi + ai * pr + xi_ref[pl.ds(r0, 8), lanes]
            xr_ref[pl.ds(r0, 8), lanes] = nr
            xi_ref[pl.ds(r0, 8), lanes] = ni
            return nr, ni

        zero = jnp.zeros((8, strip), F32)
        fr, fi = lax.fori_loop(0, seg, scan, (zero, zero), unroll=4)

        a64r = apr_ref[seg - 1:seg, lanes]
        a64i = api_ref[seg - 1:seg, lanes]
        c_r = st_ref[0:1, lanes]
        c_i = st_ref[1:2, lanes]
        for r in range(8):
            cr_ref[r:r + 1, lanes] = c_r
            ci_ref[r:r + 1, lanes] = c_i
            n_r = a64r * c_r - a64i * c_i + fr[r:r + 1]
            n_i = a64r * c_i + a64i * c_r + fi[r:r + 1]
            c_r, c_i = n_r, n_i
        st_ref[0:1, lanes] = c_r
        st_ref[1:2, lanes] = c_i

        cin_r = cr_ref[:, lanes]
        cin_i = ci_ref[:, lanes]

        def fix(k, carry):
            r0 = pl.multiple_of(k * 8, 8)
            pr = apr_ref[pl.ds(k, 1), lanes]
            pi = api_ref[pl.ds(k, 1), lanes]
            xr_ref[pl.ds(r0, 8), lanes] += pr * cin_r - pi * cin_i
            xi_ref[pl.ds(r0, 8), lanes] += pr * cin_i + pi * cin_r
            return carry

        lax.fori_loop(0, seg, fix, 0, unroll=4)

    ys = []
    for half in range(2):
        rows, cols = slice(half * hl, (half + 1) * hl), slice(half * hw, (half + 1) * hw)
        ys.append(jnp.dot(xr_ref[:, rows].astype(BF16), cre_ref[rows, cols], preferred_element_type=F32)
                  - jnp.dot(xi_ref[:, rows].astype(BF16), cim_ref[rows, cols], preferred_element_type=F32))
    y = jnp.concatenate(ys, axis=1)
    y = y + d_ref[...] * up_ref[...]
    y = 0.5 * y * (1.0 + jnp.tanh(math.sqrt(2.0 / math.pi) * (y + 0.044715 * (y * y * y))))
    gate = jnp.dot(y.astype(BF16), gw_ref[...], preferred_element_type=F32) + gb_ref[...]
    y_ref[...] = y * _sigmoid(gate)
    for j in range(S5_WIDTH // LANE):
        for k in range(seg):
            o_ref[j, pl.ds(k, 8, stride=seg), :] = y_ref[8 * k:8 * k + 8, j * LANE:(j + 1) * LANE]


def _s5(z3, b_re, b_im, c_re, c_im, ap_re, ap_im, d_skip, glu_w, glu_b, *, tile):
    b, s, _ = z3.shape
    nblk = S5_WIDTH // LANE
    ublk = (ATTN_WIDTH + 2 * ATTN_KV_WIDTH + 4 * DN_WIDTH) // LANE
    const = lambda a: pl.BlockSpec(a.shape, lambda bi, t: (0, 0))
    ucol = lambda j: pl.BlockSpec((None, tile, LANE), lambda bi, t: (bi, t, ublk + j))
    return pl.pallas_call(
        _s5_kernel,
        grid=(b, s // tile),
        in_specs=[
            ucol(0), ucol(1), ucol(2), ucol(3),
            const(b_re), const(b_im), const(c_re), const(c_im), const(ap_re), const(ap_im),
            const(d_skip), const(glu_w), const(glu_b),
        ],
        out_specs=pl.BlockSpec((nblk, None, tile, LANE), lambda bi, t: (0, bi, t, 0)),
        out_shape=jax.ShapeDtypeStruct((nblk, b, s, LANE), F32),
        scratch_shapes=[
            pltpu.VMEM((tile, S5_WIDTH), F32),
            pltpu.VMEM((tile, S5_LANES), F32),
            pltpu.VMEM((tile, S5_LANES), F32),
            pltpu.VMEM((8, S5_LANES), F32),
            pltpu.VMEM((8, S5_LANES), F32),
            pltpu.VMEM((8, S5_LANES), F32),
            pltpu.VMEM((tile, S5_WIDTH), F32),
        ],
        compiler_params=_params("parallel", "arbitrary"),
        name="s5",
    )(z3, z3, z3, z3, b_re, b_im, c_re, c_im, ap_re, ap_im, d_skip, glu_w, glu_b)


def _block_diag_in(bb):
    eye = jnp.eye(S5_GROUPS, dtype=bb.dtype)
    t = bb.reshape(S5_GROUP_CH, S5_GROUPS, S5_STATE)
    return jnp.einsum('hgp,fg->fhgp', t, eye).reshape(S5_WIDTH, S5_LANES)


def _block_diag_out(cc):
    eye = jnp.eye(S5_GROUPS, dtype=cc.dtype)
    return jnp.einsum('ghp,gf->gpfh', cc, eye).reshape(S5_LANES, S5_WIDTH)


def _rope_tables(seq):
    half = HEAD_DIM // 2
    inv_freq = ROPE_THETA ** (-jnp.arange(half, dtype=F32) / half)
    ang = jnp.arange(seq, dtype=F32)[:, None] * inv_freq[None, :]
    cos, sin = jnp.cos(ang), jnp.sin(ang)
    return jnp.concatenate([cos, cos], axis=-1), jnp.concatenate([-sin, sin], axis=-1)


def _reorder_w_in(w_in):
    s5_src = SRC_BA + 2 * DN_HEADS
    pad = jnp.zeros(w_in.shape[:-1] + (LANE - 2 * DN_HEADS,), w_in.dtype)
    return jnp.concatenate([w_in[..., :SRC_BA], w_in[..., s5_src:], w_in[..., SRC_BA:s5_src], pad], axis=-1).astype(BF16)


def _lane_pad(v, offset):
    return jnp.zeros((v.shape[0], 1, LANE), F32).at[:, 0, offset:offset + v.shape[1]].set(v)


def kernel(x, ff1_norm_pre, ff1_w_gate, ff1_w_up, ff1_w_down, ff1_norm_post, mix_norm_pre, w_in,
           attn_sinks, dn_conv_w, dn_a_log, dn_dt_bias, dn_norm_w, s5_a_re, s5_a_im, s5_log_dt,
           s5_b_re, s5_b_im, s5_c_re, s5_c_im, s5_d, s5_glu_w, s5_glu_b, w_out, mix_norm_post,
           ff2_norm_pre, ff2_w_gate, ff2_w_up, ff2_w_down, ff2_norm_post):
    b, s, d = x.shape
    depth = w_in.shape[0]
    n = b * s
    tm = min(512, n)
    tm_big = min(1024, n)
    tf = 512
    tile = min(SEQ_TILE, s)

    bf = lambda a: a.astype(BF16)
    ff1 = (bf(ff1_w_gate), bf(ff1_w_up), bf(ff1_w_down))
    ff2 = (bf(ff2_w_gate), bf(ff2_w_up), bf(ff2_w_down))
    w_in_r = _reorder_w_in(w_in)
    w_out_b = bf(w_out)
    glu_w_b = bf(s5_glu_w)
    cc, ss = _rope_tables(s)
    alog_pad = _lane_pad(dn_a_log, DN_HEADS)
    dtb_pad = _lane_pad(dn_dt_bias, DN_HEADS)
    row = lambda a, l: a[l].reshape(1, -1)

    xf = x.reshape(n, d)
    for l in range(depth):
        xf = _ffn(xf, row(ff1_norm_pre, l), *ff1, row(ff1_norm_post, l), l, tm=tm, tf=tf)

        z3 = _inproj(xf, row(mix_norm_pre, l), w_in_r, l, tm=tm_big, tn=Z_WIDTH // 3).reshape(b, s, Z_WIDTH)
        y_attn = _attention(z3, attn_sinks[l], cc, ss, tile=tile)
        y_dn = _deltanet(z3, dn_conv_w[l], alog_pad[l], dtb_pad[l], row(dn_norm_w, l), tile=tile)
        ap_re, ap_im, bb_re, bb_im = _s5_prep(s5_a_re[l], s5_a_im[l], s5_log_dt[l], s5_b_re[l], s5_b_im[l])
        y_s5 = _s5(z3, bf(_block_diag_in(bb_re)), bf(_block_diag_in(bb_im)),
                   bf(_block_diag_out(s5_c_re[l])), bf(_block_diag_out(s5_c_im[l])),
                   ap_re, ap_im, row(s5_d, l), glu_w_b[l], row(s5_glu_b, l), tile=tile)
        xf = _outproj(xf, y_attn.reshape(n, -1), y_dn.reshape(n, -1), y_s5.reshape(-1, n, LANE),
                      w_out_b, row(mix_norm_post, l), l, tm=tm)

        xf = _ffn(xf, row(ff2_norm_pre, l), *ff2, row(ff2_norm_post, l), l, tm=tm, tf=tf)
    return xf.reshape(b, s, d)
```

```python
import functools
import math

import jax
import jax.numpy as jnp
from jax import lax
from jax.experimental import pallas as pl
from jax.experimental.pallas import tpu as pltpu

F32 = jnp.float32
BF16 = jnp.bfloat16
HIGHEST = lax.Precision.HIGHEST

NORM_EPS = 1e-6
FFN_RES_WEIGHT = 0.5
ROPE_THETA = 10000.0

HEAD_DIM = 128
WINDOW = 128
ATTN_HEADS = 8
ATTN_KV_HEADS = 2
ATTN_GROUP = ATTN_HEADS // ATTN_KV_HEADS
DN_HEADS = 4
DN_CONV = 4
DN_CHUNK = 64
S5_GROUPS = 32
S5_GROUP_CH = 16
S5_STATE = 64
ATTN_WIDTH = ATTN_HEADS * HEAD_DIM
ATTN_KV_WIDTH = ATTN_KV_HEADS * HEAD_DIM
DN_WIDTH = DN_HEADS * HEAD_DIM
S5_WIDTH = S5_GROUPS * S5_GROUP_CH
S5_LANES = S5_GROUPS * S5_STATE
MIX_WIDTH = ATTN_WIDTH + DN_WIDTH + S5_WIDTH

LANE = 128
Z_WIDTH = ATTN_WIDTH + 2 * ATTN_KV_WIDTH + 4 * DN_WIDTH + S5_WIDTH + LANE
Z_BA_BLOCK = (Z_WIDTH - LANE) // LANE
SRC_BA = ATTN_WIDTH + 2 * ATTN_KV_WIDTH + 4 * DN_WIDTH

SEQ_TILE = 512
S5_SEG = SEQ_TILE // 8
VMEM_LIMIT = 56 * 1024 * 1024


def _params(*sem):
    return pltpu.CompilerParams(dimension_semantics=sem, vmem_limit_bytes=VMEM_LIMIT)


def _rms(x, gain):
    return x * lax.rsqrt(jnp.mean(x * x, axis=-1, keepdims=True) + NORM_EPS) * gain


def _sigmoid(x):
    return 1.0 / (1.0 + jnp.exp(-x))


def _ffn_kernel(*refs, cast_weights):
    if cast_weights:
        x_ref, gpre_ref, wg_ref, wu_ref, wd_ref, gpost_ref, o_ref, wgb_ref, wub_ref, wdb_ref, h_ref = refs
        wgb_ref[...] = wg_ref[...].astype(BF16)
        wub_ref[...] = wu_ref[...].astype(BF16)
        wdb_ref[...] = wd_ref[...].astype(BF16)
        wg_ref, wu_ref, wd_ref = wgb_ref, wub_ref, wdb_ref
    else:
        _, x_ref, gpre_ref, wg_ref, wu_ref, wd_ref, gpost_ref, o_ref, h_ref = refs
    j = pl.program_id(1)

    @pl.when(j == 0)
    def _():
        h_ref[...] = _rms(x_ref[...], gpre_ref[...]).astype(BF16)
        o_ref[...] = jnp.zeros_like(o_ref)

    h = h_ref[...]
    g = jnp.dot(h, wg_ref[...], preferred_element_type=F32)
    u = jnp.dot(h, wu_ref[...], preferred_element_type=F32)
    a = (g * _sigmoid(g) * u).astype(BF16)
    o_ref[...] += jnp.dot(a, wd_ref[...], preferred_element_type=F32)

    @pl.when(j == pl.num_programs(1) - 1)
    def _():
        o_ref[...] = x_ref[...] + FFN_RES_WEIGHT * _rms(o_ref[...], gpost_ref[...])


def _ffn(x, gpre, wg, wu, wd, gpost, layer, *, tm, tf, tf_head):
    n, d = x.shape
    f = wg.shape[2]
    once = pl.Buffered(1)
    vec = pl.BlockSpec((1, d), lambda i, j: (0, 0))
    shp = jax.ShapeDtypeStruct
    y, wgb, wub, wdb = pl.pallas_call(
        functools.partial(_ffn_kernel, cast_weights=True),
        grid=(1, f // tf_head),
        in_specs=[
            pl.BlockSpec((tm, d), lambda i, j: (0, 0), pipeline_mode=once),
            vec,
            pl.BlockSpec((None, d, tf_head), lambda i, j: (layer, 0, j)),
            pl.BlockSpec((None, d, tf_head), lambda i, j: (layer, 0, j)),
            pl.BlockSpec((None, tf_head, d), lambda i, j: (layer, j, 0)),
            vec,
        ],
        out_specs=[
            pl.BlockSpec((tm, d), lambda i, j: (0, 0), pipeline_mode=once),
            pl.BlockSpec((d, tf_head), lambda i, j: (0, j)),
            pl.BlockSpec((d, tf_head), lambda i, j: (0, j)),
            pl.BlockSpec((tf_head, d), lambda i, j: (j, 0)),
        ],
        out_shape=(shp((n, d), F32), shp((d, f), BF16), shp((d, f), BF16), shp((f, d), BF16)),
        scratch_shapes=[pltpu.VMEM((tm, d), BF16)],
        compiler_params=_params("arbitrary", "arbitrary"),
        name="ffn_head",
    )(x, gpre, wg, wu, wd, gpost)
    return pl.pallas_call(
        functools.partial(_ffn_kernel, cast_weights=False),
        grid=(n // tm - 1, f // tf),
        in_specs=[
            pl.BlockSpec(memory_space=pl.ANY),
            pl.BlockSpec((tm, d), lambda i, j: (i + 1, 0), pipeline_mode=once),
            vec,
            pl.BlockSpec((d, tf), lambda i, j: (0, j)),
            pl.BlockSpec((d, tf), lambda i, j: (0, j)),
            pl.BlockSpec((tf, d), lambda i, j: (j, 0)),
            vec,
        ],
        out_specs=pl.BlockSpec((tm, d), lambda i, j: (i + 1, 0)),
        out_shape=shp((n, d), F32),
        input_output_aliases={0: 0},
        scratch_shapes=[pltpu.VMEM((tm, d), BF16)],
        compiler_params=_params("parallel", "arbitrary"),
        name="ffn",
    )(y, x, gpre, wgb, wub, wdb, gpost)


def _inproj_kernel(x_ref, g_ref, wm_ref, ws_ref, wb_ref, o_ref):
    h = _rms(x_ref[...], g_ref[...]).astype(BF16)
    s5_at = wm_ref.shape[1]
    ba_at = s5_at + ws_ref.shape[1]
    o_ref[:, :s5_at] = jnp.dot(h, wm_ref[...], preferred_element_type=F32)
    o_ref[:, s5_at:ba_at] = jnp.dot(h, ws_ref[...], preferred_element_type=F32)
    o_ref[:, ba_at:] = jnp.dot(h, wb_ref[...], preferred_element_type=F32)


def _inproj(x, gain, w_all, w_s5, w_ba, layer, *, tm):
    n, d = x.shape
    resident = lambda shape: pl.BlockSpec((None,) + shape, lambda i: (layer, 0, 0), pipeline_mode=pl.Buffered(1))
    return pl.pallas_call(
        _inproj_kernel,
        grid=(n // tm,),
        in_specs=[
            pl.BlockSpec((tm, d), lambda i: (i, 0)),
            pl.BlockSpec((1, d), lambda i: (0, 0)),
            resident((d, SRC_BA)), resident((d, S5_WIDTH)), resident((d, LANE)),
        ],
        out_specs=pl.BlockSpec((tm, Z_WIDTH), lambda i: (i, 0)),
        out_shape=jax.ShapeDtypeStruct((n, Z_WIDTH), F32),
        compiler_params=_params("parallel"),
        name="inproj",
    )(x, gain, w_all, w_s5, w_ba)


def _outproj_kernel(x_ref, ya_ref, yd_ref, ys_ref, w_ref, g_ref, o_ref):
    ys = [ys_ref[j].astype(BF16) for j in range(ys_ref.shape[0])]
    y = jnp.concatenate([ya_ref[...], yd_ref[...]] + ys, axis=-1)
    mixed = jnp.dot(y, w_ref[...], preferred_element_type=F32)
    o_ref[...] = x_ref[...] + _rms(mixed, g_ref[...])


def _outproj(x, ya, yd, ys, w, gain, layer, *, tm):
    n, d = x.shape
    return pl.pallas_call(
        _outproj_kernel,
        grid=(n // tm,),
        in_specs=[
            pl.BlockSpec((tm, d), lambda i: (i, 0)),
            pl.BlockSpec((tm, ya.shape[1]), lambda i: (i, 0)),
            pl.BlockSpec((tm, yd.shape[1]), lambda i: (i, 0)),
            pl.BlockSpec((ys.shape[0], tm, ys.shape[2]), lambda i: (0, i, 0)),
            pl.BlockSpec((None,) + w.shape[1:], lambda i: (layer, 0, 0)),
            pl.BlockSpec((1, d), lambda i: (0, 0)),
        ],
        out_specs=pl.BlockSpec((tm, d), lambda i: (i, 0)),
        out_shape=jax.ShapeDtypeStruct((n, d), F32),
        compiler_params=_params("parallel"),
        name="outproj",
    )(x, ya, yd, ys, w, gain)


def _rope(x, cc, ss):
    return x * cc + pltpu.roll(x, HEAD_DIM // 2, axis=1) * ss


def _attn_kernel(sink_ref, q_ref, kp_ref, kc_ref, vp_ref, vc_ref, ccp_ref, ccc_ref, ssp_ref, ssc_ref, o_ref):
    h = pl.program_id(1)
    t = pl.program_id(2)
    w = WINDOW
    grp = ATTN_GROUP
    nblk = q_ref.shape[0] // w
    ccc = ccc_ref[...]
    ssc = ssc_ref[...]
    q = q_ref[...]
    scale = HEAD_DIM ** -0.5
    qr = [(_rope(q[:, g * HEAD_DIM:(g + 1) * HEAD_DIM], ccc, ssc) * scale).astype(BF16) for g in range(grp)]
    kk = jnp.concatenate([_rope(kp_ref[...], ccp_ref[...], ssp_ref[...]), _rope(kc_ref[...], ccc, ssc)],
                         axis=0).astype(BF16)
    vv = jnp.concatenate([vp_ref[...], vc_ref[...]], axis=0).astype(BF16)
    qi = lax.broadcasted_iota(jnp.int32, (w, 2 * w), 0) + w
    kj = lax.broadcasted_iota(jnp.int32, (w, 2 * w), 1)
    rel = qi - kj
    band = (rel >= 0) & (rel < w)
    first = band & ((kj >= w) | (t > 0))
    row_head = lax.broadcasted_iota(jnp.int32, (grp * w, 1), 0) // w
    sink = jnp.zeros((grp * w, 1), F32)
    for g in range(grp):
        sink = jnp.where(row_head == g, sink_ref[h * grp + g], sink)
    scores = []
    for blk in range(nblk):
        qs = jnp.concatenate([qr[g][blk * w:(blk + 1) * w] for g in range(grp)], axis=0)
        scores.append(_dot_nt(qs, kk[blk * w:(blk + 2) * w]))
    probs = []
    for blk in range(nblk):
        mask = jnp.concatenate([first if blk == 0 else band] * grp, axis=0)
        sc = jnp.where(mask, scores[blk], -jnp.inf)
        m = jnp.maximum(jnp.max(sc, axis=-1, keepdims=True), sink)
        p = jnp.exp(sc - m)
        denom = jnp.sum(p, axis=-1, keepdims=True) + jnp.exp(sink - m)
        probs.append((p / denom).astype(BF16))
    outs = [jnp.dot(probs[blk], vv[blk * w:(blk + 2) * w], preferred_element_type=F32) for blk in range(nblk)]
    for blk in range(nblk):
        o_ref[blk * w:(blk + 1) * w, :] = jnp.concatenate(
            [outs[blk][g * w:(g + 1) * w] for g in range(grp)], axis=1).astype(o_ref.dtype)


def _attention(z3, sinks, cc, ss, *, tile):
    b, s, _ = z3.shape
    w = WINDOW
    nblk = tile // w
    qw = ATTN_GROUP * HEAD_DIM
    kblk = ATTN_WIDTH // HEAD_DIM
    vblk = (ATTN_WIDTH + ATTN_KV_WIDTH) // HEAD_DIM
    prev = lambda t: jnp.maximum(t * nblk - 1, 0)
    return pl.pallas_call(
        _attn_kernel,
        grid=(b, ATTN_KV_HEADS, s // tile),
        in_specs=[
            pl.BlockSpec(memory_space=pltpu.SMEM),
            pl.BlockSpec((None, tile, qw), lambda bi, h, t: (bi, t, h)),
            pl.BlockSpec((None, w, HEAD_DIM), lambda bi, h, t: (bi, prev(t), kblk + h)),
            pl.BlockSpec((None, tile, HEAD_DIM), lambda bi, h, t: (bi, t, kblk + h)),
            pl.BlockSpec((None, w, HEAD_DIM), lambda bi, h, t: (bi, prev(t), vblk + h)),
            pl.BlockSpec((None, tile, HEAD_DIM), lambda bi, h, t: (bi, t, vblk + h)),
            pl.BlockSpec((w, HEAD_DIM), lambda bi, h, t: (prev(t), 0)),
            pl.BlockSpec((tile, HEAD_DIM), lambda bi, h, t: (t, 0)),
            pl.BlockSpec((w, HEAD_DIM), lambda bi, h, t: (prev(t), 0)),
            pl.BlockSpec((tile, HEAD_DIM), lambda bi, h, t: (t, 0)),
        ],
        out_specs=pl.BlockSpec((None, tile, qw), lambda bi, h, t: (bi, t, h)),
        out_shape=jax.ShapeDtypeStruct((b, s, ATTN_WIDTH), BF16),
        compiler_params=_params("parallel", "parallel", "arbitrary"),
        name="swa",
    )(sinks, z3, z3, z3, z3, z3, cc, cc, ss, ss)


def _silu(x):
    return x * _sigmoid(x)


def _softplus(x):
    return jnp.maximum(x, 0.0) + jnp.log(1.0 + jnp.exp(-jnp.abs(x)))


def _dot_nt(a, b, precision=None):
    return lax.dot_general(a, b, (((1,), (1,)), ((), ())), preferred_element_type=F32, precision=precision)


def _dot_tn(a, b, precision=None):
    return lax.dot_general(a, b, (((0,), (0,)), ((), ())), preferred_element_type=F32, precision=precision)


def _dn_kernel(q_ref, k_ref, v_ref, zg_ref, ba_ref, cw_ref, alog_ref, dtb_ref, nw_ref, o_ref,
               ext_ref, qn_ref, kn_ref, vn_ref, gc_ref, gct_ref, beta_ref, s_ref):
    t = pl.program_id(1)
    tile = q_ref.shape[0]
    c = DN_CHUNK
    d = HEAD_DIM
    nh = DN_HEADS
    wdt = nh * d

    @pl.when(t == 0)
    def _():
        ext_ref[0:8, :] = jnp.zeros((8, 3 * wdt), F32)
        s_ref[...] = jnp.zeros_like(s_ref)

    ext_ref[8:8 + tile, 0:wdt] = q_ref[...]
    ext_ref[8:8 + tile, wdt:2 * wdt] = k_ref[...]
    ext_ref[8:8 + tile, 2 * wdt:3 * wdt] = v_ref[...]
    conv = jnp.zeros((tile, 3 * wdt), F32)
    for tap in range(DN_CONV):
        off = 8 - (DN_CONV - 1) + tap
        conv = conv + ext_ref[off:off + tile, :] * cw_ref[tap:tap + 1, :]
    ext_ref[0:8, :] = ext_ref[tile:tile + 8, :]
    qkv = _silu(conv)
    for hd in range(nh):
        qh = qkv[:, hd * d:(hd + 1) * d]
        kh = qkv[:, wdt + hd * d:wdt + (hd + 1) * d]
        qn_ref[:, hd * d:(hd + 1) * d] = qh * lax.rsqrt(jnp.sum(qh * qh, axis=-1, keepdims=True) + NORM_EPS) * (d ** -0.5)
        kn_ref[:, hd * d:(hd + 1) * d] = kh * lax.rsqrt(jnp.sum(kh * kh, axis=-1, keepdims=True) + NORM_EPS)
    vn_ref[...] = qkv[:, 2 * wdt:3 * wdt]
    ba = ba_ref[...]
    beta_ref[...] = _sigmoid(ba)
    g = -jnp.exp(alog_ref[...]) * _softplus(ba + dtb_ref[...])
    ti = lax.broadcasted_iota(jnp.int32, (c, c), 0)
    tj = lax.broadcasted_iota(jnp.int32, (c, c), 1)
    csum = (ti >= tj).astype(F32)
    gc_all = jnp.concatenate([jnp.dot(csum, g[ci * c:(ci + 1) * c], preferred_element_type=F32, precision=HIGHEST)
                              for ci in range(tile // c)], axis=0)
    gc_ref[...] = gc_all
    gct_ref[...] = gc_all.T[0:8, :]

    row = lax.broadcasted_iota(jnp.int32, (c, c), 0)
    col = lax.broadcasted_iota(jnp.int32, (c, c), 1)
    causal = row >= col
    strict = row > col
    eye = (row == col).astype(F32)
    nw = nw_ref[...]

    def join_mask(half):
        return (row // (2 * half) == col // (2 * half)) & (row % (2 * half) >= half) & (col % (2 * half) < half)

    def local_stages(chunks, items):
        for ci in chunks:
            r0 = ci * c
            for hd in range(nh):
                lanes = slice(hd * d, (hd + 1) * d)
                q = qn_ref[r0:r0 + c, lanes]
                k = kn_ref[r0:r0 + c, lanes]
                beta = beta_ref[r0:r0 + c, hd:hd + 1]
                gcol = gc_ref[r0:r0 + c, nh + hd:nh + hd + 1]
                grow = gct_ref[nh + hd:nh + hd + 1, r0:r0 + c]
                glast = gcol[c - 1:c, :]
                decay = jnp.exp(jnp.where(causal, gcol - grow, -jnp.inf))
                eg = jnp.exp(gcol)
                kb = k * beta
                items.append(dict(ci=ci, hd=hd, r0=r0, lanes=lanes, decay=decay, kbf=k.astype(BF16),
                                  qbf=q.astype(BF16), kb_bf=kb.astype(BF16),
                                  rhs=jnp.concatenate([vn_ref[r0:r0 + c, lanes] * beta, kb * eg], axis=1).astype(BF16),
                                  qd=(q * eg).astype(BF16), k_dec=(k * jnp.exp(glast - gcol)).astype(BF16),
                                  egl=jnp.exp(glast)))
        for it in items:
            it["kk"] = _dot_nt(it["kb_bf"], it["kbf"])
            it["qk"] = _dot_nt(it["qbf"], it["kbf"])
        yield
        for it in items:
            it["nmat"] = jnp.where(strict, it["kk"] * it["decay"], 0.0)
            it["inv"] = eye - jnp.where(join_mask(1), it["nmat"], 0.0)
            it["attn"] = (it["qk"] * it["decay"]).astype(BF16)
        half = 2
        while half < c:
            for it in items:
                it["inv_bf"] = it["inv"].astype(BF16)
                join = jnp.where(join_mask(half), it["nmat"], 0.0).astype(BF16)
                it["bt"] = jnp.dot(join, it["inv_bf"], preferred_element_type=F32).astype(BF16)
            yield
            for it in items:
                it["upd"] = jnp.dot(it["inv_bf"], it["bt"], preferred_element_type=F32)
            yield
            for it in items:
                it["inv"] = it["inv"] - it["upd"]
            half *= 2
        for it in items:
            uw = jnp.dot(it["inv"].astype(BF16), it["rhs"], preferred_element_type=F32)
            it["u"] = uw[:, :d]
            it["wq"] = jnp.concatenate([uw[:, d:].astype(BF16), it["qd"]], axis=0)
        yield

    def sweep_stages(items, states):
        for ci in sorted({it["ci"] for it in items}):
            group = [it for it in items if it["ci"] == ci]
            for it in group:
                it["ws_qs"] = jnp.dot(it["wq"], states[it["hd"]].astype(BF16), preferred_element_type=F32)
            yield
            for it in group:
                it["v_new"] = (it["u"] - it["ws_qs"][:c]).astype(BF16)
            for it in group:
                it["av"] = jnp.dot(it["attn"], it["v_new"], preferred_element_type=F32)
                it["kv"] = _dot_tn(it["k_dec"], it["v_new"])
            yield
            for it in group:
                states[it["hd"]] = states[it["hd"]] * it["egl"] + it["kv"]
                o = it["ws_qs"][c:] + it["av"]
                o = o * lax.rsqrt(jnp.mean(o * o, axis=-1, keepdims=True) + NORM_EPS) * nw
                o = o * _silu(zg_ref[it["r0"]:it["r0"] + c, it["lanes"]])
                o_ref[it["r0"]:it["r0"] + c, it["lanes"]] = o.astype(o_ref.dtype)

    def interleave(*gens):
        live = list(gens)
        while live:
            for gen in list(live):
                if next(gen, "end") == "end":
                    live.remove(gen)

    states = [s_ref[hd] for hd in range(nh)]
    group_chunks = 2
    groups = [list(range(g0, g0 + group_chunks)) for g0 in range(0, tile // c, group_chunks)]
    ready = []
    interleave(local_stages(groups[0], ready))
    for nxt in groups[1:]:
        upcoming = []
        interleave(local_stages(nxt, upcoming), sweep_stages(ready, states))
        ready = upcoming
    interleave(sweep_stages(ready, states))
    for hd in range(nh):
        s_ref[hd] = states[hd]


def _deltanet(z3, conv_w, alog_pad, dtb_pad, norm_w, *, tile):
    b, s, _ = z3.shape
    wdt = DN_WIDTH
    qblk = (ATTN_WIDTH + 2 * ATTN_KV_WIDTH) // wdt
    seq = lambda off: pl.BlockSpec((None, tile, wdt), lambda bi, t: (bi, t, qblk + off))
    const = lambda shape: pl.BlockSpec(shape, lambda bi, t: (0, 0))
    return pl.pallas_call(
        _dn_kernel,
        grid=(b, s // tile),
        in_specs=[
            seq(0), seq(1), seq(2), seq(3),
            pl.BlockSpec((None, tile, LANE), lambda bi, t: (bi, t, Z_BA_BLOCK)),
            const(conv_w.shape), const((1, LANE)), const((1, LANE)), const((1, HEAD_DIM)),
        ],
        out_specs=pl.BlockSpec((None, tile, wdt), lambda bi, t: (bi, t, 0)),
        out_shape=jax.ShapeDtypeStruct((b, s, wdt), BF16),
        scratch_shapes=[
            pltpu.VMEM((tile + 8, 3 * wdt), F32),
            pltpu.VMEM((tile, wdt), F32),
            pltpu.VMEM((tile, wdt), F32),
            pltpu.VMEM((tile, wdt), F32),
            pltpu.VMEM((tile, LANE), F32),
            pltpu.VMEM((8, tile), F32),
            pltpu.VMEM((tile, LANE), F32),
            pltpu.VMEM((DN_HEADS, HEAD_DIM, HEAD_DIM), F32),
        ],
        compiler_params=_params("parallel", "arbitrary"),
        name="deltanet",
    )(z3, z3, z3, z3, z3, conv_w, alog_pad, dtb_pad, norm_w)


def _s5_prep_kernel(are_ref, aim_ref, ldt_ref, bre_ref, bim_ref, apr_ref, api_ref, bbr_ref, bbi_ref):
    lr = are_ref[...]
    li = aim_ref[...]
    dt = jnp.exp(ldt_ref[...])
    step = (lax.broadcasted_iota(jnp.int32, (S5_SEG, 1), 0) + 1).astype(F32)
    mag = jnp.exp(step * (lr * dt))
    ang = step * (li * dt)
    apr_ref[...] = mag * jnp.cos(ang)
    api_ref[...] = mag * jnp.sin(ang)
    m1 = jnp.exp(lr * dt)
    nr = m1 * jnp.cos(li * dt) - 1.0
    ni = m1 * jnp.sin(li * dt)
    den = lr * lr + li * li
    cr = (nr * lr + ni * li) / den
    ci = (ni * lr - nr * li) / den
    br = bre_ref[...]
    bi = bim_ref[...]
    bbr_ref[...] = cr * br - ci * bi
    bbi_ref[...] = cr * bi + ci * br


def _s5_prep(a_re, a_im, log_dt, b_re, b_im):
    row = lambda a: a.reshape(1, S5_LANES)
    ldt = jnp.repeat(log_dt, S5_STATE).reshape(1, S5_LANES)
    tr = lambda a: a.reshape(S5_LANES, S5_GROUP_CH).T
    shp = jax.ShapeDtypeStruct
    return pl.pallas_call(
        _s5_prep_kernel,
        out_shape=(shp((S5_SEG, S5_LANES), F32), shp((S5_SEG, S5_LANES), F32),
                   shp((S5_GROUP_CH, S5_LANES), F32), shp((S5_GROUP_CH, S5_LANES), F32)),
        name="s5_prep",
    )(row(a_re), row(a_im), ldt, tr(b_re), tr(b_im))


def _s5_kernel(u0_ref, u1_ref, u2_ref, u3_ref, bre_ref, bim_ref, cre_ref, cim_ref, apr_ref, api_ref, d_ref,
               gw_ref, gb_ref, o_ref, up_ref, xr_ref, xi_ref, cr_ref, ci_ref, st_ref, y_ref):
    t = pl.program_id(1)
    tile = u0_ref.shape[0]
    seg = tile // 8
    strip = 512

    @pl.when(t == 0)
    def _():
        st_ref[...] = jnp.zeros_like(st_ref)

    for j, u_ref in enumerate((u0_ref, u1_ref, u2_ref, u3_ref)):
        for k in range(seg):
            up_ref[8 * k:8 * k + 8, j * LANE:(j + 1) * LANE] = u_ref[pl.ds(k, 8, stride=seg), :]
    hw, hl = S5_WIDTH // 2, S5_LANES // 2
    for half in range(2):
        ub = up_ref[:, half * hw:(half + 1) * hw].astype(BF16)
        rows, cols = slice(half * hw, (half + 1) * hw), slice(half * hl, (half + 1) * hl)
        xr_ref[:, cols] = jnp.dot(ub, bre_ref[rows, cols], preferred_element_type=F32)
        xi_ref[:, cols] = jnp.dot(ub, bim_ref[rows, cols], preferred_element_type=F32)

    for s0 in range(0, S5_LANES, strip):
        lanes = slice(s0, s0 + strip)
        ar = jnp.broadcast_to(apr_ref[0:1, lanes], (8, strip))
        ai = jnp.broadcast_to(api_ref[0:1, lanes], (8, strip))

        def scan(k, carry):
            pr, pi = carry
            r0 = pl.multiple_of(k * 8, 8)
            nr = ar * pr - ai * pi + xr_ref[pl.ds(r0, 8), lanes]
            ni = ar * pi + ai * pr + xi_ref[pl.ds(r0, 8), lanes]
            xr_ref[pl.ds(r0, 8), lanes] = nr
            xi_ref[pl.ds(r0, 8), lanes] = ni
            return nr, ni

        zero = jnp.zeros((8, strip), F32)
        fr, fi = lax.fori_loop(0, seg, scan, (zero, zero), unroll=4)

        a64r = apr_ref[seg - 1:seg, lanes]
        a64i = api_ref[seg - 1:seg, lanes]
        c_r = st_ref[0:1, lanes]
        c_i = st_ref[1:2, lanes]
        for r in range(8):
            cr_ref[r:r + 1, lanes] = c_r
            ci_ref[r:r + 1, lanes] = c_i
            n_r = a64r * c_r - a64i * c_i + fr[r:r + 1]
            n_i = a64r * c_i + a64i * c_r + fi[r:r + 1]
            c_r, c_i = n_r, n_i
        st_ref[0:1, lanes] = c_r
        st_ref[1:2, lanes] = c_i

        cin_r = cr_ref[:, lanes]
        cin_i = ci_ref[:, lanes]

        def fix(k, carry):
            r0 = pl.multiple_of(k * 8, 8)
            pr = apr_ref[pl.ds(k, 1), lanes]
            pi = api_ref[pl.ds(k, 1), lanes]
            xr_ref[pl.ds(r0, 8), lanes] += pr * cin_r - pi * cin_i
            xi_ref[pl.ds(r0, 8), lanes] += pr * cin_i + pi * cin_r
            return carry

        lax.fori_loop(0, seg, fix, 0, unroll=4)

    ys = []
    for half in range(2):
        rows, cols = slice(half * hl, (half + 1) * hl), slice(half * hw, (half + 1) * hw)
        ys.append(jnp.dot(xr_ref[:, rows].astype(BF16), cre_ref[rows, cols], preferred_element_type=F32)
                  - jnp.dot(xi_ref[:, rows].astype(BF16), cim_ref[rows, cols], preferred_element_type=F32))
    y = jnp.concatenate(ys, axis=1)
    y = y + d_ref[...] * up_ref[...]
    y = 0.5 * y * (1.0 + jnp.tanh(math.sqrt(2.0 / math.pi) * (y + 0.044715 * (y * y * y))))
    gate = jnp.dot(y.astype(BF16), gw_ref[...], preferred_element_type=F32) + gb_ref[...]
    y_ref[...] = y * _sigmoid(gate)
    for j in range(S5_WIDTH // LANE):
        for k in range(seg):
            o_ref[j, pl.ds(k, 8, stride=seg), :] = y_ref[8 * k:8 * k + 8, j * LANE:(j + 1) * LANE]


def _s5(z3, b_re, b_im, c_re, c_im, ap_re, ap_im, d_skip, glu_w, glu_b, *, tile):
    b, s, _ = z3.shape
    nblk = S5_WIDTH // LANE
    ublk = (ATTN_WIDTH + 2 * ATTN_KV_WIDTH + 4 * DN_WIDTH) // LANE
    const = lambda a: pl.BlockSpec(a.shape, lambda bi, t: (0, 0))
    ucol = lambda j: pl.BlockSpec((None, tile, LANE), lambda bi, t: (bi, t, ublk + j))
    return pl.pallas_call(
        _s5_kernel,
        grid=(b, s // tile),
        in_specs=[
            ucol(0), ucol(1), ucol(2), ucol(3),
            const(b_re), const(b_im), const(c_re), const(c_im), const(ap_re), const(ap_im),
            const(d_skip), const(glu_w), const(glu_b),
        ],
        out_specs=pl.BlockSpec((nblk, None, tile, LANE), lambda bi, t: (0, bi, t, 0)),
        out_shape=jax.ShapeDtypeStruct((nblk, b, s, LANE), F32),
        scratch_shapes=[
            pltpu.VMEM((tile, S5_WIDTH), F32),
            pltpu.VMEM((tile, S5_LANES), F32),
            pltpu.VMEM((tile, S5_LANES), F32),
            pltpu.VMEM((8, S5_LANES), F32),
            pltpu.VMEM((8, S5_LANES), F32),
            pltpu.VMEM((8, S5_LANES), F32),
            pltpu.VMEM((tile, S5_WIDTH), F32),
        ],
        compiler_params=_params("parallel", "arbitrary"),
        name="s5",
    )(z3, z3, z3, z3, b_re, b_im, c_re, c_im, ap_re, ap_im, d_skip, glu_w, glu_b)


def _block_diag_in(bb):
    eye = jnp.eye(S5_GROUPS, dtype=bb.dtype)
    t = bb.reshape(S5_GROUP_CH, S5_GROUPS, S5_STATE)
    return jnp.einsum('hgp,fg->fhgp', t, eye).reshape(S5_WIDTH, S5_LANES)


def _block_diag_out(cc):
    eye = jnp.eye(S5_GROUPS, dtype=cc.dtype)
    return jnp.einsum('ghp,gf->gpfh', cc, eye).reshape(S5_LANES, S5_WIDTH)


def _rope_tables(seq):
    half = HEAD_DIM // 2
    inv_freq = ROPE_THETA ** (-jnp.arange(half, dtype=F32) / half)
    ang = jnp.arange(seq, dtype=F32)[:, None] * inv_freq[None, :]
    cos, sin = jnp.cos(ang), jnp.sin(ang)
    return jnp.concatenate([cos, cos], axis=-1), jnp.concatenate([-sin, sin], axis=-1)


def _split_w_in(w_in):
    s5_src = SRC_BA + 2 * DN_HEADS
    pad = jnp.zeros(w_in.shape[:-1] + (LANE - 2 * DN_HEADS,), BF16)
    w_ba = jnp.concatenate([w_in[..., SRC_BA:s5_src].astype(BF16), pad], axis=-1)
    return w_in.astype(BF16), w_in[..., s5_src:].astype(BF16), w_ba


def _lane_pad(v, offset):
    return jnp.zeros((v.shape[0], 1, LANE), F32).at[:, 0, offset:offset + v.shape[1]].set(v)


def kernel(x, ff1_norm_pre, ff1_w_gate, ff1_w_up, ff1_w_down, ff1_norm_post, mix_norm_pre, w_in,
           attn_sinks, dn_conv_w, dn_a_log, dn_dt_bias, dn_norm_w, s5_a_re, s5_a_im, s5_log_dt,
           s5_b_re, s5_b_im, s5_c_re, s5_c_im, s5_d, s5_glu_w, s5_glu_b, w_out, mix_norm_post,
           ff2_norm_pre, ff2_w_gate, ff2_w_up, ff2_w_down, ff2_norm_post):
    b, s, d = x.shape
    depth = w_in.shape[0]
    n = b * s
    tm = min(512, n)
    tm_big = min(1024, n)
    tf = 512
    tile = min(SEQ_TILE, s)

    bf = lambda a: a.astype(BF16)
    ff1 = (ff1_w_gate, ff1_w_up, ff1_w_down)
    ff2 = (ff2_w_gate, ff2_w_up, ff2_w_down)
    w_in_parts = _split_w_in(w_in)
    w_out_b = bf(w_out)
    glu_w_b = bf(s5_glu_w)
    cc, ss = _rope_tables(s)
    alog_pad = _lane_pad(dn_a_log, DN_HEADS)
    dtb_pad = _lane_pad(dn_dt_bias, DN_HEADS)
    row = lambda a, l: a[l].reshape(1, -1)

    xf = x.reshape(n, d)
    for l in range(depth):
        xf = _ffn(xf, row(ff1_norm_pre, l), *ff1, row(ff1_norm_post, l), l, tm=tm_big, tf=tf, tf_head=tf // 2)

        z3 = _inproj(xf, row(mix_norm_pre, l), *w_in_parts, l, tm=tm).reshape(b, s, Z_WIDTH)
        y_attn = _attention(z3, attn_sinks[l], cc, ss, tile=tile)
        y_dn = _deltanet(z3, dn_conv_w[l], alog_pad[l], dtb_pad[l], row(dn_norm_w, l), tile=tile)
        ap_re, ap_im, bb_re, bb_im = _s5_prep(s5_a_re[l], s5_a_im[l], s5_log_dt[l], s5_b_re[l], s5_b_im[l])
        y_s5 = _s5(z3, bf(_block_diag_in(bb_re)), bf(_block_diag_in(bb_im)),
                   bf(_block_diag_out(s5_c_re[l])), bf(_block_diag_out(s5_c_im[l])),
                   ap_re, ap_im, row(s5_d, l), glu_w_b[l], row(s5_glu_b, l), tile=tile)
        xf = _outproj(xf, y_attn.reshape(n, -1), y_dn.reshape(n, -1), y_s5.reshape(-1, n, LANE),
                      w_out_b, row(mix_norm_post, l), l, tm=tm)

        xf = _ffn(xf, row(ff2_norm_pre, l), *ff2, row(ff2_norm_post, l), l, tm=tm_big, tf=tf, tf_head=tf // 2)
    return xf.reshape(b, s, d)
```

```python
import functools
import math

import jax
import jax.numpy as jnp
from jax import lax
from jax.experimental import pallas as pl
from jax.experimental.pallas import tpu as pltpu

F32 = jnp.float32
BF16 = jnp.bfloat16
HIGHEST = lax.Precision.HIGHEST

NORM_EPS = 1e-6
FFN_RES_WEIGHT = 0.5
ROPE_THETA = 10000.0

HEAD_DIM = 128
WINDOW = 128
ATTN_HEADS = 8
ATTN_KV_HEADS = 2
ATTN_GROUP = ATTN_HEADS // ATTN_KV_HEADS
DN_HEADS = 4
DN_CONV = 4
DN_CHUNK = 64
S5_GROUPS = 32
S5_GROUP_CH = 16
S5_STATE = 64
ATTN_WIDTH = ATTN_HEADS * HEAD_DIM
ATTN_KV_WIDTH = ATTN_KV_HEADS * HEAD_DIM
DN_WIDTH = DN_HEADS * HEAD_DIM
S5_WIDTH = S5_GROUPS * S5_GROUP_CH
S5_LANES = S5_GROUPS * S5_STATE
MIX_WIDTH = ATTN_WIDTH + DN_WIDTH + S5_WIDTH

LANE = 128
Z_WIDTH = ATTN_WIDTH + 2 * ATTN_KV_WIDTH + 4 * DN_WIDTH + S5_WIDTH + LANE
Z_BA_BLOCK = (Z_WIDTH - LANE) // LANE
SRC_BA = ATTN_WIDTH + 2 * ATTN_KV_WIDTH + 4 * DN_WIDTH

SEQ_TILE = 512
S5_SEG = SEQ_TILE // 8
FFN_DOWN_COLS = 512
VMEM_LIMIT = 56 * 1024 * 1024


def _params(*sem):
    return pltpu.CompilerParams(dimension_semantics=sem, vmem_limit_bytes=VMEM_LIMIT)


def _rms(x, gain):
    return x * lax.rsqrt(jnp.mean(x * x, axis=-1, keepdims=True) + NORM_EPS) * gain


def _sigmoid(x):
    return 1.0 / (1.0 + jnp.exp(-x))


def _ffn_kernel(*refs, cast_weights):
    if cast_weights:
        x_ref, gpre_ref, wg_ref, wu_ref, wd_ref, gpost_ref, o_ref, wgb_ref, wub_ref, wdb_ref, h_ref = refs
        wgb_ref[...] = wg_ref[...].astype(BF16)
        wub_ref[...] = wu_ref[...].astype(BF16)
        wdb_ref[...] = wd_ref[...].astype(BF16)
        wg_ref, wu_ref, wd_ref = wgb_ref, wub_ref, wdb_ref
    else:
        _, x_ref, gpre_ref, wg_ref, wu_ref, wd_ref, gpost_ref, o_ref, h_ref = refs
    j = pl.program_id(1)

    @pl.when(j == 0)
    def _():
        h_ref[...] = _rms(x_ref[...], gpre_ref[...]).astype(BF16)
        o_ref[...] = jnp.zeros_like(o_ref)

    h = h_ref[...]
    g = jnp.dot(h, wg_ref[...], preferred_element_type=F32)
    u = jnp.dot(h, wu_ref[...], preferred_element_type=F32)
    a = (g * _sigmoid(g) * u).astype(BF16)
    chunk = min(FFN_DOWN_COLS, o_ref.shape[1])
    for c0 in range(0, o_ref.shape[1], chunk):
        cols = slice(c0, c0 + chunk)
        o_ref[:, cols] += jnp.dot(a, wd_ref[:, cols], preferred_element_type=F32)

    @pl.when(j == pl.num_programs(1) - 1)
    def _():
        o_ref[...] = x_ref[...] + FFN_RES_WEIGHT * _rms(o_ref[...], gpost_ref[...])


def _ffn(x, gpre, wg, wu, wd, gpost, layer, *, tm, tf, tf_head):
    n, d = x.shape
    f = wg.shape[2]
    once = pl.Buffered(1)
    vec = pl.BlockSpec((1, d), lambda i, j: (0, 0))
    shp = jax.ShapeDtypeStruct
    y, wgb, wub, wdb = pl.pallas_call(
        functools.partial(_ffn_kernel, cast_weights=True),
        grid=(1, f // tf_head),
        in_specs=[
            pl.BlockSpec((tm, d), lambda i, j: (0, 0), pipeline_mode=once),
            vec,
            pl.BlockSpec((None, d, tf_head), lambda i, j: (layer, 0, j)),
            pl.BlockSpec((None, d, tf_head), lambda i, j: (layer, 0, j)),
            pl.BlockSpec((None, tf_head, d), lambda i, j: (layer, j, 0)),
            vec,
        ],
        out_specs=[
            pl.BlockSpec((tm, d), lambda i, j: (0, 0), pipeline_mode=once),
            pl.BlockSpec((d, tf_head), lambda i, j: (0, j)),
            pl.BlockSpec((d, tf_head), lambda i, j: (0, j)),
            pl.BlockSpec((tf_head, d), lambda i, j: (j, 0)),
        ],
        out_shape=(shp((n, d), F32), shp((d, f), BF16), shp((d, f), BF16), shp((f, d), BF16)),
        scratch_shapes=[pltpu.VMEM((tm, d), BF16)],
        compiler_params=_params("arbitrary", "arbitrary"),
        name="ffn_head",
    )(x, gpre, wg, wu, wd, gpost)
    return pl.pallas_call(
        functools.partial(_ffn_kernel, cast_weights=False),
        grid=(n // tm - 1, f // tf),
        in_specs=[
            pl.BlockSpec(memory_space=pl.ANY),
            pl.BlockSpec((tm, d), lambda i, j: (i + 1, 0), pipeline_mode=once),
            vec,
            pl.BlockSpec((d, tf), lambda i, j: (0, j)),
            pl.BlockSpec((d, tf), lambda i, j: (0, j)),
            pl.BlockSpec((tf, d), lambda i, j: (j, 0)),
            vec,
        ],
        out_specs=pl.BlockSpec((tm, d), lambda i, j: (i + 1, 0)),
        out_shape=shp((n, d), F32),
        input_output_aliases={0: 0},
        scratch_shapes=[pltpu.VMEM((tm, d), BF16)],
        compiler_params=_params("parallel", "arbitrary"),
        name="ffn",
    )(y, x, gpre, wgb, wub, wdb, gpost)


def _inproj_kernel(x_ref, g_ref, wm_ref, ws_ref, wb_ref, o_ref):
    h = _rms(x_ref[...], g_ref[...]).astype(BF16)
    s5_at = wm_ref.shape[1]
    ba_at = s5_at + ws_ref.shape[1]
    o_ref[:, :s5_at] = jnp.dot(h, wm_ref[...], preferred_element_type=F32)
    o_ref[:, s5_at:ba_at] = jnp.dot(h, ws_ref[...], preferred_element_type=F32)
    o_ref[:, ba_at:] = jnp.dot(h, wb_ref[...], preferred_element_type=F32)


def _inproj(x, gain, w_all, w_s5, w_ba, layer, *, tm):
    n, d = x.shape
    resident = lambda shape: pl.BlockSpec((None,) + shape, lambda i: (layer, 0, 0), pipeline_mode=pl.Buffered(1))
    return pl.pallas_call(
        _inproj_kernel,
        grid=(n // tm,),
        in_specs=[
            pl.BlockSpec((tm, d), lambda i: (i, 0)),
            pl.BlockSpec((1, d), lambda i: (0, 0)),
            resident((d, SRC_BA)), resident((d, S5_WIDTH)), resident((d, LANE)),
        ],
        out_specs=pl.BlockSpec((tm, Z_WIDTH), lambda i: (i, 0)),
        out_shape=jax.ShapeDtypeStruct((n, Z_WIDTH), F32),
        compiler_params=_params("parallel"),
        name="inproj",
    )(x, gain, w_all, w_s5, w_ba)


def _outproj_kernel(x_ref, ya_ref, yd_ref, ys_ref, w_ref, g_ref, o_ref):
    ys = [ys_ref[j].astype(BF16) for j in range(ys_ref.shape[0])]
    y = jnp.concatenate([ya_ref[...], yd_ref[...]] + ys, axis=-1)
    mixed = jnp.dot(y, w_ref[...], preferred_element_type=F32)
    o_ref[...] = x_ref[...] + _rms(mixed, g_ref[...])


def _outproj(x, ya, yd, ys, w, gain, layer, *, tm):
    n, d = x.shape
    return pl.pallas_call(
        _outproj_kernel,
        grid=(n // tm,),
        in_specs=[
            pl.BlockSpec((tm, d), lambda i: (i, 0)),
            pl.BlockSpec((tm, ya.shape[1]), lambda i: (i, 0)),
            pl.BlockSpec((tm, yd.shape[1]), lambda i: (i, 0)),
            pl.BlockSpec((ys.shape[0], tm, ys.shape[2]), lambda i: (0, i, 0)),
            pl.BlockSpec((None,) + w.shape[1:], lambda i: (layer, 0, 0)),
            pl.BlockSpec((1, d), lambda i: (0, 0)),
        ],
        out_specs=pl.BlockSpec((tm, d), lambda i: (i, 0)),
        out_shape=jax.ShapeDtypeStruct((n, d), F32),
        compiler_params=_params("parallel"),
        name="outproj",
    )(x, ya, yd, ys, w, gain)


def _rope(x, cc, ss):
    return x * cc + pltpu.roll(x, HEAD_DIM // 2, axis=1) * ss


def _attn_kernel(sink_ref, q_ref, kp_ref, kc_ref, vp_ref, vc_ref, ccp_ref, ccc_ref, ssp_ref, ssc_ref, o_ref):
    h = pl.program_id(1)
    t = pl.program_id(2)
    w = WINDOW
    grp = ATTN_GROUP
    nblk = q_ref.shape[0] // w
    ccc = ccc_ref[...]
    ssc = ssc_ref[...]
    q = q_ref[...]
    scale = HEAD_DIM ** -0.5
    qr = [(_rope(q[:, g * HEAD_DIM:(g + 1) * HEAD_DIM], ccc, ssc) * scale).astype(BF16) for g in range(grp)]
    kk = jnp.concatenate([_rope(kp_ref[...], ccp_ref[...], ssp_ref[...]), _rope(kc_ref[...], ccc, ssc)],
                         axis=0).astype(BF16)
    vv = jnp.concatenate([vp_ref[...], vc_ref[...]], axis=0).astype(BF16)
    qi = lax.broadcasted_iota(jnp.int32, (w, 2 * w), 0) + w
    kj = lax.broadcasted_iota(jnp.int32, (w, 2 * w), 1)
    rel = qi - kj
    band = (rel >= 0) & (rel < w)
    first = band & ((kj >= w) | (t > 0))
    row_head = lax.broadcasted_iota(jnp.int32, (grp * w, 1), 0) // w
    sink = jnp.zeros((grp * w, 1), F32)
    for g in range(grp):
        sink = jnp.where(row_head == g, sink_ref[h * grp + g], sink)
    scores = []
    for blk in range(nblk):
        qs = jnp.concatenate([qr[g][blk * w:(blk + 1) * w] for g in range(grp)], axis=0)
        scores.append(_dot_nt(qs, kk[blk * w:(blk + 2) * w]))
    probs = []
    for blk in range(nblk):
        mask = jnp.concatenate([first if blk == 0 else band] * grp, axis=0)
        sc = jnp.where(mask, scores[blk], -jnp.inf)
        m = jnp.maximum(jnp.max(sc, axis=-1, keepdims=True), sink)
        p = jnp.exp(sc - m)
        denom = jnp.sum(p, axis=-1, keepdims=True) + jnp.exp(sink - m)
        probs.append((p / denom).astype(BF16))
    outs = [jnp.dot(probs[blk], vv[blk * w:(blk + 2) * w], preferred_element_type=F32) for blk in range(nblk)]
    for blk in range(nblk):
        o_ref[blk * w:(blk + 1) * w, :] = jnp.concatenate(
            [outs[blk][g * w:(g + 1) * w] for g in range(grp)], axis=1).astype(o_ref.dtype)


def _attention(z3, sinks, cc, ss, *, tile):
    b, s, _ = z3.shape
    w = WINDOW
    nblk = tile // w
    qw = ATTN_GROUP * HEAD_DIM
    kblk = ATTN_WIDTH // HEAD_DIM
    vblk = (ATTN_WIDTH + ATTN_KV_WIDTH) // HEAD_DIM
    prev = lambda t: jnp.maximum(t * nblk - 1, 0)
    return pl.pallas_call(
        _attn_kernel,
        grid=(b, ATTN_KV_HEADS, s // tile),
        in_specs=[
            pl.BlockSpec(memory_space=pltpu.SMEM),
            pl.BlockSpec((None, tile, qw), lambda bi, h, t: (bi, t, h)),
            pl.BlockSpec((None, w, HEAD_DIM), lambda bi, h, t: (bi, prev(t), kblk + h)),
            pl.BlockSpec((None, tile, HEAD_DIM), lambda bi, h, t: (bi, t, kblk + h)),
            pl.BlockSpec((None, w, HEAD_DIM), lambda bi, h, t: (bi, prev(t), vblk + h)),
            pl.BlockSpec((None, tile, HEAD_DIM), lambda bi, h, t: (bi, t, vblk + h)),
            pl.BlockSpec((w, HEAD_DIM), lambda bi, h, t: (prev(t), 0)),
            pl.BlockSpec((tile, HEAD_DIM), lambda bi, h, t: (t, 0)),
            pl.BlockSpec((w, HEAD_DIM), lambda bi, h, t: (prev(t), 0)),
            pl.BlockSpec((tile, HEAD_DIM), lambda bi, h, t: (t, 0)),
        ],
        out_specs=pl.BlockSpec((None, tile, qw), lambda bi, h, t: (bi, t, h)),
        out_shape=jax.ShapeDtypeStruct((b, s, ATTN_WIDTH), BF16),
        compiler_params=_params("parallel", "parallel", "arbitrary"),
        name="swa",
    )(sinks, z3, z3, z3, z3, z3, cc, cc, ss, ss)


def _silu(x):
    return x * _sigmoid(x)


def _softplus(x):
    return jnp.maximum(x, 0.0) + jnp.log(1.0 + jnp.exp(-jnp.abs(x)))


def _dot_nt(a, b, precision=None):
    return lax.dot_general(a, b, (((1,), (1,)), ((), ())), preferred_element_type=F32, precision=precision)


def _dot_tn(a, b, precision=None):
    return lax.dot_general(a, b, (((0,), (0,)), ((), ())), preferred_element_type=F32, precision=precision)


def _dn_kernel(q_ref, k_ref, v_ref, zg_ref, ba_ref, cw_ref, alog_ref, dtb_ref, nw_ref, o_ref,
               ext_ref, qn_ref, kn_ref, vn_ref, gc_ref, gct_ref, beta_ref, s_ref):
    t = pl.program_id(1)
    tile = q_ref.shape[0]
    c = DN_CHUNK
    d = HEAD_DIM
    nh = DN_HEADS
    wdt = nh * d

    @pl.when(t == 0)
    def _():
        ext_ref[0:8, :] = jnp.zeros((8, 3 * wdt), F32)
        s_ref[...] = jnp.zeros_like(s_ref)

    ext_ref[8:8 + tile, 0:wdt] = q_ref[...]
    ext_ref[8:8 + tile, wdt:2 * wdt] = k_ref[...]
    ext_ref[8:8 + tile, 2 * wdt:3 * wdt] = v_ref[...]
    ext = ext_ref[...]
    conv = ext[8:] * cw_ref[DN_CONV - 1:DN_CONV, :]
    for back in range(1, DN_CONV):
        conv = conv + pltpu.roll(ext, back, axis=0)[8:] * cw_ref[DN_CONV - 1 - back:DN_CONV - back, :]
    ext_ref[0:8, :] = ext[tile:tile + 8]
    qkv = _silu(conv)
    for hd in range(nh):
        qh = qkv[:, hd * d:(hd + 1) * d]
        kh = qkv[:, wdt + hd * d:wdt + (hd + 1) * d]
        qn_ref[:, hd * d:(hd + 1) * d] = qh * lax.rsqrt(jnp.sum(qh * qh, axis=-1, keepdims=True) + NORM_EPS) * (d ** -0.5)
        kn_ref[:, hd * d:(hd + 1) * d] = kh * lax.rsqrt(jnp.sum(kh * kh, axis=-1, keepdims=True) + NORM_EPS)
    vn_ref[...] = qkv[:, 2 * wdt:3 * wdt]
    ba = ba_ref[...]
    beta_ref[...] = _sigmoid(ba)
    g = -jnp.exp(alog_ref[...]) * _softplus(ba + dtb_ref[...])
    ti = lax.broadcasted_iota(jnp.int32, (c, c), 0)
    tj = lax.broadcasted_iota(jnp.int32, (c, c), 1)
    csum = (ti >= tj).astype(F32)
    gc_all = jnp.concatenate([jnp.dot(csum, g[ci * c:(ci + 1) * c], preferred_element_type=F32, precision=HIGHEST)
                              for ci in range(tile // c)], axis=0)
    gc_ref[...] = gc_all
    gct_ref[...] = gc_all.T[0:8, :]

    row = lax.broadcasted_iota(jnp.int32, (c, c), 0)
    col = lax.broadcasted_iota(jnp.int32, (c, c), 1)
    causal = row >= col
    strict = row > col
    eye = (row == col).astype(F32)
    nw = nw_ref[...]

    def join_mask(half):
        return (row // (2 * half) == col // (2 * half)) & (row % (2 * half) >= half) & (col % (2 * half) < half)

    def local_stages(chunks, items):
        for ci in chunks:
            r0 = ci * c
            for hd in range(nh):
                lanes = slice(hd * d, (hd + 1) * d)
                q = qn_ref[r0:r0 + c, lanes]
                k = kn_ref[r0:r0 + c, lanes]
                beta = beta_ref[r0:r0 + c, hd:hd + 1]
                gcol = gc_ref[r0:r0 + c, nh + hd:nh + hd + 1]
                grow = gct_ref[nh + hd:nh + hd + 1, r0:r0 + c]
                glast = gcol[c - 1:c, :]
                decay = jnp.exp(jnp.where(causal, gcol - grow, -jnp.inf))
                eg = jnp.exp(gcol)
                kb = k * beta
                items.append(dict(ci=ci, hd=hd, r0=r0, lanes=lanes, decay=decay, kbf=k.astype(BF16),
                                  qbf=q.astype(BF16), kb_bf=kb.astype(BF16),
                                  rhs=jnp.concatenate([vn_ref[r0:r0 + c, lanes] * beta, kb * eg], axis=1).astype(BF16),
                                  qd=(q * eg).astype(BF16), k_dec=(k * jnp.exp(glast - gcol)).astype(BF16),
                                  egl=jnp.exp(glast)))
        for it in items:
            it["kk"] = _dot_nt(it["kb_bf"], it["kbf"])
            it["qk"] = _dot_nt(it["qbf"], it["kbf"])
        yield
        for it in items:
            it["nmat"] = jnp.where(strict, it["kk"] * it["decay"], 0.0)
            it["inv"] = eye - jnp.where(join_mask(1), it["nmat"], 0.0)
            it["attn"] = (it["qk"] * it["decay"]).astype(BF16)
        half = 2
        while half < c:
            for it in items:
                it["inv_bf"] = it["inv"].astype(BF16)
                join = jnp.where(join_mask(half), it["nmat"], 0.0).astype(BF16)
                it["bt"] = jnp.dot(join, it["inv_bf"], preferred_element_type=F32).astype(BF16)
            yield
            for it in items:
                it["upd"] = jnp.dot(it["inv_bf"], it["bt"], preferred_element_type=F32)
            yield
            for it in items:
                it["inv"] = it["inv"] - it["upd"]
            half *= 2
        for it in items:
            uw = jnp.dot(it["inv"].astype(BF16), it["rhs"], preferred_element_type=F32)
            it["u"] = uw[:, :d]
            it["wq"] = jnp.concatenate([uw[:, d:].astype(BF16), it["qd"]], axis=0)
        yield

    def sweep_stages(items, states):
        for ci in sorted({it["ci"] for it in items}):
            group = [it for it in items if it["ci"] == ci]
            for it in group:
                it["ws_qs"] = jnp.dot(it["wq"], states[it["hd"]].astype(BF16), preferred_element_type=F32)
            yield
            for it in group:
                it["v_new"] = (it["u"] - it["ws_qs"][:c]).astype(BF16)
            for it in group:
                it["av"] = jnp.dot(it["attn"], it["v_new"], preferred_element_type=F32)
                it["kv"] = _dot_tn(it["k_dec"], it["v_new"])
            yield
            for it in group:
                states[it["hd"]] = states[it["hd"]] * it["egl"] + it["kv"]
                o = it["ws_qs"][c:] + it["av"]
                o = o * lax.rsqrt(jnp.mean(o * o, axis=-1, keepdims=True) + NORM_EPS) * nw
                o = o * _silu(zg_ref[it["r0"]:it["r0"] + c, it["lanes"]])
                o_ref[it["r0"]:it["r0"] + c, it["lanes"]] = o.astype(o_ref.dtype)

    def interleave(*gens):
        live = list(gens)
        while live:
            for gen in list(live):
                if next(gen, "end") == "end":
                    live.remove(gen)

    states = [s_ref[hd] for hd in range(nh)]
    group_chunks = 2
    groups = [list(range(g0, g0 + group_chunks)) for g0 in range(0, tile // c, group_chunks)]
    ready = []
    interleave(local_stages(groups[0], ready))
    for nxt in groups[1:]:
        upcoming = []
        interleave(local_stages(nxt, upcoming), sweep_stages(ready, states))
        ready = upcoming
    interleave(sweep_stages(ready, states))
    for hd in range(nh):
        s_ref[hd] = states[hd]


def _deltanet(z3, conv_w, alog_pad, dtb_pad, norm_w, *, tile):
    b, s, _ = z3.shape
    wdt = DN_WIDTH
    qblk = (ATTN_WIDTH + 2 * ATTN_KV_WIDTH) // wdt
    seq = lambda off: pl.BlockSpec((None, tile, wdt), lambda bi, t: (bi, t, qblk + off))
    const = lambda shape: pl.BlockSpec(shape, lambda bi, t: (0, 0))
    return pl.pallas_call(
        _dn_kernel,
        grid=(b, s // tile),
        in_specs=[
            seq(0), seq(1), seq(2), seq(3),
            pl.BlockSpec((None, tile, LANE), lambda bi, t: (bi, t, Z_BA_BLOCK)),
            const(conv_w.shape), const((1, LANE)), const((1, LANE)), const((1, HEAD_DIM)),
        ],
        out_specs=pl.BlockSpec((None, tile, wdt), lambda bi, t: (bi, t, 0)),
        out_shape=jax.ShapeDtypeStruct((b, s, wdt), BF16),
        scratch_shapes=[
            pltpu.VMEM((tile + 8, 3 * wdt), F32),
            pltpu.VMEM((tile, wdt), F32),
            pltpu.VMEM((tile, wdt), F32),
            pltpu.VMEM((tile, wdt), F32),
            pltpu.VMEM((tile, LANE), F32),
            pltpu.VMEM((8, tile), F32),
            pltpu.VMEM((tile, LANE), F32),
            pltpu.VMEM((DN_HEADS, HEAD_DIM, HEAD_DIM), F32),
        ],
        compiler_params=_params("parallel", "arbitrary"),
        name="deltanet",
    )(z3, z3, z3, z3, z3, conv_w, alog_pad, dtb_pad, norm_w)


def _s5_prep_kernel(are_ref, aim_ref, ldt_ref, bre_ref, bim_ref, apr_ref, api_ref, bbr_ref, bbi_ref):
    lr = are_ref[...]
    li = aim_ref[...]
    dt = jnp.exp(ldt_ref[...])
    step = (lax.broadcasted_iota(jnp.int32, (S5_SEG, 1), 0) + 1).astype(F32)
    mag = jnp.exp(step * (lr * dt))
    ang = step * (li * dt)
    apr_ref[...] = mag * jnp.cos(ang)
    api_ref[...] = mag * jnp.sin(ang)
    m1 = jnp.exp(lr * dt)
    nr = m1 * jnp.cos(li * dt) - 1.0
    ni = m1 * jnp.sin(li * dt)
    den = lr * lr + li * li
    cr = (nr * lr + ni * li) / den
    ci = (ni * lr - nr * li) / den
    br = bre_ref[...]
    bi = bim_ref[...]
    bbr_ref[...] = cr * br - ci * bi
    bbi_ref[...] = cr * bi + ci * br


def _s5_prep(a_re, a_im, log_dt, b_re, b_im):
    depth = a_re.shape[0]
    row = lambda a: a.reshape(depth, 1, S5_LANES)
    ldt = jnp.repeat(log_dt, S5_STATE, axis=-1).reshape(depth, 1, S5_LANES)
    tr = lambda a: a.reshape(depth, S5_LANES, S5_GROUP_CH).transpose(0, 2, 1)
    shp = jax.ShapeDtypeStruct
    per_layer = lambda rows: pl.BlockSpec((None, rows, S5_LANES), lambda l: (l, 0, 0))
    return pl.pallas_call(
        _s5_prep_kernel,
        grid=(depth,),
        in_specs=[per_layer(1), per_layer(1), per_layer(1), per_layer(S5_GROUP_CH), per_layer(S5_GROUP_CH)],
        out_specs=[per_layer(S5_SEG), per_layer(S5_SEG), per_layer(S5_GROUP_CH), per_layer(S5_GROUP_CH)],
        out_shape=(shp((depth, S5_SEG, S5_LANES), F32), shp((depth, S5_SEG, S5_LANES), F32),
                   shp((depth, S5_GROUP_CH, S5_LANES), F32), shp((depth, S5_GROUP_CH, S5_LANES), F32)),
        name="s5_prep",
    )(row(a_re), row(a_im), ldt, tr(b_re), tr(b_im))


def _s5_kernel(u0_ref, u1_ref, u2_ref, u3_ref, bre_ref, bim_ref, cre_ref, cim_ref, apr_ref, api_ref, d_ref,
               gw_ref, gb_ref, o_ref, up_ref, xr_ref, xi_ref, xrb_ref, xib_ref, cr_ref, ci_ref, st_ref, y_ref):
    t = pl.program_id(1)
    tile = u0_ref.shape[0]
    seg = tile // 8
    strip = 512

    @pl.when(t == 0)
    def _():
        st_ref[...] = jnp.zeros_like(st_ref)

    for j, u_ref in enumerate((u0_ref, u1_ref, u2_ref, u3_ref)):
        for k in range(seg):
            up_ref[8 * k:8 * k + 8, j * LANE:(j + 1) * LANE] = u_ref[pl.ds(k, 8, stride=seg), :]
    hw, hl = S5_WIDTH // 2, S5_LANES // 2
    for half in range(2):
        ub = up_ref[:, half * hw:(half + 1) * hw].astype(BF16)
        rows, cols = slice(half * hw, (half + 1) * hw), slice(half * hl, (half + 1) * hl)
        xr_ref[:, cols] = jnp.dot(ub, bre_ref[rows, cols], preferred_element_type=F32)
        xi_ref[:, cols] = jnp.dot(ub, bim_ref[rows, cols], preferred_element_type=F32)

    for s0 in range(0, S5_LANES, strip):
        lanes = slice(s0, s0 + strip)
        ar = jnp.broadcast_to(apr_ref[0:1, lanes], (8, strip))
        ai = jnp.broadcast_to(api_ref[0:1, lanes], (8, strip))

        def scan(k, carry):
            pr, pi = carry
            r0 = pl.multiple_of(k * 8, 8)
            nr = ar * pr - ai * pi + xr_ref[pl.ds(r0, 8), lanes]
            ni = ar * pi + ai * pr + xi_ref[pl.ds(r0, 8), lanes]
            xr_ref[pl.ds(r0, 8), lanes] = nr
            xi_ref[pl.ds(r0, 8), lanes] = ni
            return nr, ni

        zero = jnp.zeros((8, strip), F32)
        fr, fi = lax.fori_loop(0, seg, scan, (zero, zero), unroll=4)

        a64r = apr_ref[seg - 1:seg, lanes]
        a64i = api_ref[seg - 1:seg, lanes]
        c_r = st_ref[0:1, lanes]
        c_i = st_ref[1:2, lanes]
        for r in range(8):
            cr_ref[r:r + 1, lanes] = c_r
            ci_ref[r:r + 1, lanes] = c_i
            n_r = a64r * c_r - a64i * c_i + fr[r:r + 1]
            n_i = a64r * c_i + a64i * c_r + fi[r:r + 1]
            c_r, c_i = n_r, n_i
        st_ref[0:1, lanes] = c_r
        st_ref[1:2, lanes] = c_i

        cin_r = jnp.concatenate([cr_ref[:, lanes]] * 2, axis=0)
        cin_i = jnp.concatenate([ci_ref[:, lanes]] * 2, axis=0)

        def fix(k2, carry):
            r0 = pl.multiple_of(k2 * 16, 16)
            pw = lambda ref, k: jnp.broadcast_to(ref[pl.ds(k, 1), lanes], (8, strip))
            pr = jnp.concatenate([pw(apr_ref, 2 * k2), pw(apr_ref, 2 * k2 + 1)], axis=0)
            pi = jnp.concatenate([pw(api_ref, 2 * k2), pw(api_ref, 2 * k2 + 1)], axis=0)
            xrb_ref[pl.ds(r0, 16), lanes] = (xr_ref[pl.ds(r0, 16), lanes] + pr * cin_r - pi * cin_i).astype(BF16)
            xib_ref[pl.ds(r0, 16), lanes] = (xi_ref[pl.ds(r0, 16), lanes] + pr * cin_i + pi * cin_r).astype(BF16)
            return carry

        lax.fori_loop(0, seg // 2, fix, 0, unroll=2)

    ys = []
    for half in range(2):
        rows, cols = slice(half * hl, (half + 1) * hl), slice(half * hw, (half + 1) * hw)
        ys.append(jnp.dot(xrb_ref[:, rows], cre_ref[rows, cols], preferred_element_type=F32)
                  - jnp.dot(xib_ref[:, rows], cim_ref[rows, cols], preferred_element_type=F32))
    y = jnp.concatenate(ys, axis=1)
    y = y + d_ref[...] * up_ref[...]
    y = 0.5 * y * (1.0 + jnp.tanh(math.sqrt(2.0 / math.pi) * (y + 0.044715 * (y * y * y))))
    gate = jnp.dot(y.astype(BF16), gw_ref[...], preferred_element_type=F32) + gb_ref[...]
    y_ref[...] = y * _sigmoid(gate)
    for j in range(S5_WIDTH // LANE):
        for k in range(seg):
            o_ref[j, pl.ds(k, 8, stride=seg), :] = y_ref[8 * k:8 * k + 8, j * LANE:(j + 1) * LANE]


def _s5(z3, b_re, b_im, c_re, c_im, ap_re, ap_im, d_skip, glu_w, glu_b, layer, *, tile):
    b, s, _ = z3.shape
    nblk = S5_WIDTH // LANE
    ublk = (ATTN_WIDTH + 2 * ATTN_KV_WIDTH + 4 * DN_WIDTH) // LANE
    const = lambda a: pl.BlockSpec(a.shape, lambda bi, t: (0, 0))
    layered = lambda a: pl.BlockSpec((None,) + a.shape[1:], lambda bi, t: (layer, 0, 0))
    ucol = lambda j: pl.BlockSpec((None, tile, LANE), lambda bi, t: (bi, t, ublk + j))
    return pl.pallas_call(
        _s5_kernel,
        grid=(b, s // tile),
        in_specs=[
            ucol(0), ucol(1), ucol(2), ucol(3),
            layered(b_re), layered(b_im), layered(c_re), layered(c_im), layered(ap_re), layered(ap_im),
            const(d_skip), layered(glu_w), const(glu_b),
        ],
        out_specs=pl.BlockSpec((nblk, None, tile, LANE), lambda bi, t: (0, bi, t, 0)),
        out_shape=jax.ShapeDtypeStruct((nblk, b, s, LANE), F32),
        scratch_shapes=[
            pltpu.VMEM((tile, S5_WIDTH), F32),
            pltpu.VMEM((tile, S5_LANES), F32),
            pltpu.VMEM((tile, S5_LANES), F32),
            pltpu.VMEM((tile, S5_LANES), BF16),
            pltpu.VMEM((tile, S5_LANES), BF16),
            pltpu.VMEM((8, S5_LANES), F32),
            pltpu.VMEM((8, S5_LANES), F32),
            pltpu.VMEM((8, S5_LANES), F32),
            pltpu.VMEM((tile, S5_WIDTH), F32),
        ],
        compiler_params=_params("parallel", "arbitrary"),
        name="s5",
    )(z3, z3, z3, z3, b_re, b_im, c_re, c_im, ap_re, ap_im, d_skip, glu_w, glu_b)


def _block_diag_in(bb):
    eye = jnp.eye(S5_GROUPS, dtype=bb.dtype)
    t = bb.reshape(-1, S5_GROUP_CH, S5_GROUPS, S5_STATE)
    return jnp.einsum('lhgp,fg->lfhgp', t, eye).reshape(-1, S5_WIDTH, S5_LANES)


def _block_diag_out(cc):
    eye = jnp.eye(S5_GROUPS, dtype=cc.dtype)
    return jnp.einsum('lghp,gf->lgpfh', cc, eye).reshape(-1, S5_LANES, S5_WIDTH)


def _rope_tables(seq):
    half = HEAD_DIM // 2
    inv_freq = ROPE_THETA ** (-jnp.arange(half, dtype=F32) / half)
    ang = jnp.arange(seq, dtype=F32)[:, None] * inv_freq[None, :]
    cos, sin = jnp.cos(ang), jnp.sin(ang)
    return jnp.concatenate([cos, cos], axis=-1), jnp.concatenate([-sin, sin], axis=-1)


def _split_w_in(w_in):
    s5_src = SRC_BA + 2 * DN_HEADS
    pad = jnp.zeros(w_in.shape[:-1] + (LANE - 2 * DN_HEADS,), BF16)
    w_ba = jnp.concatenate([w_in[..., SRC_BA:s5_src].astype(BF16), pad], axis=-1)
    return w_in[..., :SRC_BA].astype(BF16), w_in[..., s5_src:].astype(BF16), w_ba


def _lane_pad(v, offset):
    return jnp.zeros((v.shape[0], 1, LANE), F32).at[:, 0, offset:offset + v.shape[1]].set(v)


def kernel(x, ff1_norm_pre, ff1_w_gate, ff1_w_up, ff1_w_down, ff1_norm_post, mix_norm_pre, w_in,
           attn_sinks, dn_conv_w, dn_a_log, dn_dt_bias, dn_norm_w, s5_a_re, s5_a_im, s5_log_dt,
           s5_b_re, s5_b_im, s5_c_re, s5_c_im, s5_d, s5_glu_w, s5_glu_b, w_out, mix_norm_post,
           ff2_norm_pre, ff2_w_gate, ff2_w_up, ff2_w_down, ff2_norm_post):
    b, s, d = x.shape
    depth = w_in.shape[0]
    n = b * s
    tm = min(512, n)
    tm_big = min(1024, n)
    tf = 512
    tile = min(SEQ_TILE, s)

    bf = lambda a: a.astype(BF16)
    ff1 = (ff1_w_gate, ff1_w_up, ff1_w_down)
    ff2 = (ff2_w_gate, ff2_w_up, ff2_w_down)
    w_in_parts = _split_w_in(w_in)
    w_out_b = bf(w_out)
    glu_w_b = bf(s5_glu_w)
    ap_re, ap_im, bb_re, bb_im = _s5_prep(s5_a_re, s5_a_im, s5_log_dt, s5_b_re, s5_b_im)
    s5_mats = (bf(_block_diag_in(bb_re)), bf(_block_diag_in(bb_im)),
               bf(_block_diag_out(s5_c_re)), bf(_block_diag_out(s5_c_im)))
    cc, ss = _rope_tables(s)
    alog_pad = _lane_pad(dn_a_log, DN_HEADS)
    dtb_pad = _lane_pad(dn_dt_bias, DN_HEADS)
    row = lambda a, l: a[l].reshape(1, -1)

    xf = x.reshape(n, d)
    for l in range(depth):
        xf = _ffn(xf, row(ff1_norm_pre, l), *ff1, row(ff1_norm_post, l), l, tm=tm_big, tf=tf, tf_head=tf // 2)

        z3 = _inproj(xf, row(mix_norm_pre, l), *w_in_parts, l, tm=tm).reshape(b, s, Z_WIDTH)
        y_attn = _attention(z3, attn_sinks[l], cc, ss, tile=tile)
        y_dn = _deltanet(z3, dn_conv_w[l], alog_pad[l], dtb_pad[l], row(dn_norm_w, l), tile=tile)
        y_s5 = _s5(z3, *s5_mats, ap_re, ap_im, row(s5_d, l), glu_w_b, row(s5_glu_b, l), l, tile=tile)
        xf = _outproj(xf, y_attn.reshape(n, -1), y_dn.reshape(n, -1), y_s5.reshape(-1, n, LANE),
                      w_out_b, row(mix_norm_post, l), l, tm=tm)

        xf = _ffn(xf, row(ff2_norm_pre, l), *ff2, row(ff2_norm_post, l), l, tm=tm_big, tf=tf, tf_head=tf // 2)
    return xf.reshape(b, s, d)
```

```python
import functools
import math

import jax
import jax.numpy as jnp
from jax import lax
from jax.experimental import pallas as pl
from jax.experimental.pallas import tpu as pltpu

F32 = jnp.float32
BF16 = jnp.bfloat16
HIGHEST = lax.Precision.HIGHEST

NORM_EPS = 1e-6
FFN_RES_WEIGHT = 0.5
ROPE_THETA = 10000.0

HEAD_DIM = 128
WINDOW = 128
ATTN_HEADS = 8
ATTN_KV_HEADS = 2
ATTN_GROUP = ATTN_HEADS // ATTN_KV_HEADS
DN_HEADS = 4
DN_CONV = 4
DN_CHUNK = 64
S5_GROUPS = 32
S5_GROUP_CH = 16
S5_STATE = 64
ATTN_WIDTH = ATTN_HEADS * HEAD_DIM
ATTN_KV_WIDTH = ATTN_KV_HEADS * HEAD_DIM
DN_WIDTH = DN_HEADS * HEAD_DIM
S5_WIDTH = S5_GROUPS * S5_GROUP_CH
S5_LANES = S5_GROUPS * S5_STATE
MIX_WIDTH = ATTN_WIDTH + DN_WIDTH + S5_WIDTH

LANE = 128
Z_WIDTH = ATTN_WIDTH + 2 * ATTN_KV_WIDTH + 4 * DN_WIDTH + S5_WIDTH + LANE
Z_BA_BLOCK = (Z_WIDTH - LANE) // LANE
SRC_BA = ATTN_WIDTH + 2 * ATTN_KV_WIDTH + 4 * DN_WIDTH

SEQ_TILE = 512
S5_SEG = SEQ_TILE // 8
FFN_DOWN_COLS = 512
VMEM_LIMIT = 56 * 1024 * 1024


def _params(*sem):
    return pltpu.CompilerParams(dimension_semantics=sem, vmem_limit_bytes=VMEM_LIMIT)


def _rms(x, gain):
    return x * lax.rsqrt(jnp.mean(x * x, axis=-1, keepdims=True) + NORM_EPS) * gain


def _sigmoid(x):
    return 1.0 / (1.0 + jnp.exp(-x))


def _ffn_kernel(*refs, cast_weights):
    if cast_weights:
        x_ref, gpre_ref, wg_ref, wu_ref, wd_ref, gpost_ref, o_ref, wgb_ref, wub_ref, wdb_ref, h_ref = refs
        wgb_ref[...] = wg_ref[...].astype(BF16)
        wub_ref[...] = wu_ref[...].astype(BF16)
        wdb_ref[...] = wd_ref[...].astype(BF16)
        wg_ref, wu_ref, wd_ref = wgb_ref, wub_ref, wdb_ref
    else:
        x_ref, gpre_ref, wg_ref, wu_ref, wd_ref, gpost_ref, o_ref, h_ref = refs
    j = pl.program_id(1)

    @pl.when(j == 0)
    def _():
        h_ref[...] = _rms(x_ref[...], gpre_ref[...]).astype(BF16)
        o_ref[...] = jnp.zeros_like(o_ref)

    h = h_ref[...]
    g = jnp.dot(h, wg_ref[...], preferred_element_type=F32)
    u = jnp.dot(h, wu_ref[...], preferred_element_type=F32)
    a = (g * _sigmoid(g) * u).astype(BF16)
    chunk = min(FFN_DOWN_COLS, o_ref.shape[1])
    for c0 in range(0, o_ref.shape[1], chunk):
        cols = slice(c0, c0 + chunk)
        o_ref[:, cols] += jnp.dot(a, wd_ref[:, cols], preferred_element_type=F32)

    @pl.when(j == pl.num_programs(1) - 1)
    def _():
        o_ref[...] = x_ref[...] + FFN_RES_WEIGHT * _rms(o_ref[...], gpost_ref[...])


def _ffn(x, gpre, wg, wu, wd, gpost, layer, *, tm, tf, tf_head):
    n, d = x.shape
    f = wg.shape[2]
    once = pl.Buffered(1)
    vec = pl.BlockSpec((1, d), lambda i, j: (0, 0))
    shp = jax.ShapeDtypeStruct
    y, wgb, wub, wdb = pl.pallas_call(
        functools.partial(_ffn_kernel, cast_weights=True),
        grid=(1, f // tf_head),
        in_specs=[
            pl.BlockSpec((tm, d), lambda i, j: (0, 0), pipeline_mode=once),
            vec,
            pl.BlockSpec((None, d, tf_head), lambda i, j: (layer, 0, j)),
            pl.BlockSpec((None, d, tf_head), lambda i, j: (layer, 0, j)),
            pl.BlockSpec((None, tf_head, d), lambda i, j: (layer, j, 0)),
            vec,
        ],
        out_specs=[
            pl.BlockSpec((tm, d), lambda i, j: (0, 0), pipeline_mode=once),
            pl.BlockSpec((d, tf_head), lambda i, j: (0, j)),
            pl.BlockSpec((d, tf_head), lambda i, j: (0, j)),
            pl.BlockSpec((tf_head, d), lambda i, j: (j, 0)),
        ],
        out_shape=(shp((n, d), F32), shp((d, f), BF16), shp((d, f), BF16), shp((f, d), BF16)),
        input_output_aliases={0: 0},
        scratch_shapes=[pltpu.VMEM((tm, d), BF16)],
        compiler_params=_params("arbitrary", "arbitrary"),
        name="ffn_head",
    )(x, gpre, wg, wu, wd, gpost)
    return pl.pallas_call(
        functools.partial(_ffn_kernel, cast_weights=False),
        grid=(n // tm - 1, f // tf),
        in_specs=[
            pl.BlockSpec((tm, d), lambda i, j: (i + 1, 0), pipeline_mode=once),
            vec,
            pl.BlockSpec((d, tf), lambda i, j: (0, j)),
            pl.BlockSpec((d, tf), lambda i, j: (0, j)),
            pl.BlockSpec((tf, d), lambda i, j: (j, 0)),
            vec,
        ],
        out_specs=pl.BlockSpec((tm, d), lambda i, j: (i + 1, 0)),
        out_shape=shp((n, d), F32),
        input_output_aliases={0: 0},
        scratch_shapes=[pltpu.VMEM((tm, d), BF16)],
        compiler_params=_params("parallel", "arbitrary"),
        name="ffn",
    )(y, gpre, wgb, wub, wdb, gpost)


def _inproj_kernel(x_ref, g_ref, wm_ref, ws_ref, wb_ref, o_ref):
    h = _rms(x_ref[...], g_ref[...]).astype(BF16)
    s5_at = wm_ref.shape[1]
    ba_at = s5_at + ws_ref.shape[1]
    o_ref[:, :s5_at] = jnp.dot(h, wm_ref[...], preferred_element_type=F32)
    o_ref[:, s5_at:ba_at] = jnp.dot(h, ws_ref[...], preferred_element_type=F32)
    o_ref[:, ba_at:] = jnp.dot(h, wb_ref[...], preferred_element_type=F32)


def _inproj(x, gain, w_all, w_s5, w_ba, layer, *, tm):
    n, d = x.shape
    resident = lambda shape: pl.BlockSpec((None,) + shape, lambda i: (layer, 0, 0), pipeline_mode=pl.Buffered(1))
    return pl.pallas_call(
        _inproj_kernel,
        grid=(n // tm,),
        in_specs=[
            pl.BlockSpec((tm, d), lambda i: (i, 0)),
            pl.BlockSpec((1, d), lambda i: (0, 0)),
            resident((d, SRC_BA)), resident((d, S5_WIDTH)), resident((d, LANE)),
        ],
        out_specs=pl.BlockSpec((tm, Z_WIDTH), lambda i: (i, 0)),
        out_shape=jax.ShapeDtypeStruct((n, Z_WIDTH), F32),
        compiler_params=_params("parallel"),
        name="inproj",
    )(x, gain, w_all, w_s5, w_ba)


def _outproj_kernel(x_ref, ya_ref, yd_ref, ys_ref, w_ref, g_ref, o_ref):
    ys = [ys_ref[j].astype(BF16) for j in range(ys_ref.shape[0])]
    y = jnp.concatenate([ya_ref[...], yd_ref[...]] + ys, axis=-1)
    mixed = jnp.dot(y, w_ref[...], preferred_element_type=F32)
    o_ref[...] = x_ref[...] + _rms(mixed, g_ref[...])


def _outproj(x, ya, yd, ys, w, gain, layer, *, tm):
    n, d = x.shape
    return pl.pallas_call(
        _outproj_kernel,
        grid=(n // tm,),
        in_specs=[
            pl.BlockSpec((tm, d), lambda i: (i, 0)),
            pl.BlockSpec((tm, ya.shape[1]), lambda i: (i, 0)),
            pl.BlockSpec((tm, yd.shape[1]), lambda i: (i, 0)),
            pl.BlockSpec((ys.shape[0], tm, ys.shape[2]), lambda i: (0, i, 0)),
            pl.BlockSpec((None,) + w.shape[1:], lambda i: (layer, 0, 0)),
            pl.BlockSpec((1, d), lambda i: (0, 0)),
        ],
        out_specs=pl.BlockSpec((tm, d), lambda i: (i, 0)),
        out_shape=jax.ShapeDtypeStruct((n, d), F32),
        compiler_params=_params("parallel"),
        name="outproj",
    )(x, ya, yd, ys, w, gain)


def _rope(x, cc, ss):
    return x * cc + pltpu.roll(x, HEAD_DIM // 2, axis=1) * ss


def _attn_kernel(sink_ref, q_ref, kp_ref, kc_ref, vp_ref, vc_ref, ccp_ref, ccc_ref, ssp_ref, ssc_ref, o_ref):
    h = pl.program_id(1)
    t = pl.program_id(2)
    w = WINDOW
    grp = ATTN_GROUP
    nblk = q_ref.shape[0] // w
    ccc = ccc_ref[...]
    ssc = ssc_ref[...]
    q = q_ref[...]
    scale = HEAD_DIM ** -0.5
    qr = [(_rope(q[:, g * HEAD_DIM:(g + 1) * HEAD_DIM], ccc, ssc) * scale).astype(BF16) for g in range(grp)]
    kk = jnp.concatenate([_rope(kp_ref[...], ccp_ref[...], ssp_ref[...]), _rope(kc_ref[...], ccc, ssc)],
                         axis=0).astype(BF16)
    vv = jnp.concatenate([vp_ref[...], vc_ref[...]], axis=0).astype(BF16)
    qi = lax.broadcasted_iota(jnp.int32, (w, 2 * w), 0) + w
    kj = lax.broadcasted_iota(jnp.int32, (w, 2 * w), 1)
    rel = qi - kj
    band = (rel >= 0) & (rel < w)
    first = band & ((kj >= w) | (t > 0))
    row_head = lax.broadcasted_iota(jnp.int32, (grp * w, 1), 0) // w
    sink = jnp.zeros((grp * w, 1), F32)
    for g in range(grp):
        sink = jnp.where(row_head == g, sink_ref[h * grp + g], sink)
    scores = []
    for blk in range(nblk):
        qs = jnp.concatenate([qr[g][blk * w:(blk + 1) * w] for g in range(grp)], axis=0)
        scores.append(_dot_nt(qs, kk[blk * w:(blk + 2) * w]))
    probs = []
    for blk in range(nblk):
        mask = jnp.concatenate([first if blk == 0 else band] * grp, axis=0)
        sc = jnp.where(mask, scores[blk], -jnp.inf)
        m = jnp.maximum(jnp.max(sc, axis=-1, keepdims=True), sink)
        p = jnp.exp(sc - m)
        denom = jnp.sum(p, axis=-1, keepdims=True) + jnp.exp(sink - m)
        probs.append((p / denom).astype(BF16))
    outs = [jnp.dot(probs[blk], vv[blk * w:(blk + 2) * w], preferred_element_type=F32) for blk in range(nblk)]
    for blk in range(nblk):
        o_ref[blk * w:(blk + 1) * w, :] = jnp.concatenate(
            [outs[blk][g * w:(g + 1) * w] for g in range(grp)], axis=1).astype(o_ref.dtype)


def _attention(z3, sinks, cc, ss, *, tile):
    b, s, _ = z3.shape
    w = WINDOW
    nblk = tile // w
    qw = ATTN_GROUP * HEAD_DIM
    kblk = ATTN_WIDTH // HEAD_DIM
    vblk = (ATTN_WIDTH + ATTN_KV_WIDTH) // HEAD_DIM
    prev = lambda t: jnp.maximum(t * nblk - 1, 0)
    return pl.pallas_call(
        _attn_kernel,
        grid=(b, ATTN_KV_HEADS, s // tile),
        in_specs=[
            pl.BlockSpec(memory_space=pltpu.SMEM),
            pl.BlockSpec((None, tile, qw), lambda bi, h, t: (bi, t, h)),
            pl.BlockSpec((None, w, HEAD_DIM), lambda bi, h, t: (bi, prev(t), kblk + h)),
            pl.BlockSpec((None, tile, HEAD_DIM), lambda bi, h, t: (bi, t, kblk + h)),
            pl.BlockSpec((None, w, HEAD_DIM), lambda bi, h, t: (bi, prev(t), vblk + h)),
            pl.BlockSpec((None, tile, HEAD_DIM), lambda bi, h, t: (bi, t, vblk + h)),
            pl.BlockSpec((w, HEAD_DIM), lambda bi, h, t: (prev(t), 0)),
            pl.BlockSpec((tile, HEAD_DIM), lambda bi, h, t: (t, 0)),
            pl.BlockSpec((w, HEAD_DIM), lambda bi, h, t: (prev(t), 0)),
            pl.BlockSpec((tile, HEAD_DIM), lambda bi, h, t: (t, 0)),
        ],
        out_specs=pl.BlockSpec((None, tile, qw), lambda bi, h, t: (bi, t, h)),
        out_shape=jax.ShapeDtypeStruct((b, s, ATTN_WIDTH), BF16),
        compiler_params=_params("parallel", "parallel", "arbitrary"),
        name="swa",
    )(sinks, z3, z3, z3, z3, z3, cc, cc, ss, ss)


def _silu(x):
    return x * _sigmoid(x)


def _softplus(x):
    return jnp.maximum(x, 0.0) + jnp.log(1.0 + jnp.exp(-jnp.abs(x)))


def _dot_nt(a, b, precision=None):
    return lax.dot_general(a, b, (((1,), (1,)), ((), ())), preferred_element_type=F32, precision=precision)


def _dot_tn(a, b, precision=None):
    return lax.dot_general(a, b, (((0,), (0,)), ((), ())), preferred_element_type=F32, precision=precision)


def _dn_kernel(q_ref, k_ref, v_ref, zg_ref, ba_ref, cw_ref, alog_ref, dtb_ref, nw_ref, o_ref,
               ext_ref, qn_ref, kn_ref, vn_ref, gc_ref, gct_ref, beta_ref, s_ref):
    t = pl.program_id(1)
    tile = q_ref.shape[0]
    c = DN_CHUNK
    d = HEAD_DIM
    nh = DN_HEADS
    wdt = nh * d

    @pl.when(t == 0)
    def _():
        ext_ref[0:8, :] = jnp.zeros((8, 3 * wdt), F32)
        s_ref[...] = jnp.zeros_like(s_ref)

    ext_ref[8:8 + tile, 0:wdt] = q_ref[...]
    ext_ref[8:8 + tile, wdt:2 * wdt] = k_ref[...]
    ext_ref[8:8 + tile, 2 * wdt:3 * wdt] = v_ref[...]
    ext = ext_ref[...]
    conv = ext[8:] * cw_ref[DN_CONV - 1:DN_CONV, :]
    for back in range(1, DN_CONV):
        conv = conv + pltpu.roll(ext, back, axis=0)[8:] * cw_ref[DN_CONV - 1 - back:DN_CONV - back, :]
    ext_ref[0:8, :] = ext[tile:tile + 8]
    qkv = _silu(conv)
    for hd in range(nh):
        qh = qkv[:, hd * d:(hd + 1) * d]
        kh = qkv[:, wdt + hd * d:wdt + (hd + 1) * d]
        qn_ref[:, hd * d:(hd + 1) * d] = qh * lax.rsqrt(jnp.sum(qh * qh, axis=-1, keepdims=True) + NORM_EPS) * (d ** -0.5)
        kn_ref[:, hd * d:(hd + 1) * d] = kh * lax.rsqrt(jnp.sum(kh * kh, axis=-1, keepdims=True) + NORM_EPS)
    vn_ref[...] = qkv[:, 2 * wdt:3 * wdt]
    ba = ba_ref[...]
    beta_ref[...] = _sigmoid(ba)
    g = -jnp.exp(alog_ref[...]) * _softplus(ba + dtb_ref[...])
    ti = lax.broadcasted_iota(jnp.int32, (c, c), 0)
    tj = lax.broadcasted_iota(jnp.int32, (c, c), 1)
    csum = (ti >= tj).astype(F32)
    gc_all = jnp.concatenate([jnp.dot(csum, g[ci * c:(ci + 1) * c], preferred_element_type=F32, precision=HIGHEST)
                              for ci in range(tile // c)], axis=0)
    gc_ref[...] = gc_all
    gct_ref[...] = gc_all.T[0:8, :]

    row = lax.broadcasted_iota(jnp.int32, (c, c), 0)
    col = lax.broadcasted_iota(jnp.int32, (c, c), 1)
    causal = row >= col
    strict = row > col
    eye = (row == col).astype(F32)
    nw = nw_ref[...]

    def join_mask(half):
        return (row // (2 * half) == col // (2 * half)) & (row % (2 * half) >= half) & (col % (2 * half) < half)

    def local_stages(chunks, items):
        for ci in chunks:
            r0 = ci * c
            for hd in range(nh):
                lanes = slice(hd * d, (hd + 1) * d)
                q = qn_ref[r0:r0 + c, lanes]
                k = kn_ref[r0:r0 + c, lanes]
                beta = beta_ref[r0:r0 + c, hd:hd + 1]
                gcol = gc_ref[r0:r0 + c, nh + hd:nh + hd + 1]
                grow = gct_ref[nh + hd:nh + hd + 1, r0:r0 + c]
                glast = gcol[c - 1:c, :]
                decay = jnp.exp(jnp.where(causal, gcol - grow, -jnp.inf))
                eg = jnp.exp(gcol)
                kb = k * beta
                items.append(dict(ci=ci, hd=hd, r0=r0, lanes=lanes, decay=decay, kbf=k.astype(BF16),
                                  qbf=q.astype(BF16), kb_bf=kb.astype(BF16),
                                  rhs=jnp.concatenate([vn_ref[r0:r0 + c, lanes] * beta, kb * eg], axis=1).astype(BF16),
                                  qd=(q * eg).astype(BF16), k_dec=(k * jnp.exp(glast - gcol)).astype(BF16),
                                  egl=jnp.exp(glast)))
        for it in items:
            it["kk"] = _dot_nt(it["kb_bf"], it["kbf"])
            it["qk"] = _dot_nt(it["qbf"], it["kbf"])
        yield
        for it in items:
            it["nmat"] = jnp.where(strict, it["kk"] * it["decay"], 0.0)
            it["inv"] = eye - jnp.where(join_mask(1), it["nmat"], 0.0)
            it["attn"] = (it["qk"] * it["decay"]).astype(BF16)
        half = 2
        while half < c:
            for it in items:
                it["inv_bf"] = it["inv"].astype(BF16)
                join = jnp.where(join_mask(half), it["nmat"], 0.0).astype(BF16)
                it["bt"] = jnp.dot(join, it["inv_bf"], preferred_element_type=F32).astype(BF16)
            yield
            for it in items:
                it["upd"] = jnp.dot(it["inv_bf"], it["bt"], preferred_element_type=F32)
            yield
            for it in items:
                it["inv"] = it["inv"] - it["upd"]
            half *= 2
        for it in items:
            uw = jnp.dot(it["inv"].astype(BF16), it["rhs"], preferred_element_type=F32)
            it["u"] = uw[:, :d]
            it["wq"] = jnp.concatenate([uw[:, d:].astype(BF16), it["qd"]], axis=0)
        yield

    def sweep_stages(items, states):
        for ci in sorted({it["ci"] for it in items}):
            group = [it for it in items if it["ci"] == ci]
            for it in group:
                it["ws_qs"] = jnp.dot(it["wq"], states[it["hd"]].astype(BF16), preferred_element_type=F32)
            yield
            for it in group:
                it["v_new"] = (it["u"] - it["ws_qs"][:c]).astype(BF16)
            for it in group:
                it["av"] = jnp.dot(it["attn"], it["v_new"], preferred_element_type=F32)
                it["kv"] = _dot_tn(it["k_dec"], it["v_new"])
            yield
            for it in group:
                states[it["hd"]] = states[it["hd"]] * it["egl"] + it["kv"]
                o = it["ws_qs"][c:] + it["av"]
                o = o * lax.rsqrt(jnp.mean(o * o, axis=-1, keepdims=True) + NORM_EPS) * nw
                o = o * _silu(zg_ref[it["r0"]:it["r0"] + c, it["lanes"]])
                o_ref[it["r0"]:it["r0"] + c, it["lanes"]] = o.astype(o_ref.dtype)

    def interleave(*gens):
        live = list(gens)
        while live:
            for gen in list(live):
                if next(gen, "end") == "end":
                    live.remove(gen)

    states = [s_ref[hd] for hd in range(nh)]
    group_chunks = 2
    groups = [list(range(g0, g0 + group_chunks)) for g0 in range(0, tile // c, group_chunks)]
    ready = []
    interleave(local_stages(groups[0], ready))
    for nxt in groups[1:]:
        upcoming = []
        interleave(local_stages(nxt, upcoming), sweep_stages(ready, states))
        ready = upcoming
    interleave(sweep_stages(ready, states))
    for hd in range(nh):
        s_ref[hd] = states[hd]


def _deltanet(z3, conv_w, alog_pad, dtb_pad, norm_w, *, tile):
    b, s, _ = z3.shape
    wdt = DN_WIDTH
    qblk = (ATTN_WIDTH + 2 * ATTN_KV_WIDTH) // wdt
    seq = lambda off: pl.BlockSpec((None, tile, wdt), lambda bi, t: (bi, t, qblk + off))
    const = lambda shape: pl.BlockSpec(shape, lambda bi, t: (0, 0))
    return pl.pallas_call(
        _dn_kernel,
        grid=(b, s // tile),
        in_specs=[
            seq(0), seq(1), seq(2), seq(3),
            pl.BlockSpec((None, tile, LANE), lambda bi, t: (bi, t, Z_BA_BLOCK)),
            const(conv_w.shape), const((1, LANE)), const((1, LANE)), const((1, HEAD_DIM)),
        ],
        out_specs=pl.BlockSpec((None, tile, wdt), lambda bi, t: (bi, t, 0)),
        out_shape=jax.ShapeDtypeStruct((b, s, wdt), BF16),
        scratch_shapes=[
            pltpu.VMEM((tile + 8, 3 * wdt), F32),
            pltpu.VMEM((tile, wdt), F32),
            pltpu.VMEM((tile, wdt), F32),
            pltpu.VMEM((tile, wdt), F32),
            pltpu.VMEM((tile, LANE), F32),
            pltpu.VMEM((8, tile), F32),
            pltpu.VMEM((tile, LANE), F32),
            pltpu.VMEM((DN_HEADS, HEAD_DIM, HEAD_DIM), F32),
        ],
        compiler_params=_params("parallel", "arbitrary"),
        name="deltanet",
    )(z3, z3, z3, z3, z3, conv_w, alog_pad, dtb_pad, norm_w)


def _s5_prep_kernel(are_ref, aim_ref, ldt_ref, bre_ref, bim_ref, cre_ref, cim_ref,
                    apr_ref, api_ref, bbr_ref, bbi_ref, cbr_ref, cbi_ref):
    lr = are_ref[...]
    li = aim_ref[...]
    dt = jnp.exp(ldt_ref[...])
    step = (lax.broadcasted_iota(jnp.int32, (S5_SEG, 1), 0) + 1).astype(F32)
    mag = jnp.exp(step * (lr * dt))
    ang = step * (li * dt)
    apr_ref[...] = mag * jnp.cos(ang)
    api_ref[...] = mag * jnp.sin(ang)
    m1 = jnp.exp(lr * dt)
    nr = m1 * jnp.cos(li * dt) - 1.0
    ni = m1 * jnp.sin(li * dt)
    den = lr * lr + li * li
    cr = (nr * lr + ni * li) / den
    ci = (ni * lr - nr * li) / den
    br = bre_ref[...]
    bi = bim_ref[...]
    mats = ((bbr_ref, cr * br - ci * bi), (bbi_ref, cr * bi + ci * br), (cbr_ref, cre_ref[...]), (cbi_ref, cim_ref[...]))
    lane_group = lax.broadcasted_iota(jnp.int32, (S5_GROUP_CH, S5_LANES), 1) // S5_STATE
    for g in range(S5_GROUPS):
        for out_ref, val in mats:
            out_ref[g * S5_GROUP_CH:(g + 1) * S5_GROUP_CH, :] = jnp.where(lane_group == g, val, 0.0).astype(BF16)


def _s5_prep(a_re, a_im, log_dt, b_re, b_im, c_re, c_im):
    depth = a_re.shape[0]
    row = lambda a: a.reshape(depth, 1, S5_LANES)
    ldt = jnp.repeat(log_dt, S5_STATE, axis=-1).reshape(depth, 1, S5_LANES)
    b_rows = lambda a: a.reshape(depth, S5_LANES, S5_GROUP_CH).transpose(0, 2, 1)
    c_rows = lambda a: a.transpose(0, 2, 1, 3).reshape(depth, S5_GROUP_CH, S5_LANES)
    shp = jax.ShapeDtypeStruct
    per_layer = lambda rows: pl.BlockSpec((None, rows, S5_LANES), lambda l: (l, 0, 0))
    chan = per_layer(S5_GROUP_CH)
    return pl.pallas_call(
        _s5_prep_kernel,
        grid=(depth,),
        in_specs=[per_layer(1), per_layer(1), per_layer(1), chan, chan, chan, chan],
        out_specs=[per_layer(S5_SEG), per_layer(S5_SEG)] + [per_layer(S5_WIDTH)] * 4,
        out_shape=(shp((depth, S5_SEG, S5_LANES), F32),) * 2 + (shp((depth, S5_WIDTH, S5_LANES), BF16),) * 4,
        name="s5_prep",
    )(row(a_re), row(a_im), ldt, b_rows(b_re), b_rows(b_im), c_rows(c_re), c_rows(c_im))


def _s5_kernel(u0_ref, u1_ref, u2_ref, u3_ref, bre_ref, bim_ref, cre_ref, cim_ref, apr_ref, api_ref, d_ref,
               gw_ref, gb_ref, o_ref, up_ref, xr_ref, xi_ref, xrb_ref, xib_ref, cr_ref, ci_ref, st_ref, y_ref):
    t = pl.program_id(1)
    tile = u0_ref.shape[0]
    seg = tile // 8
    strip = 512

    @pl.when(t == 0)
    def _():
        st_ref[...] = jnp.zeros_like(st_ref)

    for j, u_ref in enumerate((u0_ref, u1_ref, u2_ref, u3_ref)):
        for k in range(seg):
            up_ref[8 * k:8 * k + 8, j * LANE:(j + 1) * LANE] = u_ref[pl.ds(k, 8, stride=seg), :]
    hw, hl = S5_WIDTH // 2, S5_LANES // 2
    for half in range(2):
        ub = up_ref[:, half * hw:(half + 1) * hw].astype(BF16)
        rows, cols = slice(half * hw, (half + 1) * hw), slice(half * hl, (half + 1) * hl)
        xr_ref[:, cols] = jnp.dot(ub, bre_ref[rows, cols], preferred_element_type=F32)
        xi_ref[:, cols] = jnp.dot(ub, bim_ref[rows, cols], preferred_element_type=F32)

    for s0 in range(0, S5_LANES, strip):
        lanes = slice(s0, s0 + strip)
        ar = jnp.broadcast_to(apr_ref[0:1, lanes], (8, strip))
        ai = jnp.broadcast_to(api_ref[0:1, lanes], (8, strip))

        def scan(k, carry):
            pr, pi = carry
            r0 = pl.multiple_of(k * 8, 8)
            nr = ar * pr - ai * pi + xr_ref[pl.ds(r0, 8), lanes]
            ni = ar * pi + ai * pr + xi_ref[pl.ds(r0, 8), lanes]
            xr_ref[pl.ds(r0, 8), lanes] = nr
            xi_ref[pl.ds(r0, 8), lanes] = ni
            return nr, ni

        zero = jnp.zeros((8, strip), F32)
        fr, fi = lax.fori_loop(0, seg, scan, (zero, zero), unroll=4)

        a64r = apr_ref[seg - 1:seg, lanes]
        a64i = api_ref[seg - 1:seg, lanes]
        c_r = st_ref[0:1, lanes]
        c_i = st_ref[1:2, lanes]
        for r in range(8):
            cr_ref[r:r + 1, lanes] = c_r
            ci_ref[r:r + 1, lanes] = c_i
            n_r = a64r * c_r - a64i * c_i + fr[r:r + 1]
            n_i = a64r * c_i + a64i * c_r + fi[r:r + 1]
            c_r, c_i = n_r, n_i
        st_ref[0:1, lanes] = c_r
        st_ref[1:2, lanes] = c_i

        cin_r = jnp.concatenate([cr_ref[:, lanes]] * 2, axis=0)
        cin_i = jnp.concatenate([ci_ref[:, lanes]] * 2, axis=0)

        def fix(k2, carry):
            r0 = pl.multiple_of(k2 * 16, 16)
            pw = lambda ref, k: jnp.broadcast_to(ref[pl.ds(k, 1), lanes], (8, strip))
            pr = jnp.concatenate([pw(apr_ref, 2 * k2), pw(apr_ref, 2 * k2 + 1)], axis=0)
            pi = jnp.concatenate([pw(api_ref, 2 * k2), pw(api_ref, 2 * k2 + 1)], axis=0)
            xrb_ref[pl.ds(r0, 16), lanes] = (xr_ref[pl.ds(r0, 16), lanes] + pr * cin_r - pi * cin_i).astype(BF16)
            xib_ref[pl.ds(r0, 16), lanes] = (xi_ref[pl.ds(r0, 16), lanes] + pr * cin_i + pi * cin_r).astype(BF16)
            return carry

        lax.fori_loop(0, seg // 2, fix, 0, unroll=2)

    ys = []
    for half in range(2):
        states, chans = slice(half * hl, (half + 1) * hl), slice(half * hw, (half + 1) * hw)
        ys.append(_dot_nt(xrb_ref[:, states], cre_ref[chans, states])
                  - _dot_nt(xib_ref[:, states], cim_ref[chans, states]))
    y = jnp.concatenate(ys, axis=1)
    y = y + d_ref[...] * up_ref[...]
    y = 0.5 * y * (1.0 + jnp.tanh(math.sqrt(2.0 / math.pi) * (y + 0.044715 * (y * y * y))))
    gate = jnp.dot(y.astype(BF16), gw_ref[...], preferred_element_type=F32) + gb_ref[...]
    y_ref[...] = y * _sigmoid(gate)
    for j in range(S5_WIDTH // LANE):
        for k in range(seg):
            o_ref[j, pl.ds(k, 8, stride=seg), :] = y_ref[8 * k:8 * k + 8, j * LANE:(j + 1) * LANE]


def _s5(z3, b_re, b_im, c_re, c_im, ap_re, ap_im, d_skip, glu_w, glu_b, layer, *, tile):
    b, s, _ = z3.shape
    nblk = S5_WIDTH // LANE
    ublk = (ATTN_WIDTH + 2 * ATTN_KV_WIDTH + 4 * DN_WIDTH) // LANE
    const = lambda a: pl.BlockSpec(a.shape, lambda bi, t: (0, 0))
    layered = lambda a: pl.BlockSpec((None,) + a.shape[1:], lambda bi, t: (layer, 0, 0))
    ucol = lambda j: pl.BlockSpec((None, tile, LANE), lambda bi, t: (bi, t, ublk + j))
    return pl.pallas_call(
        _s5_kernel,
        grid=(b, s // tile),
        in_specs=[
            ucol(0), ucol(1), ucol(2), ucol(3),
            layered(b_re), layered(b_im), layered(c_re), layered(c_im), layered(ap_re), layered(ap_im),
            const(d_skip), layered(glu_w), const(glu_b),
        ],
        out_specs=pl.BlockSpec((nblk, None, tile, LANE), lambda bi, t: (0, bi, t, 0)),
        out_shape=jax.ShapeDtypeStruct((nblk, b, s, LANE), F32),
        scratch_shapes=[
            pltpu.VMEM((tile, S5_WIDTH), F32),
            pltpu.VMEM((tile, S5_LANES), F32),
            pltpu.VMEM((tile, S5_LANES), F32),
            pltpu.VMEM((tile, S5_LANES), BF16),
            pltpu.VMEM((tile, S5_LANES), BF16),
            pltpu.VMEM((8, S5_LANES), F32),
            pltpu.VMEM((8, S5_LANES), F32),
            pltpu.VMEM((8, S5_LANES), F32),
            pltpu.VMEM((tile, S5_WIDTH), F32),
        ],
        compiler_params=_params("parallel", "arbitrary"),
        name="s5",
    )(z3, z3, z3, z3, b_re, b_im, c_re, c_im, ap_re, ap_im, d_skip, glu_w, glu_b)


def _cast_kernel(w_ref, o_ref):
    o_ref[...] = w_ref[...].astype(o_ref.dtype)


def _cast_leading_cols(w, cols, *, rows):
    depth, d, _ = w.shape
    block = pl.BlockSpec((None, rows, cols), lambda l, r: (l, r, 0))
    return pl.pallas_call(
        _cast_kernel,
        grid=(depth, d // rows),
        in_specs=[block],
        out_specs=block,
        out_shape=jax.ShapeDtypeStruct((depth, d, cols), BF16),
        compiler_params=_params("parallel", "parallel"),
        name="cast_w_in",
    )(w)


def _rope_tables(seq):
    half = HEAD_DIM // 2
    inv_freq = ROPE_THETA ** (-jnp.arange(half, dtype=F32) / half)
    ang = jnp.arange(seq, dtype=F32)[:, None] * inv_freq[None, :]
    cos, sin = jnp.cos(ang), jnp.sin(ang)
    return jnp.concatenate([cos, cos], axis=-1), jnp.concatenate([-sin, sin], axis=-1)


def _split_w_in(w_in):
    s5_src = SRC_BA + 2 * DN_HEADS
    pad = jnp.zeros(w_in.shape[:-1] + (LANE - 2 * DN_HEADS,), BF16)
    w_ba = jnp.concatenate([w_in[..., SRC_BA:s5_src].astype(BF16), pad], axis=-1)
    w_main = _cast_leading_cols(w_in, SRC_BA, rows=min(512, w_in.shape[1]))
    return w_main, w_in[..., s5_src:].astype(BF16), w_ba


def _lane_pad(v, offset):
    return jnp.zeros((v.shape[0], 1, LANE), F32).at[:, 0, offset:offset + v.shape[1]].set(v)


def kernel(x, ff1_norm_pre, ff1_w_gate, ff1_w_up, ff1_w_down, ff1_norm_post, mix_norm_pre, w_in,
           attn_sinks, dn_conv_w, dn_a_log, dn_dt_bias, dn_norm_w, s5_a_re, s5_a_im, s5_log_dt,
           s5_b_re, s5_b_im, s5_c_re, s5_c_im, s5_d, s5_glu_w, s5_glu_b, w_out, mix_norm_post,
           ff2_norm_pre, ff2_w_gate, ff2_w_up, ff2_w_down, ff2_norm_post):
    b, s, d = x.shape
    depth = w_in.shape[0]
    n = b * s
    tm = min(512, n)
    tm_big = min(1024, n)
    tf = 512
    tile = min(SEQ_TILE, s)

    bf = lambda a: a.astype(BF16)
    ff1 = (ff1_w_gate, ff1_w_up, ff1_w_down)
    ff2 = (ff2_w_gate, ff2_w_up, ff2_w_down)
    w_in_parts = _split_w_in(w_in)
    w_out_b = bf(w_out)
    glu_w_b = bf(s5_glu_w)
    ap_re, ap_im, *s5_mats = _s5_prep(s5_a_re, s5_a_im, s5_log_dt, s5_b_re, s5_b_im, s5_c_re, s5_c_im)
    cc, ss = _rope_tables(s)
    alog_pad = _lane_pad(dn_a_log, DN_HEADS)
    dtb_pad = _lane_pad(dn_dt_bias, DN_HEADS)
    row = lambda a, l: a[l].reshape(1, -1)

    xf = x.reshape(n, d)
    for l in range(depth):
        xf = _ffn(xf, row(ff1_norm_pre, l), *ff1, row(ff1_norm_post, l), l, tm=tm_big, tf=tf, tf_head=tf // 2)

        z3 = _inproj(xf, row(mix_norm_pre, l), *w_in_parts, l, tm=tm).reshape(b, s, Z_WIDTH)
        y_attn = _attention(z3, attn_sinks[l], cc, ss, tile=tile)
        y_dn = _deltanet(z3, dn_conv_w[l], alog_pad[l], dtb_pad[l], row(dn_norm_w, l), tile=tile)
        y_s5 = _s5(z3, *s5_mats, ap_re, ap_im, row(s5_d, l), glu_w_b, row(s5_glu_b, l), l, tile=tile)
        xf = _outproj(xf, y_attn.reshape(n, -1), y_dn.reshape(n, -1), y_s5.reshape(-1, n, LANE),
                      w_out_b, row(mix_norm_post, l), l, tm=tm)

        xf = _ffn(xf, row(ff2_norm_pre, l), *ff2, row(ff2_norm_post, l), l, tm=tm_big, tf=tf, tf_head=tf // 2)
    return xf.reshape(b, s, d)
```

```python
import functools
import math

import jax
import jax.numpy as jnp
from jax import lax
from jax.experimental import pallas as pl
from jax.experimental.pallas import tpu as pltpu

F32 = jnp.float32
BF16 = jnp.bfloat16
HIGHEST = lax.Precision.HIGHEST

NORM_EPS = 1e-6
FFN_RES_WEIGHT = 0.5
ROPE_THETA = 10000.0

HEAD_DIM = 128
WINDOW = 128
ATTN_HEADS = 8
ATTN_KV_HEADS = 2
ATTN_GROUP = ATTN_HEADS // ATTN_KV_HEADS
DN_HEADS = 4
DN_CONV = 4
DN_CHUNK = 64
S5_GROUPS = 32
S5_GROUP_CH = 16
S5_STATE = 64
ATTN_WIDTH = ATTN_HEADS * HEAD_DIM
ATTN_KV_WIDTH = ATTN_KV_HEADS * HEAD_DIM
DN_WIDTH = DN_HEADS * HEAD_DIM
S5_WIDTH = S5_GROUPS * S5_GROUP_CH
S5_LANES = S5_GROUPS * S5_STATE
MIX_WIDTH = ATTN_WIDTH + DN_WIDTH + S5_WIDTH

LANE = 128
Z_WIDTH = ATTN_WIDTH + 2 * ATTN_KV_WIDTH + 4 * DN_WIDTH + S5_WIDTH + LANE
Z_BA_BLOCK = (Z_WIDTH - LANE) // LANE
SRC_BA = ATTN_WIDTH + 2 * ATTN_KV_WIDTH + 4 * DN_WIDTH

SEQ_TILE = 512
S5_SEG = SEQ_TILE // 8
FFN_DOWN_COLS = 512
FFN_SUB_COLS = 256
NORM_ROWS = 128
VMEM_LIMIT = 56 * 1024 * 1024


def _params(*sem):
    return pltpu.CompilerParams(dimension_semantics=sem, vmem_limit_bytes=VMEM_LIMIT)


def _rms(x, gain):
    return x * lax.rsqrt(jnp.mean(x * x, axis=-1, keepdims=True) + NORM_EPS) * gain


def _sigmoid(x):
    return 1.0 / (1.0 + jnp.exp(-x))


def _ffn_kernel(*refs, cast_weights):
    if cast_weights:
        x_ref, gpre_ref, wg_ref, wu_ref, wd_ref, gpost_ref, o_ref, wgb_ref, wub_ref, wdb_ref, h_ref = refs
        wgb_ref[...] = wg_ref[...].astype(BF16)
        wub_ref[...] = wu_ref[...].astype(BF16)
        wdb_ref[...] = wd_ref[...].astype(BF16)
        wg_ref, wu_ref, wd_ref = wgb_ref, wub_ref, wdb_ref
    else:
        x_ref, gpre_ref, wg_ref, wu_ref, wd_ref, gpost_ref, o_ref, h_ref = refs
    j = pl.program_id(1)

    row_blocks = [slice(r0, r0 + NORM_ROWS) for r0 in range(0, x_ref.shape[0], NORM_ROWS)]

    @pl.when(j == 0)
    def _():
        for rows in row_blocks:
            h_ref[rows, :] = _rms(x_ref[rows, :], gpre_ref[...]).astype(BF16)
        o_ref[...] = jnp.zeros_like(o_ref)

    h = h_ref[...]
    tf = wg_ref.shape[1]
    sub = min(FFN_SUB_COLS, tf)
    chunk = min(FFN_DOWN_COLS, o_ref.shape[1])
    for f0 in range(0, tf, sub):
        fs = slice(f0, f0 + sub)
        g = jnp.dot(h, wg_ref[:, fs], preferred_element_type=F32)
        u = jnp.dot(h, wu_ref[:, fs], preferred_element_type=F32)
        a = (g * _sigmoid(g) * u).astype(BF16)
        for c0 in range(0, o_ref.shape[1], chunk):
            cols = slice(c0, c0 + chunk)
            o_ref[:, cols] += jnp.dot(a, wd_ref[fs, cols], preferred_element_type=F32)

    @pl.when(j == pl.num_programs(1) - 1)
    def _():
        for rows in row_blocks:
            o_ref[rows, :] = x_ref[rows, :] + FFN_RES_WEIGHT * _rms(o_ref[rows, :], gpost_ref[...])


def _ffn(x, gpre, wg, wu, wd, gpost, layer, *, tm, tf, tf_head):
    n, d = x.shape
    f = wg.shape[2]
    once = pl.Buffered(1)
    vec = pl.BlockSpec((1, d), lambda i, j: (0, 0))
    shp = jax.ShapeDtypeStruct
    y, wgb, wub, wdb = pl.pallas_call(
        functools.partial(_ffn_kernel, cast_weights=True),
        grid=(1, f // tf_head),
        in_specs=[
            pl.BlockSpec((tm, d), lambda i, j: (0, 0), pipeline_mode=once),
            vec,
            pl.BlockSpec((None, d, tf_head), lambda i, j: (layer, 0, j)),
            pl.BlockSpec((None, d, tf_head), lambda i, j: (layer, 0, j)),
            pl.BlockSpec((None, tf_head, d), lambda i, j: (layer, j, 0)),
            vec,
        ],
        out_specs=[
            pl.BlockSpec((tm, d), lambda i, j: (0, 0), pipeline_mode=once),
            pl.BlockSpec((d, tf_head), lambda i, j: (0, j)),
            pl.BlockSpec((d, tf_head), lambda i, j: (0, j)),
            pl.BlockSpec((tf_head, d), lambda i, j: (j, 0)),
        ],
        out_shape=(shp((n, d), F32), shp((d, f), BF16), shp((d, f), BF16), shp((f, d), BF16)),
        input_output_aliases={0: 0},
        scratch_shapes=[pltpu.VMEM((tm, d), BF16)],
        compiler_params=_params("arbitrary", "arbitrary"),
        name="ffn_head",
    )(x, gpre, wg, wu, wd, gpost)
    return pl.pallas_call(
        functools.partial(_ffn_kernel, cast_weights=False),
        grid=(n // tm - 1, f // tf),
        in_specs=[
            pl.BlockSpec((tm, d), lambda i, j: (i + 1, 0)),
            vec,
            pl.BlockSpec((d, tf), lambda i, j: (0, j)),
            pl.BlockSpec((d, tf), lambda i, j: (0, j)),
            pl.BlockSpec((tf, d), lambda i, j: (j, 0)),
            vec,
        ],
        out_specs=pl.BlockSpec((tm, d), lambda i, j: (i + 1, 0)),
        out_shape=shp((n, d), F32),
        input_output_aliases={0: 0},
        scratch_shapes=[pltpu.VMEM((tm, d), BF16)],
        compiler_params=_params("parallel", "arbitrary"),
        name="ffn",
    )(y, gpre, wgb, wub, wdb, gpost)


def _inproj_kernel(x_ref, g_ref, wm_ref, ws_ref, wb_ref, o_ref):
    h = _rms(x_ref[...], g_ref[...]).astype(BF16)
    s5_at = wm_ref.shape[1]
    ba_at = s5_at + ws_ref.shape[1]
    o_ref[:, :s5_at] = jnp.dot(h, wm_ref[...], preferred_element_type=F32)
    o_ref[:, s5_at:ba_at] = jnp.dot(h, ws_ref[...], preferred_element_type=F32)
    o_ref[:, ba_at:] = jnp.dot(h, wb_ref[...], preferred_element_type=F32)


def _inproj(x, gain, w_all, w_s5, w_ba, layer, *, tm):
    n, d = x.shape
    resident = lambda shape: pl.BlockSpec((None,) + shape, lambda i: (layer, 0, 0), pipeline_mode=pl.Buffered(1))
    return pl.pallas_call(
        _inproj_kernel,
        grid=(n // tm,),
        in_specs=[
            pl.BlockSpec((tm, d), lambda i: (i, 0)),
            pl.BlockSpec((1, d), lambda i: (0, 0)),
            resident((d, SRC_BA)), resident((d, S5_WIDTH)), resident((d, LANE)),
        ],
        out_specs=pl.BlockSpec((tm, Z_WIDTH), lambda i: (i, 0)),
        out_shape=jax.ShapeDtypeStruct((n, Z_WIDTH), F32),
        compiler_params=_params("parallel"),
        name="inproj",
    )(x, gain, w_all, w_s5, w_ba)


def _outproj_kernel(x_ref, ya_ref, yd_ref, ys_ref, w_ref, g_ref, o_ref):
    ys = [ys_ref[j].astype(BF16) for j in range(ys_ref.shape[0])]
    y = jnp.concatenate([ya_ref[...], yd_ref[...]] + ys, axis=-1)
    mixed = jnp.dot(y, w_ref[...], preferred_element_type=F32)
    o_ref[...] = x_ref[...] + _rms(mixed, g_ref[...])


def _outproj(x, ya, yd, ys, w, gain, layer, *, tm):
    n, d = x.shape
    return pl.pallas_call(
        _outproj_kernel,
        grid=(n // tm,),
        in_specs=[
            pl.BlockSpec((tm, d), lambda i: (i, 0)),
            pl.BlockSpec((tm, ya.shape[1]), lambda i: (i, 0)),
            pl.BlockSpec((tm, yd.shape[1]), lambda i: (i, 0)),
            pl.BlockSpec((ys.shape[0], tm, ys.shape[2]), lambda i: (0, i, 0)),
            pl.BlockSpec((None,) + w.shape[1:], lambda i: (layer, 0, 0)),
            pl.BlockSpec((1, d), lambda i: (0, 0)),
        ],
        out_specs=pl.BlockSpec((tm, d), lambda i: (i, 0)),
        out_shape=jax.ShapeDtypeStruct((n, d), F32),
        compiler_params=_params("parallel"),
        name="outproj",
    )(x, ya, yd, ys, w, gain)


def _rope(x, cc, ss):
    return x * cc + pltpu.roll(x, HEAD_DIM // 2, axis=1) * ss


def _attn_kernel(sink_ref, q_ref, kp_ref, kc_ref, vp_ref, vc_ref, ccp_ref, ccc_ref, ssp_ref, ssc_ref, o_ref):
    h = pl.program_id(1)
    t = pl.program_id(2)
    w = WINDOW
    grp = ATTN_GROUP
    nblk = q_ref.shape[0] // w
    ccc = ccc_ref[...]
    ssc = ssc_ref[...]
    q = q_ref[...]
    scale = HEAD_DIM ** -0.5
    qr = [(_rope(q[:, g * HEAD_DIM:(g + 1) * HEAD_DIM], ccc, ssc) * scale).astype(BF16) for g in range(grp)]
    kk = jnp.concatenate([_rope(kp_ref[...], ccp_ref[...], ssp_ref[...]), _rope(kc_ref[...], ccc, ssc)],
                         axis=0).astype(BF16)
    vv = jnp.concatenate([vp_ref[...], vc_ref[...]], axis=0).astype(BF16)
    qi = lax.broadcasted_iota(jnp.int32, (w, 2 * w), 0) + w
    kj = lax.broadcasted_iota(jnp.int32, (w, 2 * w), 1)
    rel = qi - kj
    band = (rel >= 0) & (rel < w)
    first = band & ((kj >= w) | (t > 0))
    row_head = lax.broadcasted_iota(jnp.int32, (grp * w, 1), 0) // w
    sink = jnp.zeros((grp * w, 1), F32)
    for g in range(grp):
        sink = jnp.where(row_head == g, sink_ref[h * grp + g], sink)
    scores = []
    for blk in range(nblk):
        qs = jnp.concatenate([qr[g][blk * w:(blk + 1) * w] for g in range(grp)], axis=0)
        scores.append(_dot_nt(qs, kk[blk * w:(blk + 2) * w]))
    probs = []
    for blk in range(nblk):
        mask = jnp.concatenate([first if blk == 0 else band] * grp, axis=0)
        sc = jnp.where(mask, scores[blk], -jnp.inf)
        m = jnp.maximum(jnp.max(sc, axis=-1, keepdims=True), sink)
        p = jnp.exp(sc - m)
        denom = jnp.sum(p, axis=-1, keepdims=True) + jnp.exp(sink - m)
        probs.append((p / denom).astype(BF16))
    outs = [jnp.dot(probs[blk], vv[blk * w:(blk + 2) * w], preferred_element_type=F32) for blk in range(nblk)]
    for blk in range(nblk):
        o_ref[blk * w:(blk + 1) * w, :] = jnp.concatenate(
            [outs[blk][g * w:(g + 1) * w] for g in range(grp)], axis=1).astype(o_ref.dtype)


def _attention(z3, sinks, cc, ss, *, tile):
    b, s, _ = z3.shape
    w = WINDOW
    nblk = tile // w
    qw = ATTN_GROUP * HEAD_DIM
    kblk = ATTN_WIDTH // HEAD_DIM
    vblk = (ATTN_WIDTH + ATTN_KV_WIDTH) // HEAD_DIM
    prev = lambda t: jnp.maximum(t * nblk - 1, 0)
    return pl.pallas_call(
        _attn_kernel,
        grid=(b, ATTN_KV_HEADS, s // tile),
        in_specs=[
            pl.BlockSpec(memory_space=pltpu.SMEM),
            pl.BlockSpec((None, tile, qw), lambda bi, h, t: (bi, t, h)),
            pl.BlockSpec((None, w, HEAD_DIM), lambda bi, h, t: (bi, prev(t), kblk + h)),
            pl.BlockSpec((None, tile, HEAD_DIM), lambda bi, h, t: (bi, t, kblk + h)),
            pl.BlockSpec((None, w, HEAD_DIM), lambda bi, h, t: (bi, prev(t), vblk + h)),
            pl.BlockSpec((None, tile, HEAD_DIM), lambda bi, h, t: (bi, t, vblk + h)),
            pl.BlockSpec((w, HEAD_DIM), lambda bi, h, t: (prev(t), 0)),
            pl.BlockSpec((tile, HEAD_DIM), lambda bi, h, t: (t, 0)),
            pl.BlockSpec((w, HEAD_DIM), lambda bi, h, t: (prev(t), 0)),
            pl.BlockSpec((tile, HEAD_DIM), lambda bi, h, t: (t, 0)),
        ],
        out_specs=pl.BlockSpec((None, tile, qw), lambda bi, h, t: (bi, t, h)),
        out_shape=jax.ShapeDtypeStruct((b, s, ATTN_WIDTH), BF16),
        compiler_params=_params("parallel", "parallel", "arbitrary"),
        name="swa",
    )(sinks, z3, z3, z3, z3, z3, cc, cc, ss, ss)


def _silu(x):
    return x * _sigmoid(x)


def _softplus(x):
    return jnp.maximum(x, 0.0) + jnp.log(1.0 + jnp.exp(-jnp.abs(x)))


def _dot_nt(a, b, precision=None):
    return lax.dot_general(a, b, (((1,), (1,)), ((), ())), preferred_element_type=F32, precision=precision)


def _dot_tn(a, b, precision=None):
    return lax.dot_general(a, b, (((0,), (0,)), ((), ())), preferred_element_type=F32, precision=precision)


def _dn_kernel(q_ref, k_ref, v_ref, zg_ref, ba_ref, cw_ref, alog_ref, dtb_ref, nw_ref, o_ref,
               ext_ref, qn_ref, kn_ref, vn_ref, gc_ref, gct_ref, beta_ref, s_ref):
    t = pl.program_id(1)
    tile = q_ref.shape[0]
    c = DN_CHUNK
    d = HEAD_DIM
    nh = DN_HEADS
    wdt = nh * d

    @pl.when(t == 0)
    def _():
        ext_ref[0:8, :] = jnp.zeros((8, 3 * wdt), F32)
        s_ref[...] = jnp.zeros_like(s_ref)

    ext_ref[8:8 + tile, 0:wdt] = q_ref[...]
    ext_ref[8:8 + tile, wdt:2 * wdt] = k_ref[...]
    ext_ref[8:8 + tile, 2 * wdt:3 * wdt] = v_ref[...]
    ext = ext_ref[...]
    conv = ext[8:] * cw_ref[DN_CONV - 1:DN_CONV, :]
    for back in range(1, DN_CONV):
        conv = conv + pltpu.roll(ext, back, axis=0)[8:] * cw_ref[DN_CONV - 1 - back:DN_CONV - back, :]
    ext_ref[0:8, :] = ext[tile:tile + 8]
    qkv = _silu(conv)
    for hd in range(nh):
        qh = qkv[:, hd * d:(hd + 1) * d]
        kh = qkv[:, wdt + hd * d:wdt + (hd + 1) * d]
        qn_ref[:, hd * d:(hd + 1) * d] = qh * lax.rsqrt(jnp.sum(qh * qh, axis=-1, keepdims=True) + NORM_EPS) * (d ** -0.5)
        kn_ref[:, hd * d:(hd + 1) * d] = kh * lax.rsqrt(jnp.sum(kh * kh, axis=-1, keepdims=True) + NORM_EPS)
    vn_ref[...] = qkv[:, 2 * wdt:3 * wdt]
    ba = ba_ref[...]
    beta_ref[...] = _sigmoid(ba)
    g = -jnp.exp(alog_ref[...]) * _softplus(ba + dtb_ref[...])
    ti = lax.broadcasted_iota(jnp.int32, (c, c), 0)
    tj = lax.broadcasted_iota(jnp.int32, (c, c), 1)
    csum = (ti >= tj).astype(F32)
    gc_all = jnp.concatenate([jnp.dot(csum, g[ci * c:(ci + 1) * c], preferred_element_type=F32, precision=HIGHEST)
                              for ci in range(tile // c)], axis=0)
    gc_ref[...] = gc_all
    gct_ref[...] = gc_all.T[0:8, :]

    row = lax.broadcasted_iota(jnp.int32, (c, c), 0)
    col = lax.broadcasted_iota(jnp.int32, (c, c), 1)
    causal = row >= col
    strict = row > col
    eye = (row == col).astype(F32)
    nw = nw_ref[...]

    def join_mask(half):
        return (row // (2 * half) == col // (2 * half)) & (row % (2 * half) >= half) & (col % (2 * half) < half)

    def local_stages(chunks, items):
        for ci in chunks:
            r0 = ci * c
            for hd in range(nh):
                lanes = slice(hd * d, (hd + 1) * d)
                q = qn_ref[r0:r0 + c, lanes]
                k = kn_ref[r0:r0 + c, lanes]
                beta = beta_ref[r0:r0 + c, hd:hd + 1]
                gcol = gc_ref[r0:r0 + c, nh + hd:nh + hd + 1]
                grow = gct_ref[nh + hd:nh + hd + 1, r0:r0 + c]
                glast = gcol[c - 1:c, :]
                decay = jnp.exp(jnp.where(causal, gcol - grow, -jnp.inf))
                eg = jnp.exp(gcol)
                kb = k * beta
                items.append(dict(ci=ci, hd=hd, r0=r0, lanes=lanes, decay=decay, kbf=k.astype(BF16),
                                  qbf=q.astype(BF16), kb_bf=kb.astype(BF16),
                                  rhs=jnp.concatenate([vn_ref[r0:r0 + c, lanes] * beta, kb * eg], axis=1).astype(BF16),
                                  qd=(q * eg).astype(BF16), k_dec=(k * jnp.exp(glast - gcol)).astype(BF16),
                                  egl=jnp.exp(glast)))
        for it in items:
            it["kk"] = _dot_nt(it["kb_bf"], it["kbf"])
            it["qk"] = _dot_nt(it["qbf"], it["kbf"])
        yield
        for it in items:
            it["nmat"] = jnp.where(strict, it["kk"] * it["decay"], 0.0)
            it["inv"] = eye - jnp.where(join_mask(1), it["nmat"], 0.0)
            it["attn"] = (it["qk"] * it["decay"]).astype(BF16)
        half = 2
        while half < c:
            for it in items:
                it["inv_bf"] = it["inv"].astype(BF16)
                join = jnp.where(join_mask(half), it["nmat"], 0.0).astype(BF16)
                it["bt"] = jnp.dot(join, it["inv_bf"], preferred_element_type=F32).astype(BF16)
            yield
            for it in items:
                it["upd"] = jnp.dot(it["inv_bf"], it["bt"], preferred_element_type=F32)
            yield
            for it in items:
                it["inv"] = it["inv"] - it["upd"]
            half *= 2
        for it in items:
            uw = jnp.dot(it["inv"].astype(BF16), it["rhs"], preferred_element_type=F32)
            it["u"] = uw[:, :d]
            it["wq"] = jnp.concatenate([uw[:, d:].astype(BF16), it["qd"]], axis=0)
        yield

    def sweep_stages(items, states):
        for ci in sorted({it["ci"] for it in items}):
            group = [it for it in items if it["ci"] == ci]
            for it in group:
                it["ws_qs"] = jnp.dot(it["wq"], states[it["hd"]].astype(BF16), preferred_element_type=F32)
            yield
            for it in group:
                it["v_new"] = (it["u"] - it["ws_qs"][:c]).astype(BF16)
            for it in group:
                it["av"] = jnp.dot(it["attn"], it["v_new"], preferred_element_type=F32)
                it["kv"] = _dot_tn(it["k_dec"], it["v_new"])
            yield
            for it in group:
                states[it["hd"]] = states[it["hd"]] * it["egl"] + it["kv"]
                o = it["ws_qs"][c:] + it["av"]
                o = o * lax.rsqrt(jnp.mean(o * o, axis=-1, keepdims=True) + NORM_EPS) * nw
                o = o * _silu(zg_ref[it["r0"]:it["r0"] + c, it["lanes"]])
                o_ref[it["r0"]:it["r0"] + c, it["lanes"]] = o.astype(o_ref.dtype)

    def interleave(*gens):
        live = list(gens)
        while live:
            for gen in list(live):
                if next(gen, "end") == "end":
                    live.remove(gen)

    states = [s_ref[hd] for hd in range(nh)]
    group_chunks = 2
    groups = [list(range(g0, g0 + group_chunks)) for g0 in range(0, tile // c, group_chunks)]
    ready = []
    interleave(local_stages(groups[0], ready))
    for nxt in groups[1:]:
        upcoming = []
        interleave(local_stages(nxt, upcoming), sweep_stages(ready, states))
        ready = upcoming
    interleave(sweep_stages(ready, states))
    for hd in range(nh):
        s_ref[hd] = states[hd]


def _deltanet(z3, conv_w, alog_pad, dtb_pad, norm_w, *, tile):
    b, s, _ = z3.shape
    wdt = DN_WIDTH
    qblk = (ATTN_WIDTH + 2 * ATTN_KV_WIDTH) // wdt
    seq = lambda off: pl.BlockSpec((None, tile, wdt), lambda bi, t: (bi, t, qblk + off))
    const = lambda shape: pl.BlockSpec(shape, lambda bi, t: (0, 0))
    return pl.pallas_call(
        _dn_kernel,
        grid=(b, s // tile),
        in_specs=[
            seq(0), seq(1), seq(2), seq(3),
            pl.BlockSpec((None, tile, LANE), lambda bi, t: (bi, t, Z_BA_BLOCK)),
            const(conv_w.shape), const((1, LANE)), const((1, LANE)), const((1, HEAD_DIM)),
        ],
        out_specs=pl.BlockSpec((None, tile, wdt), lambda bi, t: (bi, t, 0)),
        out_shape=jax.ShapeDtypeStruct((b, s, wdt), BF16),
        scratch_shapes=[
            pltpu.VMEM((tile + 8, 3 * wdt), F32),
            pltpu.VMEM((tile, wdt), F32),
            pltpu.VMEM((tile, wdt), F32),
            pltpu.VMEM((tile, wdt), F32),
            pltpu.VMEM((tile, LANE), F32),
            pltpu.VMEM((8, tile), F32),
            pltpu.VMEM((tile, LANE), F32),
            pltpu.VMEM((DN_HEADS, HEAD_DIM, HEAD_DIM), F32),
        ],
        compiler_params=_params("parallel", "arbitrary"),
        name="deltanet",
    )(z3, z3, z3, z3, z3, conv_w, alog_pad, dtb_pad, norm_w)


def _s5_prep_kernel(are_ref, aim_ref, ldt_ref, bre_ref, bim_ref, cre_ref, cim_ref,
                    apr_ref, api_ref, bbr_ref, bbi_ref, cbr_ref, cbi_ref):
    lr = are_ref[...]
    li = aim_ref[...]
    dt = jnp.exp(ldt_ref[...])
    step = (lax.broadcasted_iota(jnp.int32, (S5_SEG, 1), 0) + 1).astype(F32)
    mag = jnp.exp(step * (lr * dt))
    ang = step * (li * dt)
    apr_ref[...] = mag * jnp.cos(ang)
    api_ref[...] = mag * jnp.sin(ang)
    m1 = jnp.exp(lr * dt)
    nr = m1 * jnp.cos(li * dt) - 1.0
    ni = m1 * jnp.sin(li * dt)
    den = lr * lr + li * li
    cr = (nr * lr + ni * li) / den
    ci = (ni * lr - nr * li) / den
    br = bre_ref[...]
    bi = bim_ref[...]
    mats = ((bbr_ref, cr * br - ci * bi), (bbi_ref, cr * bi + ci * br), (cbr_ref, cre_ref[...]), (cbi_ref, cim_ref[...]))
    lane_group = lax.broadcasted_iota(jnp.int32, (S5_GROUP_CH, S5_LANES), 1) // S5_STATE
    for g in range(S5_GROUPS):
        for out_ref, val in mats:
            out_ref[g * S5_GROUP_CH:(g + 1) * S5_GROUP_CH, :] = jnp.where(lane_group == g, val, 0.0).astype(BF16)


def _s5_prep(a_re, a_im, log_dt, b_re, b_im, c_re, c_im):
    depth = a_re.shape[0]
    row = lambda a: a.reshape(depth, 1, S5_LANES)
    ldt = jnp.repeat(log_dt, S5_STATE, axis=-1).reshape(depth, 1, S5_LANES)
    b_rows = lambda a: a.reshape(depth, S5_LANES, S5_GROUP_CH).transpose(0, 2, 1)
    c_rows = lambda a: a.transpose(0, 2, 1, 3).reshape(depth, S5_GROUP_CH, S5_LANES)
    shp = jax.ShapeDtypeStruct
    per_layer = lambda rows: pl.BlockSpec((None, rows, S5_LANES), lambda l: (l, 0, 0))
    chan = per_layer(S5_GROUP_CH)
    return pl.pallas_call(
        _s5_prep_kernel,
        grid=(depth,),
        in_specs=[per_layer(1), per_layer(1), per_layer(1), chan, chan, chan, chan],
        out_specs=[per_layer(S5_SEG), per_layer(S5_SEG)] + [per_layer(S5_WIDTH)] * 4,
        out_shape=(shp((depth, S5_SEG, S5_LANES), F32),) * 2 + (shp((depth, S5_WIDTH, S5_LANES), BF16),) * 4,
        name="s5_prep",
    )(row(a_re), row(a_im), ldt, b_rows(b_re), b_rows(b_im), c_rows(c_re), c_rows(c_im))


def _s5_kernel(u0_ref, u1_ref, u2_ref, u3_ref, bre_ref, bim_ref, cre_ref, cim_ref, apr_ref, api_ref, d_ref,
               gw_ref, gb_ref, o_ref, up_ref, xr_ref, xi_ref, xrb_ref, xib_ref, cr_ref, ci_ref, st_ref, y_ref):
    t = pl.program_id(1)
    tile = u0_ref.shape[0]
    seg = tile // 8
    strip = 512

    @pl.when(t == 0)
    def _():
        st_ref[...] = jnp.zeros_like(st_ref)

    for j, u_ref in enumerate((u0_ref, u1_ref, u2_ref, u3_ref)):
        for k in range(seg):
            up_ref[8 * k:8 * k + 8, j * LANE:(j + 1) * LANE] = u_ref[pl.ds(k, 8, stride=seg), :]
    hw, hl = S5_WIDTH // 2, S5_LANES // 2
    for half in range(2):
        ub = up_ref[:, half * hw:(half + 1) * hw].astype(BF16)
        rows, cols = slice(half * hw, (half + 1) * hw), slice(half * hl, (half + 1) * hl)
        xr_ref[:, cols] = jnp.dot(ub, bre_ref[rows, cols], preferred_element_type=F32)
        xi_ref[:, cols] = jnp.dot(ub, bim_ref[rows, cols], preferred_element_type=F32)

    for s0 in range(0, S5_LANES, strip):
        lanes = slice(s0, s0 + strip)
        ar = jnp.broadcast_to(apr_ref[0:1, lanes], (8, strip))
        ai = jnp.broadcast_to(api_ref[0:1, lanes], (8, strip))

        def scan(k, carry):
            pr, pi = carry
            r0 = pl.multiple_of(k * 8, 8)
            nr = ar * pr - ai * pi + xr_ref[pl.ds(r0, 8), lanes]
            ni = ar * pi + ai * pr + xi_ref[pl.ds(r0, 8), lanes]
            xr_ref[pl.ds(r0, 8), lanes] = nr
            xi_ref[pl.ds(r0, 8), lanes] = ni
            return nr, ni

        zero = jnp.zeros((8, strip), F32)
        fr, fi = lax.fori_loop(0, seg, scan, (zero, zero), unroll=4)

        a64r = apr_ref[seg - 1:seg, lanes]
        a64i = api_ref[seg - 1:seg, lanes]
        c_r = st_ref[0:1, lanes]
        c_i = st_ref[1:2, lanes]
        for r in range(8):
            cr_ref[r:r + 1, lanes] = c_r
            ci_ref[r:r + 1, lanes] = c_i
            n_r = a64r * c_r - a64i * c_i + fr[r:r + 1]
            n_i = a64r * c_i + a64i * c_r + fi[r:r + 1]
            c_r, c_i = n_r, n_i
        st_ref[0:1, lanes] = c_r
        st_ref[1:2, lanes] = c_i

        cin_r = jnp.concatenate([cr_ref[:, lanes]] * 2, axis=0)
        cin_i = jnp.concatenate([ci_ref[:, lanes]] * 2, axis=0)

        def fix(k2, carry):
            r0 = pl.multiple_of(k2 * 16, 16)
            pw = lambda ref, k: jnp.broadcast_to(ref[pl.ds(k, 1), lanes], (8, strip))
            pr = jnp.concatenate([pw(apr_ref, 2 * k2), pw(apr_ref, 2 * k2 + 1)], axis=0)
            pi = jnp.concatenate([pw(api_ref, 2 * k2), pw(api_ref, 2 * k2 + 1)], axis=0)
            xrb_ref[pl.ds(r0, 16), lanes] = (xr_ref[pl.ds(r0, 16), lanes] + pr * cin_r - pi * cin_i).astype(BF16)
            xib_ref[pl.ds(r0, 16), lanes] = (xi_ref[pl.ds(r0, 16), lanes] + pr * cin_i + pi * cin_r).astype(BF16)
            return carry

        lax.fori_loop(0, seg // 2, fix, 0, unroll=2)

    ys = []
    for half in range(2):
        states, chans = slice(half * hl, (half + 1) * hl), slice(half * hw, (half + 1) * hw)
        ys.append(_dot_nt(xrb_ref[:, states], cre_ref[chans, states])
                  - _dot_nt(xib_ref[:, states], cim_ref[chans, states]))
    y = jnp.concatenate(ys, axis=1)
    y = y + d_ref[...] * up_ref[...]
    y = 0.5 * y * (1.0 + jnp.tanh(math.sqrt(2.0 / math.pi) * (y + 0.044715 * (y * y * y))))
    gate = jnp.dot(y.astype(BF16), gw_ref[...], preferred_element_type=F32) + gb_ref[...]
    y_ref[...] = y * _sigmoid(gate)
    for j in range(S5_WIDTH // LANE):
        for k in range(seg):
            o_ref[j, pl.ds(k, 8, stride=seg), :] = y_ref[8 * k:8 * k + 8, j * LANE:(j + 1) * LANE]


def _s5(z3, b_re, b_im, c_re, c_im, ap_re, ap_im, d_skip, glu_w, glu_b, layer, *, tile):
    b, s, _ = z3.shape
    nblk = S5_WIDTH // LANE
    ublk = (ATTN_WIDTH + 2 * ATTN_KV_WIDTH + 4 * DN_WIDTH) // LANE
    const = lambda a: pl.BlockSpec(a.shape, lambda bi, t: (0, 0))
    layered = lambda a: pl.BlockSpec((None,) + a.shape[1:], lambda bi, t: (layer, 0, 0))
    ucol = lambda j: pl.BlockSpec((None, tile, LANE), lambda bi, t: (bi, t, ublk + j))
    return pl.pallas_call(
        _s5_kernel,
        grid=(b, s // tile),
        in_specs=[
            ucol(0), ucol(1), ucol(2), ucol(3),
            layered(b_re), layered(b_im), layered(c_re), layered(c_im), layered(ap_re), layered(ap_im),
            const(d_skip), layered(glu_w), const(glu_b),
        ],
        out_specs=pl.BlockSpec((nblk, None, tile, LANE), lambda bi, t: (0, bi, t, 0)),
        out_shape=jax.ShapeDtypeStruct((nblk, b, s, LANE), F32),
        scratch_shapes=[
            pltpu.VMEM((tile, S5_WIDTH), F32),
            pltpu.VMEM((tile, S5_LANES), F32),
            pltpu.VMEM((tile, S5_LANES), F32),
            pltpu.VMEM((tile, S5_LANES), BF16),
            pltpu.VMEM((tile, S5_LANES), BF16),
            pltpu.VMEM((8, S5_LANES), F32),
            pltpu.VMEM((8, S5_LANES), F32),
            pltpu.VMEM((8, S5_LANES), F32),
            pltpu.VMEM((tile, S5_WIDTH), F32),
        ],
        compiler_params=_params("parallel", "arbitrary"),
        name="s5",
    )(z3, z3, z3, z3, b_re, b_im, c_re, c_im, ap_re, ap_im, d_skip, glu_w, glu_b)


def _rope_tables(seq):
    half = HEAD_DIM // 2
    inv_freq = ROPE_THETA ** (-jnp.arange(half, dtype=F32) / half)
    ang = jnp.arange(seq, dtype=F32)[:, None] * inv_freq[None, :]
    cos, sin = jnp.cos(ang), jnp.sin(ang)
    return jnp.concatenate([cos, cos], axis=-1), jnp.concatenate([-sin, sin], axis=-1)


def _split_w_in(w_in):
    s5_src = SRC_BA + 2 * DN_HEADS
    pad = jnp.zeros(w_in.shape[:-1] + (LANE - 2 * DN_HEADS,), BF16)
    w_b = w_in.astype(BF16)
    w_ba = jnp.concatenate([w_b[..., SRC_BA:s5_src], pad], axis=-1)
    return w_b, w_b[..., s5_src:], w_ba


def _lane_pad(v, offset):
    return jnp.zeros((v.shape[0], 1, LANE), F32).at[:, 0, offset:offset + v.shape[1]].set(v)


def kernel(x, ff1_norm_pre, ff1_w_gate, ff1_w_up, ff1_w_down, ff1_norm_post, mix_norm_pre, w_in,
           attn_sinks, dn_conv_w, dn_a_log, dn_dt_bias, dn_norm_w, s5_a_re, s5_a_im, s5_log_dt,
           s5_b_re, s5_b_im, s5_c_re, s5_c_im, s5_d, s5_glu_w, s5_glu_b, w_out, mix_norm_post,
           ff2_norm_pre, ff2_w_gate, ff2_w_up, ff2_w_down, ff2_norm_post):
    b, s, d = x.shape
    depth = w_in.shape[0]
    n = b * s
    tm = min(512, n)
    tm_big = min(1024, n)
    tf = 512
    tile = min(SEQ_TILE, s)

    bf = lambda a: a.astype(BF16)
    ff1 = (ff1_w_gate, ff1_w_up, ff1_w_down)
    ff2 = (ff2_w_gate, ff2_w_up, ff2_w_down)
    w_in_parts = _split_w_in(w_in)
    w_out_b = bf(w_out)
    glu_w_b = bf(s5_glu_w)
    ap_re, ap_im, *s5_mats = _s5_prep(s5_a_re, s5_a_im, s5_log_dt, s5_b_re, s5_b_im, s5_c_re, s5_c_im)
    cc, ss = _rope_tables(s)
    alog_pad = _lane_pad(dn_a_log, DN_HEADS)
    dtb_pad = _lane_pad(dn_dt_bias, DN_HEADS)
    row = lambda a, l: a[l].reshape(1, -1)

    xf = x.reshape(n, d)
    for l in range(depth):
        xf = _ffn(xf, row(ff1_norm_pre, l), *ff1, row(ff1_norm_post, l), l, tm=tm_big, tf=tf, tf_head=tf // 2)

        z3 = _inproj(xf, row(mix_norm_pre, l), *w_in_parts, l, tm=tm).reshape(b, s, Z_WIDTH)
        y_attn = _attention(z3, attn_sinks[l], cc, ss, tile=tile)
        y_dn = _deltanet(z3, dn_conv_w[l], alog_pad[l], dtb_pad[l], row(dn_norm_w, l), tile=tile)
        y_s5 = _s5(z3, *s5_mats, ap_re, ap_im, row(s5_d, l), glu_w_b, row(s5_glu_b, l), l, tile=tile)
        xf = _outproj(xf, y_attn.reshape(n, -1), y_dn.reshape(n, -1), y_s5.reshape(-1, n, LANE),
                      w_out_b, row(mix_norm_post, l), l, tm=tm)

        xf = _ffn(xf, row(ff2_norm_pre, l), *ff2, row(ff2_norm_post, l), l, tm=tm_big, tf=tf, tf_head=tf // 2)
    return xf.reshape(b, s, d)
```

```python
import functools
import math

import jax
import jax.numpy as jnp
from jax import lax
from jax.experimental import pallas as pl
from jax.experimental.pallas import tpu as pltpu

F32 = jnp.float32
BF16 = jnp.bfloat16
HIGHEST = lax.Precision.HIGHEST

NORM_EPS = 1e-6
FFN_RES_WEIGHT = 0.5
ROPE_THETA = 10000.0

HEAD_DIM = 128
WINDOW = 128
ATTN_HEADS = 8
ATTN_KV_HEADS = 2
ATTN_GROUP = ATTN_HEADS // ATTN_KV_HEADS
DN_HEADS = 4
DN_CONV = 4
DN_CHUNK = 64
S5_GROUPS = 32
S5_GROUP_CH = 16
S5_STATE = 64
ATTN_WIDTH = ATTN_HEADS * HEAD_DIM
ATTN_KV_WIDTH = ATTN_KV_HEADS * HEAD_DIM
DN_WIDTH = DN_HEADS * HEAD_DIM
S5_WIDTH = S5_GROUPS * S5_GROUP_CH
S5_LANES = S5_GROUPS * S5_STATE
MIX_WIDTH = ATTN_WIDTH + DN_WIDTH + S5_WIDTH

LANE = 128
Z_WIDTH = ATTN_WIDTH + 2 * ATTN_KV_WIDTH + 4 * DN_WIDTH + S5_WIDTH + LANE
Z_BA_BLOCK = (Z_WIDTH - LANE) // LANE
SRC_BA = ATTN_WIDTH + 2 * ATTN_KV_WIDTH + 4 * DN_WIDTH

SEQ_TILE = 512
S5_SEG = SEQ_TILE // 8
FFN_DOWN_COLS = 512
FFN_SUB_COLS = 256
NORM_ROWS = 128
VMEM_LIMIT = 56 * 1024 * 1024


def _params(*sem):
    return pltpu.CompilerParams(dimension_semantics=sem, vmem_limit_bytes=VMEM_LIMIT)


def _rms(x, gain):
    return x * lax.rsqrt(jnp.mean(x * x, axis=-1, keepdims=True) + NORM_EPS) * gain


def _sigmoid(x):
    return 0.5 * jnp.tanh(0.5 * x) + 0.5


def _ffn_kernel(*refs, cast_weights):
    if cast_weights:
        x_ref, gpre_ref, wg_ref, wu_ref, wd_ref, gpost_ref, o_ref, wgb_ref, wub_ref, wdb_ref, h_ref = refs
        wgb_ref[...] = wg_ref[...].astype(BF16)
        wub_ref[...] = wu_ref[...].astype(BF16)
        wdb_ref[...] = wd_ref[...].astype(BF16)
        wg_ref, wu_ref, wd_ref = wgb_ref, wub_ref, wdb_ref
    else:
        x_ref, gpre_ref, wg_ref, wu_ref, wd_ref, gpost_ref, o_ref, h_ref = refs
    j = pl.program_id(1)

    row_blocks = [slice(r0, r0 + NORM_ROWS) for r0 in range(0, x_ref.shape[0], NORM_ROWS)]

    @pl.when(j == 0)
    def _():
        for rows in row_blocks:
            h_ref[rows, :] = _rms(x_ref[rows, :], gpre_ref[...]).astype(BF16)
        o_ref[...] = jnp.zeros_like(o_ref)

    h = h_ref[...]
    tf = wg_ref.shape[1]
    sub = min(FFN_SUB_COLS, tf)
    chunk = min(FFN_DOWN_COLS, o_ref.shape[1])
    for f0 in range(0, tf, sub):
        fs = slice(f0, f0 + sub)
        g = jnp.dot(h, wg_ref[:, fs], preferred_element_type=F32)
        u = jnp.dot(h, wu_ref[:, fs], preferred_element_type=F32)
        a = (g * _sigmoid(g) * u).astype(BF16)
        for c0 in range(0, o_ref.shape[1], chunk):
            cols = slice(c0, c0 + chunk)
            o_ref[:, cols] += jnp.dot(a, wd_ref[fs, cols], preferred_element_type=F32)

    @pl.when(j == pl.num_programs(1) - 1)
    def _():
        for rows in row_blocks:
            o_ref[rows, :] = x_ref[rows, :] + FFN_RES_WEIGHT * _rms(o_ref[rows, :], gpost_ref[...])


def _ffn(x, gpre, wg, wu, wd, gpost, layer, *, tm, tf, tf_head):
    n, d = x.shape
    f = wg.shape[2]
    once = pl.Buffered(1)
    vec = pl.BlockSpec((1, d), lambda i, j: (0, 0))
    shp = jax.ShapeDtypeStruct
    y, wgb, wub, wdb = pl.pallas_call(
        functools.partial(_ffn_kernel, cast_weights=True),
        grid=(1, f // tf_head),
        in_specs=[
            pl.BlockSpec((tm, d), lambda i, j: (0, 0), pipeline_mode=once),
            vec,
            pl.BlockSpec((None, d, tf_head), lambda i, j: (layer, 0, j)),
            pl.BlockSpec((None, d, tf_head), lambda i, j: (layer, 0, j)),
            pl.BlockSpec((None, tf_head, d), lambda i, j: (layer, j, 0)),
            vec,
        ],
        out_specs=[
            pl.BlockSpec((tm, d), lambda i, j: (0, 0), pipeline_mode=once),
            pl.BlockSpec((d, tf_head), lambda i, j: (0, j)),
            pl.BlockSpec((d, tf_head), lambda i, j: (0, j)),
            pl.BlockSpec((tf_head, d), lambda i, j: (j, 0)),
        ],
        out_shape=(shp((n, d), F32), shp((d, f), BF16), shp((d, f), BF16), shp((f, d), BF16)),
        input_output_aliases={0: 0},
        scratch_shapes=[pltpu.VMEM((tm, d), BF16)],
        compiler_params=_params("arbitrary", "arbitrary"),
        name="ffn_head",
    )(x, gpre, wg, wu, wd, gpost)
    return pl.pallas_call(
        functools.partial(_ffn_kernel, cast_weights=False),
        grid=(n // tm - 1, f // tf),
        in_specs=[
            pl.BlockSpec((tm, d), lambda i, j: (i + 1, 0)),
            vec,
            pl.BlockSpec((d, tf), lambda i, j: (0, j)),
            pl.BlockSpec((d, tf), lambda i, j: (0, j)),
            pl.BlockSpec((tf, d), lambda i, j: (j, 0)),
            vec,
        ],
        out_specs=pl.BlockSpec((tm, d), lambda i, j: (i + 1, 0)),
        out_shape=shp((n, d), F32),
        input_output_aliases={0: 0},
        scratch_shapes=[pltpu.VMEM((tm, d), BF16)],
        compiler_params=_params("parallel", "arbitrary"),
        name="ffn",
    )(y, gpre, wgb, wub, wdb, gpost)


def _inproj_kernel(x_ref, g_ref, wm_ref, ws_ref, wb_ref, o_ref):
    h = _rms(x_ref[...], g_ref[...]).astype(BF16)
    s5_at = wm_ref.shape[1]
    ba_at = s5_at + ws_ref.shape[1]
    o_ref[:, :s5_at] = jnp.dot(h, wm_ref[...], preferred_element_type=F32)
    o_ref[:, s5_at:ba_at] = jnp.dot(h, ws_ref[...], preferred_element_type=F32)
    o_ref[:, ba_at:] = jnp.dot(h, wb_ref[...], preferred_element_type=F32)


def _inproj(x, gain, w_all, w_s5, w_ba, layer, *, tm):
    n, d = x.shape
    resident = lambda shape: pl.BlockSpec((None,) + shape, lambda i: (layer, 0, 0), pipeline_mode=pl.Buffered(1))
    return pl.pallas_call(
        _inproj_kernel,
        grid=(n // tm,),
        in_specs=[
            pl.BlockSpec((tm, d), lambda i: (i, 0)),
            pl.BlockSpec((1, d), lambda i: (0, 0)),
            resident((d, SRC_BA)), resident((d, S5_WIDTH)), resident((d, LANE)),
        ],
        out_specs=pl.BlockSpec((tm, Z_WIDTH), lambda i: (i, 0)),
        out_shape=jax.ShapeDtypeStruct((n, Z_WIDTH), F32),
        compiler_params=_params("parallel"),
        name="inproj",
    )(x, gain, w_all, w_s5, w_ba)


def _outproj_kernel(x_ref, ya_ref, yd_ref, ys_ref, w_ref, g_ref, o_ref):
    ys = [ys_ref[j].astype(BF16) for j in range(ys_ref.shape[0])]
    y = jnp.concatenate([ya_ref[...], yd_ref[...]] + ys, axis=-1)
    mixed = jnp.dot(y, w_ref[...], preferred_element_type=F32)
    o_ref[...] = x_ref[...] + _rms(mixed, g_ref[...])


def _outproj(x, ya, yd, ys, w, gain, layer, *, tm):
    n, d = x.shape
    return pl.pallas_call(
        _outproj_kernel,
        grid=(n // tm,),
        in_specs=[
            pl.BlockSpec((tm, d), lambda i: (i, 0)),
            pl.BlockSpec((tm, ya.shape[1]), lambda i: (i, 0)),
            pl.BlockSpec((tm, yd.shape[1]), lambda i: (i, 0)),
            pl.BlockSpec((ys.shape[0], tm, ys.shape[2]), lambda i: (0, i, 0)),
            pl.BlockSpec((None,) + w.shape[1:], lambda i: (layer, 0, 0)),
            pl.BlockSpec((1, d), lambda i: (0, 0)),
        ],
        out_specs=pl.BlockSpec((tm, d), lambda i: (i, 0)),
        out_shape=jax.ShapeDtypeStruct((n, d), F32),
        compiler_params=_params("parallel"),
        name="outproj",
    )(x, ya, yd, ys, w, gain)


def _rope(x, cc, ss):
    return x * cc + pltpu.roll(x, HEAD_DIM // 2, axis=1) * ss


def _attn_kernel(sink_ref, q_ref, kp_ref, kc_ref, vp_ref, vc_ref, ccp_ref, ccc_ref, ssp_ref, ssc_ref, o_ref):
    h = pl.program_id(1)
    t = pl.program_id(2)
    w = WINDOW
    grp = ATTN_GROUP
    nblk = q_ref.shape[0] // w
    ccc = ccc_ref[...]
    ssc = ssc_ref[...]
    q = q_ref[...]
    scale = HEAD_DIM ** -0.5
    qr = [(_rope(q[:, g * HEAD_DIM:(g + 1) * HEAD_DIM], ccc, ssc) * scale).astype(BF16) for g in range(grp)]
    kk = jnp.concatenate([_rope(kp_ref[...], ccp_ref[...], ssp_ref[...]), _rope(kc_ref[...], ccc, ssc)],
                         axis=0).astype(BF16)
    vv = jnp.concatenate([vp_ref[...], vc_ref[...]], axis=0).astype(BF16)
    qi = lax.broadcasted_iota(jnp.int32, (w, 2 * w), 0) + w
    kj = lax.broadcasted_iota(jnp.int32, (w, 2 * w), 1)
    rel = qi - kj
    band = (rel >= 0) & (rel < w)
    first = band & ((kj >= w) | (t > 0))
    row_head = lax.broadcasted_iota(jnp.int32, (grp * w, 1), 0) // w
    sink = jnp.zeros((grp * w, 1), F32)
    for g in range(grp):
        sink = jnp.where(row_head == g, sink_ref[h * grp + g], sink)
    scores = []
    for blk in range(nblk):
        qs = jnp.concatenate([qr[g][blk * w:(blk + 1) * w] for g in range(grp)], axis=0)
        scores.append(_dot_nt(qs, kk[blk * w:(blk + 2) * w]))
    probs = []
    for blk in range(nblk):
        mask = jnp.concatenate([first if blk == 0 else band] * grp, axis=0)
        sc = jnp.where(mask, scores[blk], -jnp.inf)
        m = jnp.maximum(jnp.max(sc, axis=-1, keepdims=True), sink)
        p = jnp.exp(sc - m)
        denom = jnp.sum(p, axis=-1, keepdims=True) + jnp.exp(sink - m)
        probs.append((p / denom).astype(BF16))
    outs = [jnp.dot(probs[blk], vv[blk * w:(blk + 2) * w], preferred_element_type=F32) for blk in range(nblk)]
    for blk in range(nblk):
        o_ref[blk * w:(blk + 1) * w, :] = jnp.concatenate(
            [outs[blk][g * w:(g + 1) * w] for g in range(grp)], axis=1).astype(o_ref.dtype)


def _attention(z3, sinks, cc, ss, *, tile):
    b, s, _ = z3.shape
    w = WINDOW
    nblk = tile // w
    qw = ATTN_GROUP * HEAD_DIM
    kblk = ATTN_WIDTH // HEAD_DIM
    vblk = (ATTN_WIDTH + ATTN_KV_WIDTH) // HEAD_DIM
    prev = lambda t: jnp.maximum(t * nblk - 1, 0)
    return pl.pallas_call(
        _attn_kernel,
        grid=(b, ATTN_KV_HEADS, s // tile),
        in_specs=[
            pl.BlockSpec(memory_space=pltpu.SMEM),
            pl.BlockSpec((None, tile, qw), lambda bi, h, t: (bi, t, h)),
            pl.BlockSpec((None, w, HEAD_DIM), lambda bi, h, t: (bi, prev(t), kblk + h)),
            pl.BlockSpec((None, tile, HEAD_DIM), lambda bi, h, t: (bi, t, kblk + h)),
            pl.BlockSpec((None, w, HEAD_DIM), lambda bi, h, t: (bi, prev(t), vblk + h)),
            pl.BlockSpec((None, tile, HEAD_DIM), lambda bi, h, t: (bi, t, vblk + h)),
            pl.BlockSpec((w, HEAD_DIM), lambda bi, h, t: (prev(t), 0)),
            pl.BlockSpec((tile, HEAD_DIM), lambda bi, h, t: (t, 0)),
            pl.BlockSpec((w, HEAD_DIM), lambda bi, h, t: (prev(t), 0)),
            pl.BlockSpec((tile, HEAD_DIM), lambda bi, h, t: (t, 0)),
        ],
        out_specs=pl.BlockSpec((None, tile, qw), lambda bi, h, t: (bi, t, h)),
        out_shape=jax.ShapeDtypeStruct((b, s, ATTN_WIDTH), BF16),
        compiler_params=_params("parallel", "parallel", "arbitrary"),
        name="swa",
    )(sinks, z3, z3, z3, z3, z3, cc, cc, ss, ss)


def _silu(x):
    return x * _sigmoid(x)


def _softplus(x):
    return jnp.maximum(x, 0.0) + jnp.log(1.0 + jnp.exp(-jnp.abs(x)))


def _dot_nt(a, b, precision=None):
    return lax.dot_general(a, b, (((1,), (1,)), ((), ())), preferred_element_type=F32, precision=precision)


def _dot_tn(a, b, precision=None):
    return lax.dot_general(a, b, (((0,), (0,)), ((), ())), preferred_element_type=F32, precision=precision)


def _dn_kernel(q_ref, k_ref, v_ref, zg_ref, ba_ref, cw_ref, alog_ref, dtb_ref, nw_ref, o_ref,
               ext_ref, qn_ref, kn_ref, vn_ref, gc_ref, gct_ref, beta_ref, s_ref):
    t = pl.program_id(1)
    tile = q_ref.shape[0]
    c = DN_CHUNK
    d = HEAD_DIM
    nh = DN_HEADS
    wdt = nh * d

    @pl.when(t == 0)
    def _():
        ext_ref[0:8, :] = jnp.zeros((8, 3 * wdt), F32)
        s_ref[...] = jnp.zeros_like(s_ref)

    ext_ref[8:8 + tile, 0:wdt] = q_ref[...]
    ext_ref[8:8 + tile, wdt:2 * wdt] = k_ref[...]
    ext_ref[8:8 + tile, 2 * wdt:3 * wdt] = v_ref[...]
    ext = ext_ref[...]
    conv = ext[8:] * cw_ref[DN_CONV - 1:DN_CONV, :]
    for back in range(1, DN_CONV):
        conv = conv + pltpu.roll(ext, back, axis=0)[8:] * cw_ref[DN_CONV - 1 - back:DN_CONV - back, :]
    ext_ref[0:8, :] = ext[tile:tile + 8]
    qkv = _silu(conv)
    for hd in range(nh):
        qh = qkv[:, hd * d:(hd + 1) * d]
        kh = qkv[:, wdt + hd * d:wdt + (hd + 1) * d]
        qn_ref[:, hd * d:(hd + 1) * d] = qh * lax.rsqrt(jnp.sum(qh * qh, axis=-1, keepdims=True) + NORM_EPS) * (d ** -0.5)
        kn_ref[:, hd * d:(hd + 1) * d] = kh * lax.rsqrt(jnp.sum(kh * kh, axis=-1, keepdims=True) + NORM_EPS)
    vn_ref[...] = qkv[:, 2 * wdt:3 * wdt]
    ba = ba_ref[...]
    beta_ref[...] = _sigmoid(ba)
    g = -jnp.exp(alog_ref[...]) * _softplus(ba + dtb_ref[...])
    ti = lax.broadcasted_iota(jnp.int32, (c, c), 0)
    tj = lax.broadcasted_iota(jnp.int32, (c, c), 1)
    csum = (ti >= tj).astype(F32)
    gc_all = jnp.concatenate([jnp.dot(csum, g[ci * c:(ci + 1) * c], preferred_element_type=F32, precision=HIGHEST)
                              for ci in range(tile // c)], axis=0)
    gc_ref[...] = gc_all
    gct_ref[...] = gc_all.T[0:8, :]

    row = lax.broadcasted_iota(jnp.int32, (c, c), 0)
    col = lax.broadcasted_iota(jnp.int32, (c, c), 1)
    causal = row >= col
    strict = row > col
    eye = (row == col).astype(F32)
    nw = nw_ref[...]

    def join_mask(half):
        return (row // (2 * half) == col // (2 * half)) & (row % (2 * half) >= half) & (col % (2 * half) < half)

    def local_stages(chunks, items):
        for ci in chunks:
            r0 = ci * c
            for hd in range(nh):
                lanes = slice(hd * d, (hd + 1) * d)
                q = qn_ref[r0:r0 + c, lanes]
                k = kn_ref[r0:r0 + c, lanes]
                beta = beta_ref[r0:r0 + c, hd:hd + 1]
                gcol = gc_ref[r0:r0 + c, nh + hd:nh + hd + 1]
                grow = gct_ref[nh + hd:nh + hd + 1, r0:r0 + c]
                glast = gcol[c - 1:c, :]
                decay = jnp.exp(jnp.where(causal, gcol - grow, -jnp.inf))
                eg = jnp.exp(gcol)
                kb = k * beta
                items.append(dict(ci=ci, hd=hd, r0=r0, lanes=lanes, decay=decay, kbf=k.astype(BF16),
                                  qbf=q.astype(BF16), kb_bf=kb.astype(BF16),
                                  rhs=jnp.concatenate([vn_ref[r0:r0 + c, lanes] * beta, kb * eg], axis=1).astype(BF16),
                                  qd=(q * eg).astype(BF16), k_dec=(k * jnp.exp(glast - gcol)).astype(BF16),
                                  egl=jnp.exp(glast)))
        for it in items:
            it["kk"] = _dot_nt(it["kb_bf"], it["kbf"])
            it["qk"] = _dot_nt(it["qbf"], it["kbf"])
        yield
        for it in items:
            it["nmat"] = jnp.where(strict, it["kk"] * it["decay"], 0.0)
            it["inv"] = eye - jnp.where(join_mask(1), it["nmat"], 0.0)
            it["attn"] = (it["qk"] * it["decay"]).astype(BF16)
        half = 2
        while half < c:
            for it in items:
                it["inv_bf"] = it["inv"].astype(BF16)
                join = jnp.where(join_mask(half), it["nmat"], 0.0).astype(BF16)
                it["bt"] = jnp.dot(join, it["inv_bf"], preferred_element_type=F32).astype(BF16)
            yield
            for it in items:
                it["upd"] = jnp.dot(it["inv_bf"], it["bt"], preferred_element_type=F32)
            yield
            for it in items:
                it["inv"] = it["inv"] - it["upd"]
            half *= 2
        for it in items:
            uw = jnp.dot(it["inv"].astype(BF16), it["rhs"], preferred_element_type=F32)
            it["u"] = uw[:, :d]
            it["wq"] = jnp.concatenate([uw[:, d:].astype(BF16), it["qd"]], axis=0)
        yield

    def sweep_stages(items, states):
        for ci in sorted({it["ci"] for it in items}):
            group = [it for it in items if it["ci"] == ci]
            for it in group:
                it["ws_qs"] = jnp.dot(it["wq"], states[it["hd"]].astype(BF16), preferred_element_type=F32)
            yield
            for it in group:
                it["v_new"] = (it["u"] - it["ws_qs"][:c]).astype(BF16)
            for it in group:
                it["av"] = jnp.dot(it["attn"], it["v_new"], preferred_element_type=F32)
                it["kv"] = _dot_tn(it["k_dec"], it["v_new"])
            yield
            for it in group:
                states[it["hd"]] = states[it["hd"]] * it["egl"] + it["kv"]
                o = it["ws_qs"][c:] + it["av"]
                o = o * lax.rsqrt(jnp.mean(o * o, axis=-1, keepdims=True) + NORM_EPS) * nw
                o = o * _silu(zg_ref[it["r0"]:it["r0"] + c, it["lanes"]])
                o_ref[it["r0"]:it["r0"] + c, it["lanes"]] = o.astype(o_ref.dtype)

    def interleave(*gens):
        live = list(gens)
        while live:
            for gen in list(live):
                if next(gen, "end") == "end":
                    live.remove(gen)

    states = [s_ref[hd] for hd in range(nh)]
    group_chunks = 4
    groups = [list(range(g0, g0 + group_chunks)) for g0 in range(0, tile // c, group_chunks)]
    ready = []
    interleave(local_stages(groups[0], ready))
    for nxt in groups[1:]:
        upcoming = []
        interleave(local_stages(nxt, upcoming), sweep_stages(ready, states))
        ready = upcoming
    interleave(sweep_stages(ready, states))
    for hd in range(nh):
        s_ref[hd] = states[hd]


def _deltanet(z3, conv_w, alog_pad, dtb_pad, norm_w, *, tile):
    b, s, _ = z3.shape
    wdt = DN_WIDTH
    qblk = (ATTN_WIDTH + 2 * ATTN_KV_WIDTH) // wdt
    seq = lambda off: pl.BlockSpec((None, tile, wdt), lambda bi, t: (bi, t, qblk + off))
    const = lambda shape: pl.BlockSpec(shape, lambda bi, t: (0, 0))
    return pl.pallas_call(
        _dn_kernel,
        grid=(b, s // tile),
        in_specs=[
            seq(0), seq(1), seq(2), seq(3),
            pl.BlockSpec((None, tile, LANE), lambda bi, t: (bi, t, Z_BA_BLOCK)),
            const(conv_w.shape), const((1, LANE)), const((1, LANE)), const((1, HEAD_DIM)),
        ],
        out_specs=pl.BlockSpec((None, tile, wdt), lambda bi, t: (bi, t, 0)),
        out_shape=jax.ShapeDtypeStruct((b, s, wdt), BF16),
        scratch_shapes=[
            pltpu.VMEM((tile + 8, 3 * wdt), F32),
            pltpu.VMEM((tile, wdt), F32),
            pltpu.VMEM((tile, wdt), F32),
            pltpu.VMEM((tile, wdt), F32),
            pltpu.VMEM((tile, LANE), F32),
            pltpu.VMEM((8, tile), F32),
            pltpu.VMEM((tile, LANE), F32),
            pltpu.VMEM((DN_HEADS, HEAD_DIM, HEAD_DIM), F32),
        ],
        compiler_params=_params("parallel", "arbitrary"),
        name="deltanet",
    )(z3, z3, z3, z3, z3, conv_w, alog_pad, dtb_pad, norm_w)


def _s5_prep_kernel(are_ref, aim_ref, ldt_ref, bre_ref, bim_ref, cre_ref, cim_ref,
                    apr_ref, api_ref, bbr_ref, bbi_ref, cbr_ref, cbi_ref):
    lr = are_ref[...]
    li = aim_ref[...]
    dt = jnp.exp(ldt_ref[...])
    step = (lax.broadcasted_iota(jnp.int32, (S5_SEG, 1), 0) + 1).astype(F32)
    mag = jnp.exp(step * (lr * dt))
    ang = step * (li * dt)
    apr_ref[...] = mag * jnp.cos(ang)
    api_ref[...] = mag * jnp.sin(ang)
    m1 = jnp.exp(lr * dt)
    nr = m1 * jnp.cos(li * dt) - 1.0
    ni = m1 * jnp.sin(li * dt)
    den = lr * lr + li * li
    cr = (nr * lr + ni * li) / den
    ci = (ni * lr - nr * li) / den
    br = bre_ref[...]
    bi = bim_ref[...]
    mats = ((bbr_ref, cr * br - ci * bi), (bbi_ref, cr * bi + ci * br), (cbr_ref, cre_ref[...]), (cbi_ref, cim_ref[...]))
    lane_group = lax.broadcasted_iota(jnp.int32, (S5_GROUP_CH, S5_LANES), 1) // S5_STATE
    for g in range(S5_GROUPS):
        for out_ref, val in mats:
            out_ref[g * S5_GROUP_CH:(g + 1) * S5_GROUP_CH, :] = jnp.where(lane_group == g, val, 0.0).astype(BF16)


def _s5_prep(a_re, a_im, log_dt, b_re, b_im, c_re, c_im):
    depth = a_re.shape[0]
    row = lambda a: a.reshape(depth, 1, S5_LANES)
    ldt = jnp.repeat(log_dt, S5_STATE, axis=-1).reshape(depth, 1, S5_LANES)
    b_rows = lambda a: a.reshape(depth, S5_LANES, S5_GROUP_CH).transpose(0, 2, 1)
    c_rows = lambda a: a.transpose(0, 2, 1, 3).reshape(depth, S5_GROUP_CH, S5_LANES)
    shp = jax.ShapeDtypeStruct
    per_layer = lambda rows: pl.BlockSpec((None, rows, S5_LANES), lambda l: (l, 0, 0))
    chan = per_layer(S5_GROUP_CH)
    return pl.pallas_call(
        _s5_prep_kernel,
        grid=(depth,),
        in_specs=[per_layer(1), per_layer(1), per_layer(1), chan, chan, chan, chan],
        out_specs=[per_layer(S5_SEG), per_layer(S5_SEG)] + [per_layer(S5_WIDTH)] * 4,
        out_shape=(shp((depth, S5_SEG, S5_LANES), F32),) * 2 + (shp((depth, S5_WIDTH, S5_LANES), BF16),) * 4,
        name="s5_prep",
    )(row(a_re), row(a_im), ldt, b_rows(b_re), b_rows(b_im), c_rows(c_re), c_rows(c_im))


def _s5_kernel(u0_ref, u1_ref, u2_ref, u3_ref, bre_ref, bim_ref, cre_ref, cim_ref, apr_ref, api_ref, d_ref,
               gw_ref, gb_ref, o_ref, up_ref, xr_ref, xi_ref, xrb_ref, xib_ref, cr_ref, ci_ref, st_ref, y_ref):
    t = pl.program_id(1)
    tile = u0_ref.shape[0]
    seg = tile // 8
    strip = 1024

    @pl.when(t == 0)
    def _():
        st_ref[...] = jnp.zeros_like(st_ref)

    for j, u_ref in enumerate((u0_ref, u1_ref, u2_ref, u3_ref)):
        for k in range(seg):
            up_ref[8 * k:8 * k + 8, j * LANE:(j + 1) * LANE] = u_ref[pl.ds(k, 8, stride=seg), :]
    hw, hl = S5_WIDTH // 2, S5_LANES // 2
    for half in range(2):
        ub = up_ref[:, half * hw:(half + 1) * hw].astype(BF16)
        rows, cols = slice(half * hw, (half + 1) * hw), slice(half * hl, (half + 1) * hl)
        xr_ref[:, cols] = jnp.dot(ub, bre_ref[rows, cols], preferred_element_type=F32)
        xi_ref[:, cols] = jnp.dot(ub, bim_ref[rows, cols], preferred_element_type=F32)

    for s0 in range(0, S5_LANES, strip):
        lanes = slice(s0, s0 + strip)
        ar = jnp.broadcast_to(apr_ref[0:1, lanes], (8, strip))
        ai = jnp.broadcast_to(api_ref[0:1, lanes], (8, strip))

        def scan(k, carry):
            pr, pi = carry
            r0 = pl.multiple_of(k * 8, 8)
            nr = ar * pr - ai * pi + xr_ref[pl.ds(r0, 8), lanes]
            ni = ar * pi + ai * pr + xi_ref[pl.ds(r0, 8), lanes]
            xr_ref[pl.ds(r0, 8), lanes] = nr
            xi_ref[pl.ds(r0, 8), lanes] = ni
            return nr, ni

        zero = jnp.zeros((8, strip), F32)
        fr, fi = lax.fori_loop(0, seg, scan, (zero, zero), unroll=4)

        a64r = apr_ref[seg - 1:seg, lanes]
        a64i = api_ref[seg - 1:seg, lanes]
        c_r = st_ref[0:1, lanes]
        c_i = st_ref[1:2, lanes]
        for r in range(8):
            cr_ref[r:r + 1, lanes] = c_r
            ci_ref[r:r + 1, lanes] = c_i
            n_r = a64r * c_r - a64i * c_i + fr[r:r + 1]
            n_i = a64r * c_i + a64i * c_r + fi[r:r + 1]
            c_r, c_i = n_r, n_i
        st_ref[0:1, lanes] = c_r
        st_ref[1:2, lanes] = c_i

        cin_r = jnp.concatenate([cr_ref[:, lanes]] * 2, axis=0)
        cin_i = jnp.concatenate([ci_ref[:, lanes]] * 2, axis=0)

        def fix(k2, carry):
            r0 = pl.multiple_of(k2 * 16, 16)
            pw = lambda ref, k: jnp.broadcast_to(ref[pl.ds(k, 1), lanes], (8, strip))
            pr = jnp.concatenate([pw(apr_ref, 2 * k2), pw(apr_ref, 2 * k2 + 1)], axis=0)
            pi = jnp.concatenate([pw(api_ref, 2 * k2), pw(api_ref, 2 * k2 + 1)], axis=0)
            xrb_ref[pl.ds(r0, 16), lanes] = (xr_ref[pl.ds(r0, 16), lanes] + pr * cin_r - pi * cin_i).astype(BF16)
            xib_ref[pl.ds(r0, 16), lanes] = (xi_ref[pl.ds(r0, 16), lanes] + pr * cin_i + pi * cin_r).astype(BF16)
            return carry

        lax.fori_loop(0, seg // 2, fix, 0, unroll=2)

    ys = []
    for half in range(2):
        states, chans = slice(half * hl, (half + 1) * hl), slice(half * hw, (half + 1) * hw)
        ys.append(_dot_nt(xrb_ref[:, states], cre_ref[chans, states])
                  - _dot_nt(xib_ref[:, states], cim_ref[chans, states]))
    y = jnp.concatenate(ys, axis=1)
    y = y + d_ref[...] * up_ref[...]
    y = 0.5 * y * (1.0 + jnp.tanh(math.sqrt(2.0 / math.pi) * (y + 0.044715 * (y * y * y))))
    gate = jnp.dot(y.astype(BF16), gw_ref[...], preferred_element_type=F32) + gb_ref[...]
    y_ref[...] = y * _sigmoid(gate)
    for j in range(S5_WIDTH // LANE):
        for k in range(seg):
            o_ref[j, pl.ds(k, 8, stride=seg), :] = y_ref[8 * k:8 * k + 8, j * LANE:(j + 1) * LANE]


def _s5(z3, b_re, b_im, c_re, c_im, ap_re, ap_im, d_skip, glu_w, glu_b, layer, *, tile):
    b, s, _ = z3.shape
    nblk = S5_WIDTH // LANE
    ublk = (ATTN_WIDTH + 2 * ATTN_KV_WIDTH + 4 * DN_WIDTH) // LANE
    const = lambda a: pl.BlockSpec(a.shape, lambda bi, t: (0, 0))
    layered = lambda a: pl.BlockSpec((None,) + a.shape[1:], lambda bi, t: (layer, 0, 0))
    ucol = lambda j: pl.BlockSpec((None, tile, LANE), lambda bi, t: (bi, t, ublk + j))
    return pl.pallas_call(
        _s5_kernel,
        grid=(b, s // tile),
        in_specs=[
            ucol(0), ucol(1), ucol(2), ucol(3),
            layered(b_re), layered(b_im), layered(c_re), layered(c_im), layered(ap_re), layered(ap_im),
            const(d_skip), layered(glu_w), const(glu_b),
        ],
        out_specs=pl.BlockSpec((nblk, None, tile, LANE), lambda bi, t: (0, bi, t, 0)),
        out_shape=jax.ShapeDtypeStruct((nblk, b, s, LANE), F32),
        scratch_shapes=[
            pltpu.VMEM((tile, S5_WIDTH), F32),
            pltpu.VMEM((tile, S5_LANES), F32),
            pltpu.VMEM((tile, S5_LANES), F32),
            pltpu.VMEM((tile, S5_LANES), BF16),
            pltpu.VMEM((tile, S5_LANES), BF16),
            pltpu.VMEM((8, S5_LANES), F32),
            pltpu.VMEM((8, S5_LANES), F32),
            pltpu.VMEM((8, S5_LANES), F32),
            pltpu.VMEM((tile, S5_WIDTH), F32),
        ],
        compiler_params=_params("parallel", "arbitrary"),
        name="s5",
    )(z3, z3, z3, z3, b_re, b_im, c_re, c_im, ap_re, ap_im, d_skip, glu_w, glu_b)


def _rope_tables(seq):
    half = HEAD_DIM // 2
    inv_freq = ROPE_THETA ** (-jnp.arange(half, dtype=F32) / half)
    ang = jnp.arange(seq, dtype=F32)[:, None] * inv_freq[None, :]
    cos, sin = jnp.cos(ang), jnp.sin(ang)
    return jnp.concatenate([cos, cos], axis=-1), jnp.concatenate([-sin, sin], axis=-1)


def _split_w_in(w_in):
    s5_src = SRC_BA + 2 * DN_HEADS
    pad = jnp.zeros(w_in.shape[:-1] + (LANE - 2 * DN_HEADS,), BF16)
    w_b = w_in.astype(BF16)
    w_ba = jnp.concatenate([w_b[..., SRC_BA:s5_src], pad], axis=-1)
    return w_b, w_b[..., s5_src:], w_ba


def _lane_pad(v, offset):
    return jnp.zeros((v.shape[0], 1, LANE), F32).at[:, 0, offset:offset + v.shape[1]].set(v)


def kernel(x, ff1_norm_pre, ff1_w_gate, ff1_w_up, ff1_w_down, ff1_norm_post, mix_norm_pre, w_in,
           attn_sinks, dn_conv_w, dn_a_log, dn_dt_bias, dn_norm_w, s5_a_re, s5_a_im, s5_log_dt,
           s5_b_re, s5_b_im, s5_c_re, s5_c_im, s5_d, s5_glu_w, s5_glu_b, w_out, mix_norm_post,
           ff2_norm_pre, ff2_w_gate, ff2_w_up, ff2_w_down, ff2_norm_post):
    b, s, d = x.shape
    depth = w_in.shape[0]
    n = b * s
    tm = min(512, n)
    tm_big = min(1024, n)
    tf = 512
    tile = min(SEQ_TILE, s)

    bf = lambda a: a.astype(BF16)
    ff1 = (ff1_w_gate, ff1_w_up, ff1_w_down)
    ff2 = (ff2_w_gate, ff2_w_up, ff2_w_down)
    w_in_parts = _split_w_in(w_in)
    w_out_b = bf(w_out)
    glu_w_b = bf(s5_glu_w)
    ap_re, ap_im, *s5_mats = _s5_prep(s5_a_re, s5_a_im, s5_log_dt, s5_b_re, s5_b_im, s5_c_re, s5_c_im)
    cc, ss = _rope_tables(s)
    alog_pad = _lane_pad(dn_a_log, DN_HEADS)
    dtb_pad = _lane_pad(dn_dt_bias, DN_HEADS)
    row = lambda a, l: a[l].reshape(1, -1)

    xf = x.reshape(n, d)
    for l in range(depth):
        xf = _ffn(xf, row(ff1_norm_pre, l), *ff1, row(ff1_norm_post, l), l, tm=tm_big, tf=tf, tf_head=tf // 2)

        z3 = _inproj(xf, row(mix_norm_pre, l), *w_in_parts, l, tm=tm).reshape(b, s, Z_WIDTH)
        y_attn = _attention(z3, attn_sinks[l], cc, ss, tile=tile)
        y_dn = _deltanet(z3, dn_conv_w[l], alog_pad[l], dtb_pad[l], row(dn_norm_w, l), tile=tile)
        y_s5 = _s5(z3, *s5_mats, ap_re, ap_im, row(s5_d, l), glu_w_b, row(s5_glu_b, l), l, tile=tile)
        xf = _outproj(xf, y_attn.reshape(n, -1), y_dn.reshape(n, -1), y_s5.reshape(-1, n, LANE),
                      w_out_b, row(mix_norm_post, l), l, tm=tm)

        xf = _ffn(xf, row(ff2_norm_pre, l), *ff2, row(ff2_norm_post, l), l, tm=tm_big, tf=tf, tf_head=tf // 2)
    return xf.reshape(b, s, d)
```

```python
import functools
import math

import jax
import jax.numpy as jnp
from jax import lax
from jax.experimental import pallas as pl
from jax.experimental.pallas import tpu as pltpu

F32 = jnp.float32
BF16 = jnp.bfloat16
HIGHEST = lax.Precision.HIGHEST

NORM_EPS = 1e-6
FFN_RES_WEIGHT = 0.5
ROPE_THETA = 10000.0

HEAD_DIM = 128
WINDOW = 128
ATTN_HEADS = 8
ATTN_KV_HEADS = 2
ATTN_GROUP = ATTN_HEADS // ATTN_KV_HEADS
DN_HEADS = 4
DN_CONV = 4
DN_CHUNK = 64
S5_GROUPS = 32
S5_GROUP_CH = 16
S5_STATE = 64
ATTN_WIDTH = ATTN_HEADS * HEAD_DIM
ATTN_KV_WIDTH = ATTN_KV_HEADS * HEAD_DIM
DN_WIDTH = DN_HEADS * HEAD_DIM
S5_WIDTH = S5_GROUPS * S5_GROUP_CH
S5_LANES = S5_GROUPS * S5_STATE
MIX_WIDTH = ATTN_WIDTH + DN_WIDTH + S5_WIDTH

LANE = 128
Z_WIDTH = ATTN_WIDTH + 2 * ATTN_KV_WIDTH + 4 * DN_WIDTH + S5_WIDTH + LANE
Z_BA_BLOCK = (Z_WIDTH - LANE) // LANE
SRC_BA = ATTN_WIDTH + 2 * ATTN_KV_WIDTH + 4 * DN_WIDTH

SEQ_TILE = 512
DN_TILE = 1024
S5_TILE = 1024
S5_SEG = S5_TILE // 8
FFN_DOWN_COLS = 512
FFN_SUB_COLS = 256
NORM_ROWS = 128
VMEM_LIMIT = 56 * 1024 * 1024


def _params(*sem):
    return pltpu.CompilerParams(dimension_semantics=sem, vmem_limit_bytes=VMEM_LIMIT)


def _rms(x, gain):
    return x * lax.rsqrt(jnp.mean(x * x, axis=-1, keepdims=True) + NORM_EPS) * gain


def _sigmoid(x):
    return 0.5 * jnp.tanh(0.5 * x) + 0.5


def _ffn_kernel(*refs, cast_weights):
    if cast_weights:
        x_ref, gpre_ref, wg_ref, wu_ref, wd_ref, gpost_ref, o_ref, wgb_ref, wub_ref, wdb_ref, h_ref = refs
        wgb_ref[...] = wg_ref[...].astype(BF16)
        wub_ref[...] = wu_ref[...].astype(BF16)
        wdb_ref[...] = wd_ref[...].astype(BF16)
        wg_ref, wu_ref, wd_ref = wgb_ref, wub_ref, wdb_ref
    else:
        x_ref, gpre_ref, wg_ref, wu_ref, wd_ref, gpost_ref, o_ref, h_ref = refs
    j = pl.program_id(1)

    row_blocks = [slice(r0, r0 + NORM_ROWS) for r0 in range(0, x_ref.shape[0], NORM_ROWS)]

    @pl.when(j == 0)
    def _():
        for rows in row_blocks:
            h_ref[rows, :] = _rms(x_ref[rows, :], gpre_ref[...]).astype(BF16)
        o_ref[...] = jnp.zeros_like(o_ref)

    h = h_ref[...]
    tf = wg_ref.shape[1]
    sub = min(FFN_SUB_COLS, tf)
    chunk = min(FFN_DOWN_COLS, o_ref.shape[1])
    for f0 in range(0, tf, sub):
        fs = slice(f0, f0 + sub)
        g = jnp.dot(h, wg_ref[:, fs], preferred_element_type=F32)
        u = jnp.dot(h, wu_ref[:, fs], preferred_element_type=F32)
        a = (g * _sigmoid(g) * u).astype(BF16)
        for c0 in range(0, o_ref.shape[1], chunk):
            cols = slice(c0, c0 + chunk)
            o_ref[:, cols] += jnp.dot(a, wd_ref[fs, cols], preferred_element_type=F32)

    @pl.when(j == pl.num_programs(1) - 1)
    def _():
        for rows in row_blocks:
            o_ref[rows, :] = x_ref[rows, :] + FFN_RES_WEIGHT * _rms(o_ref[rows, :], gpost_ref[...])


def _ffn(x, gpre, wg, wu, wd, gpost, layer, *, tm, tf, tf_head):
    n, d = x.shape
    f = wg.shape[2]
    once = pl.Buffered(1)
    vec = pl.BlockSpec((1, d), lambda i, j: (0, 0))
    shp = jax.ShapeDtypeStruct
    y, wgb, wub, wdb = pl.pallas_call(
        functools.partial(_ffn_kernel, cast_weights=True),
        grid=(1, f // tf_head),
        in_specs=[
            pl.BlockSpec((tm, d), lambda i, j: (0, 0), pipeline_mode=once),
            vec,
            pl.BlockSpec((None, d, tf_head), lambda i, j: (layer, 0, j)),
            pl.BlockSpec((None, d, tf_head), lambda i, j: (layer, 0, j)),
            pl.BlockSpec((None, tf_head, d), lambda i, j: (layer, j, 0)),
            vec,
        ],
        out_specs=[
            pl.BlockSpec((tm, d), lambda i, j: (0, 0), pipeline_mode=once),
            pl.BlockSpec((d, tf_head), lambda i, j: (0, j)),
            pl.BlockSpec((d, tf_head), lambda i, j: (0, j)),
            pl.BlockSpec((tf_head, d), lambda i, j: (j, 0)),
        ],
        out_shape=(shp((n, d), F32), shp((d, f), BF16), shp((d, f), BF16), shp((f, d), BF16)),
        input_output_aliases={0: 0},
        scratch_shapes=[pltpu.VMEM((tm, d), BF16)],
        compiler_params=_params("arbitrary", "arbitrary"),
        name="ffn_head",
    )(x, gpre, wg, wu, wd, gpost)
    return pl.pallas_call(
        functools.partial(_ffn_kernel, cast_weights=False),
        grid=(n // tm - 1, f // tf),
        in_specs=[
            pl.BlockSpec((tm, d), lambda i, j: (i + 1, 0)),
            vec,
            pl.BlockSpec((d, tf), lambda i, j: (0, j)),
            pl.BlockSpec((d, tf), lambda i, j: (0, j)),
            pl.BlockSpec((tf, d), lambda i, j: (j, 0)),
            vec,
        ],
        out_specs=pl.BlockSpec((tm, d), lambda i, j: (i + 1, 0)),
        out_shape=shp((n, d), F32),
        input_output_aliases={0: 0},
        scratch_shapes=[pltpu.VMEM((tm, d), BF16)],
        compiler_params=_params("parallel", "arbitrary"),
        name="ffn",
    )(y, gpre, wgb, wub, wdb, gpost)


def _inproj_kernel(x_ref, g_ref, wm_ref, ws_ref, wb_ref, o_ref):
    h = _rms(x_ref[...], g_ref[...]).astype(BF16)
    s5_at = wm_ref.shape[1]
    ba_at = s5_at + ws_ref.shape[1]
    o_ref[:, :s5_at] = jnp.dot(h, wm_ref[...], preferred_element_type=F32)
    o_ref[:, s5_at:ba_at] = jnp.dot(h, ws_ref[...], preferred_element_type=F32)
    o_ref[:, ba_at:] = jnp.dot(h, wb_ref[...], preferred_element_type=F32)


def _inproj(x, gain, w_all, w_s5, w_ba, layer, *, tm):
    n, d = x.shape
    resident = lambda shape: pl.BlockSpec((None,) + shape, lambda i: (layer, 0, 0), pipeline_mode=pl.Buffered(1))
    return pl.pallas_call(
        _inproj_kernel,
        grid=(n // tm,),
        in_specs=[
            pl.BlockSpec((tm, d), lambda i: (i, 0)),
            pl.BlockSpec((1, d), lambda i: (0, 0)),
            resident((d, SRC_BA)), resident((d, S5_WIDTH)), resident((d, LANE)),
        ],
        out_specs=pl.BlockSpec((tm, Z_WIDTH), lambda i: (i, 0)),
        out_shape=jax.ShapeDtypeStruct((n, Z_WIDTH), F32),
        compiler_params=_params("parallel"),
        name="inproj",
    )(x, gain, w_all, w_s5, w_ba)


def _outproj_kernel(x_ref, ya_ref, yd_ref, ys_ref, w_ref, g_ref, o_ref):
    ys = [ys_ref[j].astype(BF16) for j in range(ys_ref.shape[0])]
    y = jnp.concatenate([ya_ref[...], yd_ref[...]] + ys, axis=-1)
    mixed = jnp.dot(y, w_ref[...], preferred_element_type=F32)
    o_ref[...] = x_ref[...] + _rms(mixed, g_ref[...])


def _outproj(x, ya, yd, ys, w, gain, layer, *, tm):
    n, d = x.shape
    return pl.pallas_call(
        _outproj_kernel,
        grid=(n // tm,),
        in_specs=[
            pl.BlockSpec((tm, d), lambda i: (i, 0)),
            pl.BlockSpec((tm, ya.shape[1]), lambda i: (i, 0)),
            pl.BlockSpec((tm, yd.shape[1]), lambda i: (i, 0)),
            pl.BlockSpec((ys.shape[0], tm, ys.shape[2]), lambda i: (0, i, 0)),
            pl.BlockSpec((None,) + w.shape[1:], lambda i: (layer, 0, 0)),
            pl.BlockSpec((1, d), lambda i: (0, 0)),
        ],
        out_specs=pl.BlockSpec((tm, d), lambda i: (i, 0)),
        out_shape=jax.ShapeDtypeStruct((n, d), F32),
        compiler_params=_params("parallel"),
        name="outproj",
    )(x, ya, yd, ys, w, gain)


def _rope(x, cc, ss):
    return x * cc + pltpu.roll(x, HEAD_DIM // 2, axis=1) * ss


def _attn_kernel(sink_ref, q_ref, kp_ref, kc_ref, vp_ref, vc_ref, ccp_ref, ccc_ref, ssp_ref, ssc_ref, o_ref):
    h = pl.program_id(1)
    t = pl.program_id(2)
    w = WINDOW
    grp = ATTN_GROUP
    nblk = q_ref.shape[0] // w
    ccc = ccc_ref[...]
    ssc = ssc_ref[...]
    q = q_ref[...]
    scale = HEAD_DIM ** -0.5
    qr = [(_rope(q[:, g * HEAD_DIM:(g + 1) * HEAD_DIM], ccc, ssc) * scale).astype(BF16) for g in range(grp)]
    kk = jnp.concatenate([_rope(kp_ref[...], ccp_ref[...], ssp_ref[...]), _rope(kc_ref[...], ccc, ssc)],
                         axis=0).astype(BF16)
    vv = jnp.concatenate([vp_ref[...], vc_ref[...]], axis=0).astype(BF16)
    qi = lax.broadcasted_iota(jnp.int32, (w, 2 * w), 0) + w
    kj = lax.broadcasted_iota(jnp.int32, (w, 2 * w), 1)
    rel = qi - kj
    band = (rel >= 0) & (rel < w)
    first = band & ((kj >= w) | (t > 0))
    row_head = lax.broadcasted_iota(jnp.int32, (grp * w, 1), 0) // w
    sink = jnp.zeros((grp * w, 1), F32)
    for g in range(grp):
        sink = jnp.where(row_head == g, sink_ref[h * grp + g], sink)
    scores = []
    for blk in range(nblk):
        qs = jnp.concatenate([qr[g][blk * w:(blk + 1) * w] for g in range(grp)], axis=0)
        scores.append(_dot_nt(qs, kk[blk * w:(blk + 2) * w]))
    probs = []
    for blk in range(nblk):
        mask = jnp.concatenate([first if blk == 0 else band] * grp, axis=0)
        sc = jnp.where(mask, scores[blk], -jnp.inf)
        m = jnp.maximum(jnp.max(sc, axis=-1, keepdims=True), sink)
        p = jnp.exp(sc - m)
        denom = jnp.sum(p, axis=-1, keepdims=True) + jnp.exp(sink - m)
        probs.append((p / denom).astype(BF16))
    outs = [jnp.dot(probs[blk], vv[blk * w:(blk + 2) * w], preferred_element_type=F32) for blk in range(nblk)]
    for blk in range(nblk):
        o_ref[blk * w:(blk + 1) * w, :] = jnp.concatenate(
            [outs[blk][g * w:(g + 1) * w] for g in range(grp)], axis=1).astype(o_ref.dtype)


def _attention(z3, sinks, cc, ss, *, tile):
    b, s, _ = z3.shape
    w = WINDOW
    nblk = tile // w
    qw = ATTN_GROUP * HEAD_DIM
    kblk = ATTN_WIDTH // HEAD_DIM
    vblk = (ATTN_WIDTH + ATTN_KV_WIDTH) // HEAD_DIM
    prev = lambda t: jnp.maximum(t * nblk - 1, 0)
    return pl.pallas_call(
        _attn_kernel,
        grid=(b, ATTN_KV_HEADS, s // tile),
        in_specs=[
            pl.BlockSpec(memory_space=pltpu.SMEM),
            pl.BlockSpec((None, tile, qw), lambda bi, h, t: (bi, t, h)),
            pl.BlockSpec((None, w, HEAD_DIM), lambda bi, h, t: (bi, prev(t), kblk + h)),
            pl.BlockSpec((None, tile, HEAD_DIM), lambda bi, h, t: (bi, t, kblk + h)),
            pl.BlockSpec((None, w, HEAD_DIM), lambda bi, h, t: (bi, prev(t), vblk + h)),
            pl.BlockSpec((None, tile, HEAD_DIM), lambda bi, h, t: (bi, t, vblk + h)),
            pl.BlockSpec((w, HEAD_DIM), lambda bi, h, t: (prev(t), 0)),
            pl.BlockSpec((tile, HEAD_DIM), lambda bi, h, t: (t, 0)),
            pl.BlockSpec((w, HEAD_DIM), lambda bi, h, t: (prev(t), 0)),
            pl.BlockSpec((tile, HEAD_DIM), lambda bi, h, t: (t, 0)),
        ],
        out_specs=pl.BlockSpec((None, tile, qw), lambda bi, h, t: (bi, t, h)),
        out_shape=jax.ShapeDtypeStruct((b, s, ATTN_WIDTH), BF16),
        compiler_params=_params("parallel", "parallel", "arbitrary"),
        name="swa",
    )(sinks, z3, z3, z3, z3, z3, cc, cc, ss, ss)


def _silu(x):
    return x * _sigmoid(x)


def _softplus(x):
    return jnp.maximum(x, 0.0) + jnp.log(1.0 + jnp.exp(-jnp.abs(x)))


def _dot_nt(a, b, precision=None):
    return lax.dot_general(a, b, (((1,), (1,)), ((), ())), preferred_element_type=F32, precision=precision)


def _dot_tn(a, b, precision=None):
    return lax.dot_general(a, b, (((0,), (0,)), ((), ())), preferred_element_type=F32, precision=precision)


def _dn_kernel(q_ref, k_ref, v_ref, zg_ref, ba_ref, cw_ref, alog_ref, dtb_ref, nw_ref, o_ref,
               ext_ref, qn_ref, kn_ref, vn_ref, gc_ref, gct_ref, beta_ref, s_ref):
    t = pl.program_id(1)
    tile = q_ref.shape[0]
    c = DN_CHUNK
    d = HEAD_DIM
    nh = DN_HEADS
    wdt = nh * d

    @pl.when(t == 0)
    def _():
        ext_ref[0:8, :] = jnp.zeros((8, 3 * wdt), F32)
        s_ref[...] = jnp.zeros_like(s_ref)

    ext_ref[8:8 + tile, 0:wdt] = q_ref[...]
    ext_ref[8:8 + tile, wdt:2 * wdt] = k_ref[...]
    ext_ref[8:8 + tile, 2 * wdt:3 * wdt] = v_ref[...]
    ext = ext_ref[...]
    conv = ext[8:] * cw_ref[DN_CONV - 1:DN_CONV, :]
    for back in range(1, DN_CONV):
        conv = conv + pltpu.roll(ext, back, axis=0)[8:] * cw_ref[DN_CONV - 1 - back:DN_CONV - back, :]
    ext_ref[0:8, :] = ext[tile:tile + 8]
    qkv = _silu(conv)
    for hd in range(nh):
        qh = qkv[:, hd * d:(hd + 1) * d]
        kh = qkv[:, wdt + hd * d:wdt + (hd + 1) * d]
        qn_ref[:, hd * d:(hd + 1) * d] = qh * lax.rsqrt(jnp.sum(qh * qh, axis=-1, keepdims=True) + NORM_EPS) * (d ** -0.5)
        kn_ref[:, hd * d:(hd + 1) * d] = kh * lax.rsqrt(jnp.sum(kh * kh, axis=-1, keepdims=True) + NORM_EPS)
    vn_ref[...] = qkv[:, 2 * wdt:3 * wdt]
    ba = ba_ref[...]
    beta_ref[...] = _sigmoid(ba)
    g = -jnp.exp(alog_ref[...]) * _softplus(ba + dtb_ref[...])
    ti = lax.broadcasted_iota(jnp.int32, (c, c), 0)
    tj = lax.broadcasted_iota(jnp.int32, (c, c), 1)
    csum = (ti >= tj).astype(F32)
    gc_all = jnp.concatenate([jnp.dot(csum, g[ci * c:(ci + 1) * c], preferred_element_type=F32, precision=HIGHEST)
                              for ci in range(tile // c)], axis=0)
    gc_ref[...] = gc_all
    gct_ref[...] = gc_all.T[0:8, :]

    row = lax.broadcasted_iota(jnp.int32, (c, c), 0)
    col = lax.broadcasted_iota(jnp.int32, (c, c), 1)
    causal = row >= col
    strict = row > col
    eye = (row == col).astype(F32)
    nw = nw_ref[...]

    def join_mask(half):
        return (row // (2 * half) == col // (2 * half)) & (row % (2 * half) >= half) & (col % (2 * half) < half)

    def local_stages(chunks, items):
        for ci in chunks:
            r0 = ci * c
            for hd in range(nh):
                lanes = slice(hd * d, (hd + 1) * d)
                q = qn_ref[r0:r0 + c, lanes]
                k = kn_ref[r0:r0 + c, lanes]
                beta = beta_ref[r0:r0 + c, hd:hd + 1]
                gcol = gc_ref[r0:r0 + c, nh + hd:nh + hd + 1]
                grow = gct_ref[nh + hd:nh + hd + 1, r0:r0 + c]
                glast = gcol[c - 1:c, :]
                decay = jnp.exp(jnp.where(causal, gcol - grow, -jnp.inf))
                eg = jnp.exp(gcol)
                kb = k * beta
                items.append(dict(ci=ci, hd=hd, r0=r0, lanes=lanes, decay=decay, kbf=k.astype(BF16),
                                  qbf=q.astype(BF16), kb_bf=kb.astype(BF16),
                                  rhs=jnp.concatenate([vn_ref[r0:r0 + c, lanes] * beta, kb * eg], axis=1).astype(BF16),
                                  qd=(q * eg).astype(BF16), k_dec=(k * jnp.exp(glast - gcol)).astype(BF16),
                                  egl=jnp.exp(glast)))
        for it in items:
            it["kk"] = _dot_nt(it["kb_bf"], it["kbf"])
            it["qk"] = _dot_nt(it["qbf"], it["kbf"])
        yield
        for it in items:
            it["nmat"] = jnp.where(strict, it["kk"] * it["decay"], 0.0)
            it["inv"] = eye - jnp.where(join_mask(1), it["nmat"], 0.0)
            it["attn"] = (it["qk"] * it["decay"]).astype(BF16)
        half = 2
        while half < c:
            for it in items:
                it["inv_bf"] = it["inv"].astype(BF16)
                join = jnp.where(join_mask(half), it["nmat"], 0.0).astype(BF16)
                it["bt"] = jnp.dot(join, it["inv_bf"], preferred_element_type=F32).astype(BF16)
            yield
            for it in items:
                it["upd"] = jnp.dot(it["inv_bf"], it["bt"], preferred_element_type=F32)
            yield
            for it in items:
                it["inv"] = it["inv"] - it["upd"]
            half *= 2
        for it in items:
            uw = jnp.dot(it["inv"].astype(BF16), it["rhs"], preferred_element_type=F32)
            it["u"] = uw[:, :d]
            it["wq"] = jnp.concatenate([uw[:, d:].astype(BF16), it["qd"]], axis=0)
        yield

    def sweep_stages(items, states):
        for ci in sorted({it["ci"] for it in items}):
            group = [it for it in items if it["ci"] == ci]
            for it in group:
                it["ws_qs"] = jnp.dot(it["wq"], states[it["hd"]].astype(BF16), preferred_element_type=F32)
            yield
            for it in group:
                it["v_new"] = (it["u"] - it["ws_qs"][:c]).astype(BF16)
            for it in group:
                it["av"] = jnp.dot(it["attn"], it["v_new"], preferred_element_type=F32)
                it["kv"] = _dot_tn(it["k_dec"], it["v_new"])
            yield
            for it in group:
                states[it["hd"]] = states[it["hd"]] * it["egl"] + it["kv"]
                o = it["ws_qs"][c:] + it["av"]
                o = o * lax.rsqrt(jnp.mean(o * o, axis=-1, keepdims=True) + NORM_EPS) * nw
                o = o * _silu(zg_ref[it["r0"]:it["r0"] + c, it["lanes"]])
                o_ref[it["r0"]:it["r0"] + c, it["lanes"]] = o.astype(o_ref.dtype)

    def interleave(*gens):
        live = list(gens)
        while live:
            for gen in list(live):
                if next(gen, "end") == "end":
                    live.remove(gen)

    states = [s_ref[hd] for hd in range(nh)]
    group_chunks = 4
    groups = [list(range(g0, g0 + group_chunks)) for g0 in range(0, tile // c, group_chunks)]
    ready = []
    interleave(local_stages(groups[0], ready))
    for nxt in groups[1:]:
        upcoming = []
        interleave(local_stages(nxt, upcoming), sweep_stages(ready, states))
        ready = upcoming
    interleave(sweep_stages(ready, states))
    for hd in range(nh):
        s_ref[hd] = states[hd]


def _deltanet(z3, conv_w, alog_pad, dtb_pad, norm_w, *, tile):
    b, s, _ = z3.shape
    wdt = DN_WIDTH
    qblk = (ATTN_WIDTH + 2 * ATTN_KV_WIDTH) // wdt
    seq = lambda off: pl.BlockSpec((None, tile, wdt), lambda bi, t: (bi, t, qblk + off))
    const = lambda shape: pl.BlockSpec(shape, lambda bi, t: (0, 0))
    return pl.pallas_call(
        _dn_kernel,
        grid=(b, s // tile),
        in_specs=[
            seq(0), seq(1), seq(2), seq(3),
            pl.BlockSpec((None, tile, LANE), lambda bi, t: (bi, t, Z_BA_BLOCK)),
            const(conv_w.shape), const((1, LANE)), const((1, LANE)), const((1, HEAD_DIM)),
        ],
        out_specs=pl.BlockSpec((None, tile, wdt), lambda bi, t: (bi, t, 0)),
        out_shape=jax.ShapeDtypeStruct((b, s, wdt), BF16),
        scratch_shapes=[
            pltpu.VMEM((tile + 8, 3 * wdt), F32),
            pltpu.VMEM((tile, wdt), F32),
            pltpu.VMEM((tile, wdt), F32),
            pltpu.VMEM((tile, wdt), F32),
            pltpu.VMEM((tile, LANE), F32),
            pltpu.VMEM((8, tile), F32),
            pltpu.VMEM((tile, LANE), F32),
            pltpu.VMEM((DN_HEADS, HEAD_DIM, HEAD_DIM), F32),
        ],
        compiler_params=_params("parallel", "arbitrary"),
        name="deltanet",
    )(z3, z3, z3, z3, z3, conv_w, alog_pad, dtb_pad, norm_w)


def _s5_prep_kernel(are_ref, aim_ref, ldt_ref, bre_ref, bim_ref, cre_ref, cim_ref,
                    apr_ref, api_ref, bbr_ref, bbi_ref, cbr_ref, cbi_ref):
    lr = are_ref[...]
    li = aim_ref[...]
    dt = jnp.exp(ldt_ref[...])
    step = (lax.broadcasted_iota(jnp.int32, (S5_SEG, 1), 0) + 1).astype(F32)
    mag = jnp.exp(step * (lr * dt))
    ang = step * (li * dt)
    apr_ref[...] = mag * jnp.cos(ang)
    api_ref[...] = mag * jnp.sin(ang)
    m1 = jnp.exp(lr * dt)
    nr = m1 * jnp.cos(li * dt) - 1.0
    ni = m1 * jnp.sin(li * dt)
    den = lr * lr + li * li
    cr = (nr * lr + ni * li) / den
    ci = (ni * lr - nr * li) / den
    br = bre_ref[...]
    bi = bim_ref[...]
    mats = ((bbr_ref, cr * br - ci * bi), (bbi_ref, cr * bi + ci * br), (cbr_ref, cre_ref[...]), (cbi_ref, cim_ref[...]))
    lane_group = lax.broadcasted_iota(jnp.int32, (S5_GROUP_CH, S5_LANES), 1) // S5_STATE
    for g in range(S5_GROUPS):
        for out_ref, val in mats:
            out_ref[g * S5_GROUP_CH:(g + 1) * S5_GROUP_CH, :] = jnp.where(lane_group == g, val, 0.0).astype(BF16)


def _s5_prep(a_re, a_im, log_dt, b_re, b_im, c_re, c_im):
    depth = a_re.shape[0]
    row = lambda a: a.reshape(depth, 1, S5_LANES)
    ldt = jnp.repeat(log_dt, S5_STATE, axis=-1).reshape(depth, 1, S5_LANES)
    b_rows = lambda a: a.reshape(depth, S5_LANES, S5_GROUP_CH).transpose(0, 2, 1)
    c_rows = lambda a: a.transpose(0, 2, 1, 3).reshape(depth, S5_GROUP_CH, S5_LANES)
    shp = jax.ShapeDtypeStruct
    per_layer = lambda rows: pl.BlockSpec((None, rows, S5_LANES), lambda l: (l, 0, 0))
    chan = per_layer(S5_GROUP_CH)
    return pl.pallas_call(
        _s5_prep_kernel,
        grid=(depth,),
        in_specs=[per_layer(1), per_layer(1), per_layer(1), chan, chan, chan, chan],
        out_specs=[per_layer(S5_SEG), per_layer(S5_SEG)] + [per_layer(S5_WIDTH)] * 4,
        out_shape=(shp((depth, S5_SEG, S5_LANES), F32),) * 2 + (shp((depth, S5_WIDTH, S5_LANES), BF16),) * 4,
        name="s5_prep",
    )(row(a_re), row(a_im), ldt, b_rows(b_re), b_rows(b_im), c_rows(c_re), c_rows(c_im))


def _s5_kernel(u0_ref, u1_ref, u2_ref, u3_ref, bre_ref, bim_ref, cre_ref, cim_ref, apr_ref, api_ref, d_ref,
               gw_ref, gb_ref, o_ref, up_ref, xr_ref, xi_ref, xrb_ref, xib_ref, cr_ref, ci_ref, st_ref, y_ref):
    t = pl.program_id(1)
    tile = u0_ref.shape[0]
    seg = tile // 8
    strip = 1024

    @pl.when(t == 0)
    def _():
        st_ref[...] = jnp.zeros_like(st_ref)

    for j, u_ref in enumerate((u0_ref, u1_ref, u2_ref, u3_ref)):
        for k in range(seg):
            up_ref[8 * k:8 * k + 8, j * LANE:(j + 1) * LANE] = u_ref[pl.ds(k, 8, stride=seg), :]
    hw, hl = S5_WIDTH // 2, S5_LANES // 2
    for half in range(2):
        ub = up_ref[:, half * hw:(half + 1) * hw].astype(BF16)
        rows, cols = slice(half * hw, (half + 1) * hw), slice(half * hl, (half + 1) * hl)
        xr_ref[:, cols] = jnp.dot(ub, bre_ref[rows, cols], preferred_element_type=F32)
        xi_ref[:, cols] = jnp.dot(ub, bim_ref[rows, cols], preferred_element_type=F32)

    for s0 in range(0, S5_LANES, strip):
        lanes = slice(s0, s0 + strip)
        ar = jnp.broadcast_to(apr_ref[0:1, lanes], (8, strip))
        ai = jnp.broadcast_to(api_ref[0:1, lanes], (8, strip))

        def scan(k, carry):
            pr, pi = carry
            r0 = pl.multiple_of(k * 8, 8)
            nr = ar * pr - ai * pi + xr_ref[pl.ds(r0, 8), lanes]
            ni = ar * pi + ai * pr + xi_ref[pl.ds(r0, 8), lanes]
            xr_ref[pl.ds(r0, 8), lanes] = nr
            xi_ref[pl.ds(r0, 8), lanes] = ni
            return nr, ni

        zero = jnp.zeros((8, strip), F32)
        fr, fi = lax.fori_loop(0, seg, scan, (zero, zero), unroll=4)

        a64r = apr_ref[seg - 1:seg, lanes]
        a64i = api_ref[seg - 1:seg, lanes]
        c_r = st_ref[0:1, lanes]
        c_i = st_ref[1:2, lanes]
        for r in range(8):
            cr_ref[r:r + 1, lanes] = c_r
            ci_ref[r:r + 1, lanes] = c_i
            n_r = a64r * c_r - a64i * c_i + fr[r:r + 1]
            n_i = a64r * c_i + a64i * c_r + fi[r:r + 1]
            c_r, c_i = n_r, n_i
        st_ref[0:1, lanes] = c_r
        st_ref[1:2, lanes] = c_i

        cin_r = jnp.concatenate([cr_ref[:, lanes]] * 2, axis=0)
        cin_i = jnp.concatenate([ci_ref[:, lanes]] * 2, axis=0)

        def fix(k2, carry):
            r0 = pl.multiple_of(k2 * 16, 16)
            pw = lambda ref, k: jnp.broadcast_to(ref[pl.ds(k, 1), lanes], (8, strip))
            pr = jnp.concatenate([pw(apr_ref, 2 * k2), pw(apr_ref, 2 * k2 + 1)], axis=0)
            pi = jnp.concatenate([pw(api_ref, 2 * k2), pw(api_ref, 2 * k2 + 1)], axis=0)
            xrb_ref[pl.ds(r0, 16), lanes] = (xr_ref[pl.ds(r0, 16), lanes] + pr * cin_r - pi * cin_i).astype(BF16)
            xib_ref[pl.ds(r0, 16), lanes] = (xi_ref[pl.ds(r0, 16), lanes] + pr * cin_i + pi * cin_r).astype(BF16)
            return carry

        lax.fori_loop(0, seg // 2, fix, 0, unroll=2)

    ys = []
    for half in range(2):
        states, chans = slice(half * hl, (half + 1) * hl), slice(half * hw, (half + 1) * hw)
        ys.append(_dot_nt(xrb_ref[:, states], cre_ref[chans, states])
                  - _dot_nt(xib_ref[:, states], cim_ref[chans, states]))
    y = jnp.concatenate(ys, axis=1)
    y = y + d_ref[...] * up_ref[...]
    y = 0.5 * y * (1.0 + jnp.tanh(math.sqrt(2.0 / math.pi) * (y + 0.044715 * (y * y * y))))
    gate = jnp.dot(y.astype(BF16), gw_ref[...], preferred_element_type=F32) + gb_ref[...]
    y_ref[...] = y * _sigmoid(gate)
    for j in range(S5_WIDTH // LANE):
        for k in range(seg):
            o_ref[j, pl.ds(k, 8, stride=seg), :] = y_ref[8 * k:8 * k + 8, j * LANE:(j + 1) * LANE]


def _s5(z3, b_re, b_im, c_re, c_im, ap_re, ap_im, d_skip, glu_w, glu_b, layer, *, tile):
    b, s, _ = z3.shape
    nblk = S5_WIDTH // LANE
    ublk = (ATTN_WIDTH + 2 * ATTN_KV_WIDTH + 4 * DN_WIDTH) // LANE
    const = lambda a: pl.BlockSpec(a.shape, lambda bi, t: (0, 0))
    layered = lambda a: pl.BlockSpec((None,) + a.shape[1:], lambda bi, t: (layer, 0, 0))
    ucol = lambda j: pl.BlockSpec((None, tile, LANE), lambda bi, t: (bi, t, ublk + j))
    return pl.pallas_call(
        _s5_kernel,
        grid=(b, s // tile),
        in_specs=[
            ucol(0), ucol(1), ucol(2), ucol(3),
            layered(b_re), layered(b_im), layered(c_re), layered(c_im), layered(ap_re), layered(ap_im),
            const(d_skip), layered(glu_w), const(glu_b),
        ],
        out_specs=pl.BlockSpec((nblk, None, tile, LANE), lambda bi, t: (0, bi, t, 0)),
        out_shape=jax.ShapeDtypeStruct((nblk, b, s, LANE), F32),
        scratch_shapes=[
            pltpu.VMEM((tile, S5_WIDTH), F32),
            pltpu.VMEM((tile, S5_LANES), F32),
            pltpu.VMEM((tile, S5_LANES), F32),
            pltpu.VMEM((tile, S5_LANES), BF16),
            pltpu.VMEM((tile, S5_LANES), BF16),
            pltpu.VMEM((8, S5_LANES), F32),
            pltpu.VMEM((8, S5_LANES), F32),
            pltpu.VMEM((8, S5_LANES), F32),
            pltpu.VMEM((tile, S5_WIDTH), F32),
        ],
        compiler_params=_params("parallel", "arbitrary"),
        name="s5",
    )(z3, z3, z3, z3, b_re, b_im, c_re, c_im, ap_re, ap_im, d_skip, glu_w, glu_b)


def _rope_tables(seq):
    half = HEAD_DIM // 2
    inv_freq = ROPE_THETA ** (-jnp.arange(half, dtype=F32) / half)
    ang = jnp.arange(seq, dtype=F32)[:, None] * inv_freq[None, :]
    cos, sin = jnp.cos(ang), jnp.sin(ang)
    return jnp.concatenate([cos, cos], axis=-1), jnp.concatenate([-sin, sin], axis=-1)


def _split_w_in(w_in):
    s5_src = SRC_BA + 2 * DN_HEADS
    pad = jnp.zeros(w_in.shape[:-1] + (LANE - 2 * DN_HEADS,), BF16)
    w_b = w_in.astype(BF16)
    w_ba = jnp.concatenate([w_b[..., SRC_BA:s5_src], pad], axis=-1)
    return w_b, w_b[..., s5_src:], w_ba


def _lane_pad(v, offset):
    return jnp.zeros((v.shape[0], 1, LANE), F32).at[:, 0, offset:offset + v.shape[1]].set(v)


def kernel(x, ff1_norm_pre, ff1_w_gate, ff1_w_up, ff1_w_down, ff1_norm_post, mix_norm_pre, w_in,
           attn_sinks, dn_conv_w, dn_a_log, dn_dt_bias, dn_norm_w, s5_a_re, s5_a_im, s5_log_dt,
           s5_b_re, s5_b_im, s5_c_re, s5_c_im, s5_d, s5_glu_w, s5_glu_b, w_out, mix_norm_post,
           ff2_norm_pre, ff2_w_gate, ff2_w_up, ff2_w_down, ff2_norm_post):
    b, s, d = x.shape
    depth = w_in.shape[0]
    n = b * s
    tm = min(512, n)
    tm_big = min(1024, n)
    tf = 512
    tile = min(SEQ_TILE, s)

    bf = lambda a: a.astype(BF16)
    ff1 = (ff1_w_gate, ff1_w_up, ff1_w_down)
    ff2 = (ff2_w_gate, ff2_w_up, ff2_w_down)
    w_in_parts = _split_w_in(w_in)
    w_out_b = bf(w_out)
    glu_w_b = bf(s5_glu_w)
    ap_re, ap_im, *s5_mats = _s5_prep(s5_a_re, s5_a_im, s5_log_dt, s5_b_re, s5_b_im, s5_c_re, s5_c_im)
    cc, ss = _rope_tables(s)
    alog_pad = _lane_pad(dn_a_log, DN_HEADS)
    dtb_pad = _lane_pad(dn_dt_bias, DN_HEADS)
    row = lambda a, l: a[l].reshape(1, -1)

    xf = x.reshape(n, d)
    for l in range(depth):
        xf = _ffn(xf, row(ff1_norm_pre, l), *ff1, row(ff1_norm_post, l), l, tm=tm_big, tf=tf, tf_head=tf // 2)

        z3 = _inproj(xf, row(mix_norm_pre, l), *w_in_parts, l, tm=tm).reshape(b, s, Z_WIDTH)
        y_attn = _attention(z3, attn_sinks[l], cc, ss, tile=tile)
        y_dn = _deltanet(z3, dn_conv_w[l], alog_pad[l], dtb_pad[l], row(dn_norm_w, l), tile=min(DN_TILE, s))
        y_s5 = _s5(z3, *s5_mats, ap_re, ap_im, row(s5_d, l), glu_w_b, row(s5_glu_b, l), l, tile=min(S5_TILE, s))
        xf = _outproj(xf, y_attn.reshape(n, -1), y_dn.reshape(n, -1), y_s5.reshape(-1, n, LANE),
                      w_out_b, row(mix_norm_post, l), l, tm=tm)

        xf = _ffn(xf, row(ff2_norm_pre, l), *ff2, row(ff2_norm_post, l), l, tm=tm_big, tf=tf, tf_head=tf // 2)
    return xf.reshape(b, s, d)
```

```python
import functools
import math

import jax
import jax.numpy as jnp
from jax import lax
from jax.experimental import pallas as pl
from jax.experimental.pallas import tpu as pltpu

F32 = jnp.float32
BF16 = jnp.bfloat16
HIGHEST = lax.Precision.HIGHEST

NORM_EPS = 1e-6
FFN_RES_WEIGHT = 0.5
ROPE_THETA = 10000.0

HEAD_DIM = 128
WINDOW = 128
ATTN_HEADS = 8
ATTN_KV_HEADS = 2
ATTN_GROUP = ATTN_HEADS // ATTN_KV_HEADS
DN_HEADS = 4
DN_CONV = 4
DN_CHUNK = 64
S5_GROUPS = 32
S5_GROUP_CH = 16
S5_STATE = 64
ATTN_WIDTH = ATTN_HEADS * HEAD_DIM
ATTN_KV_WIDTH = ATTN_KV_HEADS * HEAD_DIM
DN_WIDTH = DN_HEADS * HEAD_DIM
S5_WIDTH = S5_GROUPS * S5_GROUP_CH
S5_LANES = S5_GROUPS * S5_STATE
MIX_WIDTH = ATTN_WIDTH + DN_WIDTH + S5_WIDTH

LANE = 128
Z_WIDTH = ATTN_WIDTH + 2 * ATTN_KV_WIDTH + 4 * DN_WIDTH + S5_WIDTH + LANE
Z_BA_BLOCK = (Z_WIDTH - LANE) // LANE
SRC_BA = ATTN_WIDTH + 2 * ATTN_KV_WIDTH + 4 * DN_WIDTH

SEQ_TILE = 512
DN_TILE = 1024
DN_GROUP_CHUNKS = 4
S5_TILE = 1024
S5_SEG = S5_TILE // 8
PROJ_ROWS = 512
FFN_ROWS = 1024
FFN_COLS = 512
FFN_DOWN_COLS = 512
FFN_SUB_COLS = 256
NORM_ROWS = 128
VMEM_LIMIT = 56 * 1024 * 1024


def _params(*sem):
    return pltpu.CompilerParams(dimension_semantics=sem, vmem_limit_bytes=VMEM_LIMIT)


def _rms(x, gain):
    return x * lax.rsqrt(jnp.mean(x * x, axis=-1, keepdims=True) + NORM_EPS) * gain


def _sigmoid(x):
    return 0.5 * jnp.tanh(0.5 * x) + 0.5


def _ffn_kernel(*refs, cast_weights):
    if cast_weights:
        x_ref, gpre_ref, wg_ref, wu_ref, wd_ref, gpost_ref, o_ref, wgb_ref, wub_ref, wdb_ref, h_ref = refs
        wgb_ref[...] = wg_ref[...].astype(BF16)
        wub_ref[...] = wu_ref[...].astype(BF16)
        wdb_ref[...] = wd_ref[...].astype(BF16)
        wg_ref, wu_ref, wd_ref = wgb_ref, wub_ref, wdb_ref
    else:
        x_ref, gpre_ref, wg_ref, wu_ref, wd_ref, gpost_ref, o_ref, h_ref = refs
    j = pl.program_id(1)

    row_blocks = [slice(r0, r0 + NORM_ROWS) for r0 in range(0, x_ref.shape[0], NORM_ROWS)]

    @pl.when(j == 0)
    def _():
        for rows in row_blocks:
            h_ref[rows, :] = _rms(x_ref[rows, :], gpre_ref[...]).astype(BF16)
        o_ref[...] = jnp.zeros_like(o_ref)

    h = h_ref[...]
    tf = wg_ref.shape[1]
    sub = min(FFN_SUB_COLS, tf)
    chunk = min(FFN_DOWN_COLS, o_ref.shape[1])
    for f0 in range(0, tf, sub):
        fs = slice(f0, f0 + sub)
        g = jnp.dot(h, wg_ref[:, fs], preferred_element_type=F32)
        u = jnp.dot(h, wu_ref[:, fs], preferred_element_type=F32)
        a = (g * _sigmoid(g) * u).astype(BF16)
        for c0 in range(0, o_ref.shape[1], chunk):
            cols = slice(c0, c0 + chunk)
            o_ref[:, cols] += jnp.dot(a, wd_ref[fs, cols], preferred_element_type=F32)

    @pl.when(j == pl.num_programs(1) - 1)
    def _():
        for rows in row_blocks:
            o_ref[rows, :] = x_ref[rows, :] + FFN_RES_WEIGHT * _rms(o_ref[rows, :], gpost_ref[...])


def _ffn(x, gpre, wg, wu, wd, gpost, layer, *, tm, tf, tf_head):
    n, d = x.shape
    f = wg.shape[2]
    once = pl.Buffered(1)
    vec = pl.BlockSpec((1, d), lambda i, j: (0, 0))
    shp = jax.ShapeDtypeStruct
    y, wgb, wub, wdb = pl.pallas_call(
        functools.partial(_ffn_kernel, cast_weights=True),
        grid=(1, f // tf_head),
        in_specs=[
            pl.BlockSpec((tm, d), lambda i, j: (0, 0), pipeline_mode=once),
            vec,
            pl.BlockSpec((None, d, tf_head), lambda i, j: (layer, 0, j)),
            pl.BlockSpec((None, d, tf_head), lambda i, j: (layer, 0, j)),
            pl.BlockSpec((None, tf_head, d), lambda i, j: (layer, j, 0)),
            vec,
        ],
        out_specs=[
            pl.BlockSpec((tm, d), lambda i, j: (0, 0), pipeline_mode=once),
            pl.BlockSpec((d, tf_head), lambda i, j: (0, j)),
            pl.BlockSpec((d, tf_head), lambda i, j: (0, j)),
            pl.BlockSpec((tf_head, d), lambda i, j: (j, 0)),
        ],
        out_shape=(shp((n, d), F32), shp((d, f), BF16), shp((d, f), BF16), shp((f, d), BF16)),
        input_output_aliases={0: 0},
        scratch_shapes=[pltpu.VMEM((tm, d), BF16)],
        compiler_params=_params("arbitrary", "arbitrary"),
        name="ffn_head",
    )(x, gpre, wg, wu, wd, gpost)
    return pl.pallas_call(
        functools.partial(_ffn_kernel, cast_weights=False),
        grid=(n // tm - 1, f // tf),
        in_specs=[
            pl.BlockSpec((tm, d), lambda i, j: (i + 1, 0)),
            vec,
            pl.BlockSpec((d, tf), lambda i, j: (0, j)),
            pl.BlockSpec((d, tf), lambda i, j: (0, j)),
            pl.BlockSpec((tf, d), lambda i, j: (j, 0)),
            vec,
        ],
        out_specs=pl.BlockSpec((tm, d), lambda i, j: (i + 1, 0)),
        out_shape=shp((n, d), F32),
        input_output_aliases={0: 0},
        scratch_shapes=[pltpu.VMEM((tm, d), BF16)],
        compiler_params=_params("parallel", "arbitrary"),
        name="ffn",
    )(y, gpre, wgb, wub, wdb, gpost)


def _inproj_kernel(x_ref, g_ref, wm_ref, ws_ref, wb_ref, o_ref):
    h = _rms(x_ref[...], g_ref[...]).astype(BF16)
    s5_at = wm_ref.shape[1]
    ba_at = s5_at + ws_ref.shape[1]
    o_ref[:, :s5_at] = jnp.dot(h, wm_ref[...], preferred_element_type=F32)
    o_ref[:, s5_at:ba_at] = jnp.dot(h, ws_ref[...], preferred_element_type=F32)
    o_ref[:, ba_at:] = jnp.dot(h, wb_ref[...], preferred_element_type=F32)


def _inproj(x, gain, w_all, w_s5, w_ba, layer, *, tm):
    n, d = x.shape
    resident = lambda shape: pl.BlockSpec((None,) + shape, lambda i: (layer, 0, 0), pipeline_mode=pl.Buffered(1))
    return pl.pallas_call(
        _inproj_kernel,
        grid=(n // tm,),
        in_specs=[
            pl.BlockSpec((tm, d), lambda i: (i, 0)),
            pl.BlockSpec((1, d), lambda i: (0, 0)),
            resident((d, SRC_BA)), resident((d, S5_WIDTH)), resident((d, LANE)),
        ],
        out_specs=pl.BlockSpec((tm, Z_WIDTH), lambda i: (i, 0)),
        out_shape=jax.ShapeDtypeStruct((n, Z_WIDTH), F32),
        compiler_params=_params("parallel"),
        name="inproj",
    )(x, gain, w_all, w_s5, w_ba)


def _outproj_kernel(x_ref, ya_ref, yd_ref, ys_ref, w_ref, g_ref, o_ref):
    ys = [ys_ref[j].astype(BF16) for j in range(ys_ref.shape[0])]
    y = jnp.concatenate([ya_ref[...], yd_ref[...]] + ys, axis=-1)
    mixed = jnp.dot(y, w_ref[...], preferred_element_type=F32)
    o_ref[...] = x_ref[...] + _rms(mixed, g_ref[...])


def _outproj(x, ya, yd, ys, w, gain, layer, *, tm):
    n, d = x.shape
    return pl.pallas_call(
        _outproj_kernel,
        grid=(n // tm,),
        in_specs=[
            pl.BlockSpec((tm, d), lambda i: (i, 0)),
            pl.BlockSpec((tm, ya.shape[1]), lambda i: (i, 0)),
            pl.BlockSpec((tm, yd.shape[1]), lambda i: (i, 0)),
            pl.BlockSpec((ys.shape[0], tm, ys.shape[2]), lambda i: (0, i, 0)),
            pl.BlockSpec((None,) + w.shape[1:], lambda i: (layer, 0, 0)),
            pl.BlockSpec((1, d), lambda i: (0, 0)),
        ],
        out_specs=pl.BlockSpec((tm, d), lambda i: (i, 0)),
        out_shape=jax.ShapeDtypeStruct((n, d), F32),
        compiler_params=_params("parallel"),
        name="outproj",
    )(x, ya, yd, ys, w, gain)


def _rope(x, cc, ss):
    return x * cc + pltpu.roll(x, HEAD_DIM // 2, axis=1) * ss


def _attn_kernel(sink_ref, q_ref, kp_ref, kc_ref, vp_ref, vc_ref, ccp_ref, ccc_ref, ssp_ref, ssc_ref, o_ref):
    h = pl.program_id(1)
    t = pl.program_id(2)
    w = WINDOW
    grp = ATTN_GROUP
    nblk = q_ref.shape[0] // w
    ccc = ccc_ref[...]
    ssc = ssc_ref[...]
    q = q_ref[...]
    scale = HEAD_DIM ** -0.5
    qr = [(_rope(q[:, g * HEAD_DIM:(g + 1) * HEAD_DIM], ccc, ssc) * scale).astype(BF16) for g in range(grp)]
    kk = jnp.concatenate([_rope(kp_ref[...], ccp_ref[...], ssp_ref[...]), _rope(kc_ref[...], ccc, ssc)],
                         axis=0).astype(BF16)
    vv = jnp.concatenate([vp_ref[...], vc_ref[...]], axis=0).astype(BF16)
    qi = lax.broadcasted_iota(jnp.int32, (w, 2 * w), 0) + w
    kj = lax.broadcasted_iota(jnp.int32, (w, 2 * w), 1)
    rel = qi - kj
    band = (rel >= 0) & (rel < w)
    first = band & ((kj >= w) | (t > 0))
    row_head = lax.broadcasted_iota(jnp.int32, (grp * w, 1), 0) // w
    sink = jnp.zeros((grp * w, 1), F32)
    for g in range(grp):
        sink = jnp.where(row_head == g, sink_ref[h * grp + g], sink)
    scores = []
    for blk in range(nblk):
        qs = jnp.concatenate([qr[g][blk * w:(blk + 1) * w] for g in range(grp)], axis=0)
        scores.append(_dot_nt(qs, kk[blk * w:(blk + 2) * w]))
    probs = []
    for blk in range(nblk):
        mask = jnp.concatenate([first if blk == 0 else band] * grp, axis=0)
        sc = jnp.where(mask, scores[blk], -jnp.inf)
        m = jnp.maximum(jnp.max(sc, axis=-1, keepdims=True), sink)
        p = jnp.exp(sc - m)
        denom = jnp.sum(p, axis=-1, keepdims=True) + jnp.exp(sink - m)
        probs.append((p / denom).astype(BF16))
    outs = [jnp.dot(probs[blk], vv[blk * w:(blk + 2) * w], preferred_element_type=F32) for blk in range(nblk)]
    for blk in range(nblk):
        o_ref[blk * w:(blk + 1) * w, :] = jnp.concatenate(
            [outs[blk][g * w:(g + 1) * w] for g in range(grp)], axis=1).astype(o_ref.dtype)


def _attention(z3, sinks, cc, ss, *, tile):
    b, s, _ = z3.shape
    w = WINDOW
    nblk = tile // w
    qw = ATTN_GROUP * HEAD_DIM
    kblk = ATTN_WIDTH // HEAD_DIM
    vblk = (ATTN_WIDTH + ATTN_KV_WIDTH) // HEAD_DIM
    prev = lambda t: jnp.maximum(t * nblk - 1, 0)
    return pl.pallas_call(
        _attn_kernel,
        grid=(b, ATTN_KV_HEADS, s // tile),
        in_specs=[
            pl.BlockSpec(memory_space=pltpu.SMEM),
            pl.BlockSpec((None, tile, qw), lambda bi, h, t: (bi, t, h)),
            pl.BlockSpec((None, w, HEAD_DIM), lambda bi, h, t: (bi, prev(t), kblk + h)),
            pl.BlockSpec((None, tile, HEAD_DIM), lambda bi, h, t: (bi, t, kblk + h)),
            pl.BlockSpec((None, w, HEAD_DIM), lambda bi, h, t: (bi, prev(t), vblk + h)),
            pl.BlockSpec((None, tile, HEAD_DIM), lambda bi, h, t: (bi, t, vblk + h)),
            pl.BlockSpec((w, HEAD_DIM), lambda bi, h, t: (prev(t), 0)),
            pl.BlockSpec((tile, HEAD_DIM), lambda bi, h, t: (t, 0)),
            pl.BlockSpec((w, HEAD_DIM), lambda bi, h, t: (prev(t), 0)),
            pl.BlockSpec((tile, HEAD_DIM), lambda bi, h, t: (t, 0)),
        ],
        out_specs=pl.BlockSpec((None, tile, qw), lambda bi, h, t: (bi, t, h)),
        out_shape=jax.ShapeDtypeStruct((b, s, ATTN_WIDTH), BF16),
        compiler_params=_params("parallel", "parallel", "arbitrary"),
        name="swa",
    )(sinks, z3, z3, z3, z3, z3, cc, cc, ss, ss)


def _silu(x):
    return x * _sigmoid(x)


def _softplus(x):
    return jnp.maximum(x, 0.0) + jnp.log(1.0 + jnp.exp(-jnp.abs(x)))


def _dot_nt(a, b, precision=None):
    return lax.dot_general(a, b, (((1,), (1,)), ((), ())), preferred_element_type=F32, precision=precision)


def _dot_tn(a, b, precision=None):
    return lax.dot_general(a, b, (((0,), (0,)), ((), ())), preferred_element_type=F32, precision=precision)


def _dn_kernel(q_ref, k_ref, v_ref, zg_ref, ba_ref, cw_ref, alog_ref, dtb_ref, nw_ref, o_ref,
               ext_ref, qn_ref, kn_ref, vn_ref, gc_ref, gct_ref, beta_ref, s_ref):
    t = pl.program_id(1)
    tile = q_ref.shape[0]
    c = DN_CHUNK
    d = HEAD_DIM
    nh = DN_HEADS
    wdt = nh * d

    @pl.when(t == 0)
    def _():
        ext_ref[0:8, :] = jnp.zeros((8, 3 * wdt), F32)
        s_ref[...] = jnp.zeros_like(s_ref)

    ext_ref[8:8 + tile, 0:wdt] = q_ref[...]
    ext_ref[8:8 + tile, wdt:2 * wdt] = k_ref[...]
    ext_ref[8:8 + tile, 2 * wdt:3 * wdt] = v_ref[...]
    ext = ext_ref[...]
    conv = ext[8:] * cw_ref[DN_CONV - 1:DN_CONV, :]
    for back in range(1, DN_CONV):
        conv = conv + pltpu.roll(ext, back, axis=0)[8:] * cw_ref[DN_CONV - 1 - back:DN_CONV - back, :]
    ext_ref[0:8, :] = ext[tile:tile + 8]
    qkv = _silu(conv)
    for hd in range(nh):
        qh = qkv[:, hd * d:(hd + 1) * d]
        kh = qkv[:, wdt + hd * d:wdt + (hd + 1) * d]
        qn_ref[:, hd * d:(hd + 1) * d] = qh * lax.rsqrt(jnp.sum(qh * qh, axis=-1, keepdims=True) + NORM_EPS) * (d ** -0.5)
        kn_ref[:, hd * d:(hd + 1) * d] = kh * lax.rsqrt(jnp.sum(kh * kh, axis=-1, keepdims=True) + NORM_EPS)
    vn_ref[...] = qkv[:, 2 * wdt:3 * wdt]
    ba = ba_ref[...]
    beta_ref[...] = _sigmoid(ba)
    g = -jnp.exp(alog_ref[...]) * _softplus(ba + dtb_ref[...])
    ti = lax.broadcasted_iota(jnp.int32, (c, c), 0)
    tj = lax.broadcasted_iota(jnp.int32, (c, c), 1)
    csum = (ti >= tj).astype(F32)
    gc_all = jnp.concatenate([jnp.dot(csum, g[ci * c:(ci + 1) * c], preferred_element_type=F32, precision=HIGHEST)
                              for ci in range(tile // c)], axis=0)
    gc_ref[...] = gc_all
    gct_ref[...] = gc_all.T[0:8, :]

    row = lax.broadcasted_iota(jnp.int32, (c, c), 0)
    col = lax.broadcasted_iota(jnp.int32, (c, c), 1)
    causal = row >= col
    strict = row > col
    eye = (row == col).astype(F32)
    nw = nw_ref[...]

    def join_mask(half):
        return (row // (2 * half) == col // (2 * half)) & (row % (2 * half) >= half) & (col % (2 * half) < half)

    def local_stages(chunks, items):
        for ci in chunks:
            r0 = ci * c
            for hd in range(nh):
                lanes = slice(hd * d, (hd + 1) * d)
                q = qn_ref[r0:r0 + c, lanes]
                k = kn_ref[r0:r0 + c, lanes]
                beta = beta_ref[r0:r0 + c, hd:hd + 1]
                gcol = gc_ref[r0:r0 + c, nh + hd:nh + hd + 1]
                grow = gct_ref[nh + hd:nh + hd + 1, r0:r0 + c]
                glast = gcol[c - 1:c, :]
                decay = jnp.exp(jnp.where(causal, gcol - grow, -jnp.inf))
                eg = jnp.exp(gcol)
                kb = k * beta
                items.append(dict(ci=ci, hd=hd, r0=r0, lanes=lanes, decay=decay, kbf=k.astype(BF16),
                                  qbf=q.astype(BF16), kb_bf=kb.astype(BF16),
                                  rhs=jnp.concatenate([vn_ref[r0:r0 + c, lanes] * beta, kb * eg], axis=1).astype(BF16),
                                  qd=(q * eg).astype(BF16), k_dec=(k * jnp.exp(glast - gcol)).astype(BF16),
                                  egl=jnp.exp(glast)))
        for it in items:
            it["kk"] = _dot_nt(it["kb_bf"], it["kbf"])
            it["qk"] = _dot_nt(it["qbf"], it["kbf"])
        yield
        for it in items:
            it["nmat"] = jnp.where(strict, it["kk"] * it["decay"], 0.0)
            it["inv"] = eye - jnp.where(join_mask(1), it["nmat"], 0.0)
            it["attn"] = (it["qk"] * it["decay"]).astype(BF16)
        half = 2
        while half < c:
            for it in items:
                it["inv_bf"] = it["inv"].astype(BF16)
                join = jnp.where(join_mask(half), it["nmat"], 0.0).astype(BF16)
                it["bt"] = jnp.dot(join, it["inv_bf"], preferred_element_type=F32).astype(BF16)
            yield
            for it in items:
                it["upd"] = jnp.dot(it["inv_bf"], it["bt"], preferred_element_type=F32)
            yield
            for it in items:
                it["inv"] = it["inv"] - it["upd"]
            half *= 2
        for it in items:
            uw = jnp.dot(it["inv"].astype(BF16), it["rhs"], preferred_element_type=F32)
            it["u"] = uw[:, :d]
            it["wq"] = jnp.concatenate([uw[:, d:].astype(BF16), it["qd"]], axis=0)
        yield

    def sweep_stages(items, states):
        for ci in sorted({it["ci"] for it in items}):
            group = [it for it in items if it["ci"] == ci]
            for it in group:
                it["ws_qs"] = jnp.dot(it["wq"], states[it["hd"]].astype(BF16), preferred_element_type=F32)
            yield
            for it in group:
                it["v_new"] = (it["u"] - it["ws_qs"][:c]).astype(BF16)
            for it in group:
                it["av"] = jnp.dot(it["attn"], it["v_new"], preferred_element_type=F32)
                it["kv"] = _dot_tn(it["k_dec"], it["v_new"])
            yield
            for it in group:
                states[it["hd"]] = states[it["hd"]] * it["egl"] + it["kv"]
                o = it["ws_qs"][c:] + it["av"]
                o = o * lax.rsqrt(jnp.mean(o * o, axis=-1, keepdims=True) + NORM_EPS) * nw
                o = o * _silu(zg_ref[it["r0"]:it["r0"] + c, it["lanes"]])
                o_ref[it["r0"]:it["r0"] + c, it["lanes"]] = o.astype(o_ref.dtype)

    def interleave(*gens):
        live = list(gens)
        while live:
            for gen in list(live):
                if next(gen, "end") == "end":
                    live.remove(gen)

    states = [s_ref[hd] for hd in range(nh)]
    group_chunks = DN_GROUP_CHUNKS
    groups = [list(range(g0, g0 + group_chunks)) for g0 in range(0, tile // c, group_chunks)]
    ready = []
    interleave(local_stages(groups[0], ready))
    for nxt in groups[1:]:
        upcoming = []
        interleave(local_stages(nxt, upcoming), sweep_stages(ready, states))
        ready = upcoming
    interleave(sweep_stages(ready, states))
    for hd in range(nh):
        s_ref[hd] = states[hd]


def _deltanet(z3, conv_w, alog_pad, dtb_pad, norm_w, *, tile):
    b, s, _ = z3.shape
    wdt = DN_WIDTH
    qblk = (ATTN_WIDTH + 2 * ATTN_KV_WIDTH) // wdt
    seq = lambda off: pl.BlockSpec((None, tile, wdt), lambda bi, t: (bi, t, qblk + off))
    const = lambda shape: pl.BlockSpec(shape, lambda bi, t: (0, 0))
    return pl.pallas_call(
        _dn_kernel,
        grid=(b, s // tile),
        in_specs=[
            seq(0), seq(1), seq(2), seq(3),
            pl.BlockSpec((None, tile, LANE), lambda bi, t: (bi, t, Z_BA_BLOCK)),
            const(conv_w.shape), const((1, LANE)), const((1, LANE)), const((1, HEAD_DIM)),
        ],
        out_specs=pl.BlockSpec((None, tile, wdt), lambda bi, t: (bi, t, 0)),
        out_shape=jax.ShapeDtypeStruct((b, s, wdt), BF16),
        scratch_shapes=[
            pltpu.VMEM((tile + 8, 3 * wdt), F32),
            pltpu.VMEM((tile, wdt), F32),
            pltpu.VMEM((tile, wdt), F32),
            pltpu.VMEM((tile, wdt), F32),
            pltpu.VMEM((tile, LANE), F32),
            pltpu.VMEM((8, tile), F32),
            pltpu.VMEM((tile, LANE), F32),
            pltpu.VMEM((DN_HEADS, HEAD_DIM, HEAD_DIM), F32),
        ],
        compiler_params=_params("parallel", "arbitrary"),
        name="deltanet",
    )(z3, z3, z3, z3, z3, conv_w, alog_pad, dtb_pad, norm_w)


def _s5_prep_kernel(are_ref, aim_ref, ldt_ref, bre_ref, bim_ref, cre_ref, cim_ref,
                    apr_ref, api_ref, bbr_ref, bbi_ref, cbr_ref, cbi_ref):
    lr = are_ref[...]
    li = aim_ref[...]
    dt = jnp.exp(ldt_ref[...])
    step = (lax.broadcasted_iota(jnp.int32, (S5_SEG, 1), 0) + 1).astype(F32)
    mag = jnp.exp(step * (lr * dt))
    ang = step * (li * dt)
    apr_ref[...] = mag * jnp.cos(ang)
    api_ref[...] = mag * jnp.sin(ang)
    m1 = jnp.exp(lr * dt)
    nr = m1 * jnp.cos(li * dt) - 1.0
    ni = m1 * jnp.sin(li * dt)
    den = lr * lr + li * li
    cr = (nr * lr + ni * li) / den
    ci = (ni * lr - nr * li) / den
    br = bre_ref[...]
    bi = bim_ref[...]
    mats = ((bbr_ref, cr * br - ci * bi), (bbi_ref, cr * bi + ci * br), (cbr_ref, cre_ref[...]), (cbi_ref, cim_ref[...]))
    lane_group = lax.broadcasted_iota(jnp.int32, (S5_GROUP_CH, S5_LANES), 1) // S5_STATE
    for g in range(S5_GROUPS):
        for out_ref, val in mats:
            out_ref[g * S5_GROUP_CH:(g + 1) * S5_GROUP_CH, :] = jnp.where(lane_group == g, val, 0.0).astype(BF16)


def _s5_prep(a_re, a_im, log_dt, b_re, b_im, c_re, c_im):
    depth = a_re.shape[0]
    row = lambda a: a.reshape(depth, 1, S5_LANES)
    ldt = jnp.repeat(log_dt, S5_STATE, axis=-1).reshape(depth, 1, S5_LANES)
    b_rows = lambda a: a.reshape(depth, S5_LANES, S5_GROUP_CH).transpose(0, 2, 1)
    c_rows = lambda a: a.transpose(0, 2, 1, 3).reshape(depth, S5_GROUP_CH, S5_LANES)
    shp = jax.ShapeDtypeStruct
    per_layer = lambda rows: pl.BlockSpec((None, rows, S5_LANES), lambda l: (l, 0, 0))
    chan = per_layer(S5_GROUP_CH)
    return pl.pallas_call(
        _s5_prep_kernel,
        grid=(depth,),
        in_specs=[per_layer(1), per_layer(1), per_layer(1), chan, chan, chan, chan],
        out_specs=[per_layer(S5_SEG), per_layer(S5_SEG)] + [per_layer(S5_WIDTH)] * 4,
        out_shape=(shp((depth, S5_SEG, S5_LANES), F32),) * 2 + (shp((depth, S5_WIDTH, S5_LANES), BF16),) * 4,
        name="s5_prep",
    )(row(a_re), row(a_im), ldt, b_rows(b_re), b_rows(b_im), c_rows(c_re), c_rows(c_im))


def _s5_kernel(u0_ref, u1_ref, u2_ref, u3_ref, bre_ref, bim_ref, cre_ref, cim_ref, apr_ref, api_ref, d_ref,
               gw_ref, gb_ref, o_ref, up_ref, xr_ref, xi_ref, xrb_ref, xib_ref, cr_ref, ci_ref, st_ref, y_ref):
    t = pl.program_id(1)
    tile = u0_ref.shape[0]
    seg = tile // 8
    strip = 1024

    @pl.when(t == 0)
    def _():
        st_ref[...] = jnp.zeros_like(st_ref)

    for j, u_ref in enumerate((u0_ref, u1_ref, u2_ref, u3_ref)):
        for k in range(seg):
            up_ref[8 * k:8 * k + 8, j * LANE:(j + 1) * LANE] = u_ref[pl.ds(k, 8, stride=seg), :]
    hw, hl = S5_WIDTH // 2, S5_LANES // 2
    for half in range(2):
        ub = up_ref[:, half * hw:(half + 1) * hw].astype(BF16)
        rows, cols = slice(half * hw, (half + 1) * hw), slice(half * hl, (half + 1) * hl)
        xr_ref[:, cols] = jnp.dot(ub, bre_ref[rows, cols], preferred_element_type=F32)
        xi_ref[:, cols] = jnp.dot(ub, bim_ref[rows, cols], preferred_element_type=F32)

    for s0 in range(0, S5_LANES, strip):
        lanes = slice(s0, s0 + strip)
        ar = jnp.broadcast_to(apr_ref[0:1, lanes], (8, strip))
        ai = jnp.broadcast_to(api_ref[0:1, lanes], (8, strip))

        def scan(k, carry):
            pr, pi = carry
            r0 = pl.multiple_of(k * 8, 8)
            nr = ar * pr - ai * pi + xr_ref[pl.ds(r0, 8), lanes]
            ni = ar * pi + ai * pr + xi_ref[pl.ds(r0, 8), lanes]
            xr_ref[pl.ds(r0, 8), lanes] = nr
            xi_ref[pl.ds(r0, 8), lanes] = ni
            return nr, ni

        zero = jnp.zeros((8, strip), F32)
        fr, fi = lax.fori_loop(0, seg, scan, (zero, zero), unroll=4)

        a64r = apr_ref[seg - 1:seg, lanes]
        a64i = api_ref[seg - 1:seg, lanes]
        c_r = st_ref[0:1, lanes]
        c_i = st_ref[1:2, lanes]
        for r in range(8):
            cr_ref[r:r + 1, lanes] = c_r
            ci_ref[r:r + 1, lanes] = c_i
            n_r = a64r * c_r - a64i * c_i + fr[r:r + 1]
            n_i = a64r * c_i + a64i * c_r + fi[r:r + 1]
            c_r, c_i = n_r, n_i
        st_ref[0:1, lanes] = c_r
        st_ref[1:2, lanes] = c_i

        cin_r = jnp.concatenate([cr_ref[:, lanes]] * 2, axis=0)
        cin_i = jnp.concatenate([ci_ref[:, lanes]] * 2, axis=0)

        def fix(k2, carry):
            r0 = pl.multiple_of(k2 * 16, 16)
            pw = lambda ref, k: jnp.broadcast_to(ref[pl.ds(k, 1), lanes], (8, strip))
            pr = jnp.concatenate([pw(apr_ref, 2 * k2), pw(apr_ref, 2 * k2 + 1)], axis=0)
            pi = jnp.concatenate([pw(api_ref, 2 * k2), pw(api_ref, 2 * k2 + 1)], axis=0)
            xrb_ref[pl.ds(r0, 16), lanes] = (xr_ref[pl.ds(r0, 16), lanes] + pr * cin_r - pi * cin_i).astype(BF16)
            xib_ref[pl.ds(r0, 16), lanes] = (xi_ref[pl.ds(r0, 16), lanes] + pr * cin_i + pi * cin_r).astype(BF16)
            return carry

        lax.fori_loop(0, seg // 2, fix, 0, unroll=2)

    ys = []
    for half in range(2):
        states, chans = slice(half * hl, (half + 1) * hl), slice(half * hw, (half + 1) * hw)
        ys.append(_dot_nt(xrb_ref[:, states], cre_ref[chans, states])
                  - _dot_nt(xib_ref[:, states], cim_ref[chans, states]))
    y = jnp.concatenate(ys, axis=1)
    y = y + d_ref[...] * up_ref[...]
    y = 0.5 * y * (1.0 + jnp.tanh(math.sqrt(2.0 / math.pi) * (y + 0.044715 * (y * y * y))))
    gate = jnp.dot(y.astype(BF16), gw_ref[...], preferred_element_type=F32) + gb_ref[...]
    y_ref[...] = y * _sigmoid(gate)
    for j in range(S5_WIDTH // LANE):
        for k in range(seg):
            o_ref[j, pl.ds(k, 8, stride=seg), :] = y_ref[8 * k:8 * k + 8, j * LANE:(j + 1) * LANE]


def _s5(z3, b_re, b_im, c_re, c_im, ap_re, ap_im, d_skip, glu_w, glu_b, layer, *, tile):
    b, s, _ = z3.shape
    nblk = S5_WIDTH // LANE
    ublk = (ATTN_WIDTH + 2 * ATTN_KV_WIDTH + 4 * DN_WIDTH) // LANE
    const = lambda a: pl.BlockSpec(a.shape, lambda bi, t: (0, 0))
    layered = lambda a: pl.BlockSpec((None,) + a.shape[1:], lambda bi, t: (layer, 0, 0))
    ucol = lambda j: pl.BlockSpec((None, tile, LANE), lambda bi, t: (bi, t, ublk + j))
    return pl.pallas_call(
        _s5_kernel,
        grid=(b, s // tile),
        in_specs=[
            ucol(0), ucol(1), ucol(2), ucol(3),
            layered(b_re), layered(b_im), layered(c_re), layered(c_im), layered(ap_re), layered(ap_im),
            const(d_skip), layered(glu_w), const(glu_b),
        ],
        out_specs=pl.BlockSpec((nblk, None, tile, LANE), lambda bi, t: (0, bi, t, 0)),
        out_shape=jax.ShapeDtypeStruct((nblk, b, s, LANE), F32),
        scratch_shapes=[
            pltpu.VMEM((tile, S5_WIDTH), F32),
            pltpu.VMEM((tile, S5_LANES), F32),
            pltpu.VMEM((tile, S5_LANES), F32),
            pltpu.VMEM((tile, S5_LANES), BF16),
            pltpu.VMEM((tile, S5_LANES), BF16),
            pltpu.VMEM((8, S5_LANES), F32),
            pltpu.VMEM((8, S5_LANES), F32),
            pltpu.VMEM((8, S5_LANES), F32),
            pltpu.VMEM((tile, S5_WIDTH), F32),
        ],
        compiler_params=_params("parallel", "arbitrary"),
        name="s5",
    )(z3, z3, z3, z3, b_re, b_im, c_re, c_im, ap_re, ap_im, d_skip, glu_w, glu_b)


def _rope_tables(seq):
    half = HEAD_DIM // 2
    inv_freq = ROPE_THETA ** (-jnp.arange(half, dtype=F32) / half)
    ang = jnp.arange(seq, dtype=F32)[:, None] * inv_freq[None, :]
    cos, sin = jnp.cos(ang), jnp.sin(ang)
    return jnp.concatenate([cos, cos], axis=-1), jnp.concatenate([-sin, sin], axis=-1)


def _split_w_in(w_in):
    s5_src = SRC_BA + 2 * DN_HEADS
    pad = jnp.zeros(w_in.shape[:-1] + (LANE - 2 * DN_HEADS,), BF16)
    w_b = w_in.astype(BF16)
    w_ba = jnp.concatenate([w_b[..., SRC_BA:s5_src], pad], axis=-1)
    return w_b, w_b[..., s5_src:], w_ba


def _lane_pad(v, offset):
    return jnp.zeros((v.shape[0], 1, LANE), F32).at[:, 0, offset:offset + v.shape[1]].set(v)


def kernel(x, ff1_norm_pre, ff1_w_gate, ff1_w_up, ff1_w_down, ff1_norm_post, mix_norm_pre, w_in,
           attn_sinks, dn_conv_w, dn_a_log, dn_dt_bias, dn_norm_w, s5_a_re, s5_a_im, s5_log_dt,
           s5_b_re, s5_b_im, s5_c_re, s5_c_im, s5_d, s5_glu_w, s5_glu_b, w_out, mix_norm_post,
           ff2_norm_pre, ff2_w_gate, ff2_w_up, ff2_w_down, ff2_norm_post):
    b, s, d = x.shape
    depth = w_in.shape[0]
    n = b * s
    tm = min(PROJ_ROWS, n)
    tm_big = min(FFN_ROWS, n)
    tf = FFN_COLS
    tile = min(SEQ_TILE, s)

    bf = lambda a: a.astype(BF16)
    ff1 = (ff1_w_gate, ff1_w_up, ff1_w_down)
    ff2 = (ff2_w_gate, ff2_w_up, ff2_w_down)
    w_in_parts = _split_w_in(w_in)
    w_out_b = bf(w_out)
    glu_w_b = bf(s5_glu_w)
    ap_re, ap_im, *s5_mats = _s5_prep(s5_a_re, s5_a_im, s5_log_dt, s5_b_re, s5_b_im, s5_c_re, s5_c_im)
    cc, ss = _rope_tables(s)
    alog_pad = _lane_pad(dn_a_log, DN_HEADS)
    dtb_pad = _lane_pad(dn_dt_bias, DN_HEADS)
    row = lambda a, l: a[l].reshape(1, -1)

    xf = x.reshape(n, d)
    for l in range(depth):
        xf = _ffn(xf, row(ff1_norm_pre, l), *ff1, row(ff1_norm_post, l), l, tm=tm_big, tf=tf, tf_head=tf // 2)

        z3 = _inproj(xf, row(mix_norm_pre, l), *w_in_parts, l, tm=tm).reshape(b, s, Z_WIDTH)
        y_attn = _attention(z3, attn_sinks[l], cc, ss, tile=tile)
        y_dn = _deltanet(z3, dn_conv_w[l], alog_pad[l], dtb_pad[l], row(dn_norm_w, l), tile=min(DN_TILE, s))
        y_s5 = _s5(z3, *s5_mats, ap_re, ap_im, row(s5_d, l), glu_w_b, row(s5_glu_b, l), l, tile=min(S5_TILE, s))
        xf = _outproj(xf, y_attn.reshape(n, -1), y_dn.reshape(n, -1), y_s5.reshape(-1, n, LANE),
                      w_out_b, row(mix_norm_post, l), l, tm=tm)

        xf = _ffn(xf, row(ff2_norm_pre, l), *ff2, row(ff2_norm_post, l), l, tm=tm_big, tf=tf, tf_head=tf // 2)
    return xf.reshape(b, s, d)
```

```python
import functools
import math

import jax
import jax.numpy as jnp
from jax import lax
from jax.experimental import pallas as pl
from jax.experimental.pallas import tpu as pltpu

F32 = jnp.float32
BF16 = jnp.bfloat16
HIGHEST = lax.Precision.HIGHEST

NORM_EPS = 1e-6
FFN_RES_WEIGHT = 0.5
ROPE_THETA = 10000.0

HEAD_DIM = 128
WINDOW = 128
ATTN_HEADS = 8
ATTN_KV_HEADS = 2
ATTN_GROUP = ATTN_HEADS // ATTN_KV_HEADS
DN_HEADS = 4
DN_CONV = 4
DN_CHUNK = 64
S5_GROUPS = 32
S5_GROUP_CH = 16
S5_STATE = 64
ATTN_WIDTH = ATTN_HEADS * HEAD_DIM
ATTN_KV_WIDTH = ATTN_KV_HEADS * HEAD_DIM
DN_WIDTH = DN_HEADS * HEAD_DIM
S5_WIDTH = S5_GROUPS * S5_GROUP_CH
S5_LANES = S5_GROUPS * S5_STATE
MIX_WIDTH = ATTN_WIDTH + DN_WIDTH + S5_WIDTH

LANE = 128
Z_WIDTH = ATTN_WIDTH + 2 * ATTN_KV_WIDTH + 4 * DN_WIDTH + S5_WIDTH + LANE
Z_BA_BLOCK = (Z_WIDTH - LANE) // LANE
SRC_BA = ATTN_WIDTH + 2 * ATTN_KV_WIDTH + 4 * DN_WIDTH

SEQ_TILE = 512
DN_TILE = 1024
DN_GROUP_CHUNKS = 4
S5_TILE = 1024
S5_SEG = S5_TILE // 8
PROJ_ROWS = 512
FFN_ROWS = 1024
FFN_COLS = 512
FFN_DOWN_COLS = 512
FFN_SUB_COLS = 256
NORM_ROWS = 128
VMEM_LIMIT = 56 * 1024 * 1024


def _params(*sem):
    return pltpu.CompilerParams(dimension_semantics=sem, vmem_limit_bytes=VMEM_LIMIT)


def _rms(x, gain):
    return x * lax.rsqrt(jnp.mean(x * x, axis=-1, keepdims=True) + NORM_EPS) * gain


def _sigmoid(x):
    return 0.5 * jnp.tanh(0.5 * x) + 0.5


def _ffn_kernel(*refs, cast_weights):
    if cast_weights:
        x_ref, gpre_ref, wg_ref, wu_ref, wd_ref, gpost_ref, o_ref, wgb_ref, wub_ref, wdb_ref, h_ref = refs
        wgb_ref[...] = wg_ref[...].astype(BF16)
        wub_ref[...] = wu_ref[...].astype(BF16)
        wdb_ref[...] = wd_ref[...].astype(BF16)
        wg_ref, wu_ref, wd_ref = wgb_ref, wub_ref, wdb_ref
    else:
        x_ref, gpre_ref, wg_ref, wu_ref, wd_ref, gpost_ref, o_ref, h_ref = refs
    j = pl.program_id(1)

    row_blocks = [slice(r0, r0 + NORM_ROWS) for r0 in range(0, x_ref.shape[0], NORM_ROWS)]

    @pl.when(j == 0)
    def _():
        for rows in row_blocks:
            h_ref[rows, :] = _rms(x_ref[rows, :], gpre_ref[...]).astype(BF16)
        o_ref[...] = jnp.zeros_like(o_ref)

    h = h_ref[...]
    tf = wg_ref.shape[1]
    sub = min(FFN_SUB_COLS, tf)
    chunk = min(FFN_DOWN_COLS, o_ref.shape[1])
    for f0 in range(0, tf, sub):
        fs = slice(f0, f0 + sub)
        g = jnp.dot(h, wg_ref[:, fs], preferred_element_type=F32)
        u = jnp.dot(h, wu_ref[:, fs], preferred_element_type=F32)
        a = (g * _sigmoid(g) * u).astype(BF16)
        for c0 in range(0, o_ref.shape[1], chunk):
            cols = slice(c0, c0 + chunk)
            o_ref[:, cols] += jnp.dot(a, wd_ref[fs, cols], preferred_element_type=F32)

    @pl.when(j == pl.num_programs(1) - 1)
    def _():
        for rows in row_blocks:
            o_ref[rows, :] = x_ref[rows, :] + FFN_RES_WEIGHT * _rms(o_ref[rows, :], gpost_ref[...])


def _ffn(x, gpre, wg, wu, wd, gpost, layer, *, tm, tf, tf_head):
    n, d = x.shape
    f = wg.shape[2]
    once = pl.Buffered(1)
    vec = pl.BlockSpec((1, d), lambda i, j: (0, 0))
    shp = jax.ShapeDtypeStruct
    y, wgb, wub, wdb = pl.pallas_call(
        functools.partial(_ffn_kernel, cast_weights=True),
        grid=(1, f // tf_head),
        in_specs=[
            pl.BlockSpec((tm, d), lambda i, j: (0, 0), pipeline_mode=once),
            vec,
            pl.BlockSpec((None, d, tf_head), lambda i, j: (layer, 0, j)),
            pl.BlockSpec((None, d, tf_head), lambda i, j: (layer, 0, j)),
            pl.BlockSpec((None, tf_head, d), lambda i, j: (layer, j, 0)),
            vec,
        ],
        out_specs=[
            pl.BlockSpec((tm, d), lambda i, j: (0, 0), pipeline_mode=once),
            pl.BlockSpec((d, tf_head), lambda i, j: (0, j)),
            pl.BlockSpec((d, tf_head), lambda i, j: (0, j)),
            pl.BlockSpec((tf_head, d), lambda i, j: (j, 0)),
        ],
        out_shape=(shp((n, d), F32), shp((d, f), BF16), shp((d, f), BF16), shp((f, d), BF16)),
        input_output_aliases={0: 0},
        scratch_shapes=[pltpu.VMEM((tm, d), BF16)],
        compiler_params=_params("arbitrary", "arbitrary"),
        name="ffn_head",
    )(x, gpre, wg, wu, wd, gpost)
    return pl.pallas_call(
        functools.partial(_ffn_kernel, cast_weights=False),
        grid=(n // tm - 1, f // tf),
        in_specs=[
            pl.BlockSpec((tm, d), lambda i, j: (i + 1, 0)),
            vec,
            pl.BlockSpec((d, tf), lambda i, j: (0, j)),
            pl.BlockSpec((d, tf), lambda i, j: (0, j)),
            pl.BlockSpec((tf, d), lambda i, j: (j, 0)),
            vec,
        ],
        out_specs=pl.BlockSpec((tm, d), lambda i, j: (i + 1, 0)),
        out_shape=shp((n, d), F32),
        input_output_aliases={0: 0},
        scratch_shapes=[pltpu.VMEM((tm, d), BF16)],
        compiler_params=_params("parallel", "arbitrary"),
        name="ffn",
    )(y, gpre, wgb, wub, wdb, gpost)


def _inproj_kernel(x_ref, g_ref, wm_ref, ws_ref, wb_ref, o_ref):
    h = _rms(x_ref[...], g_ref[...]).astype(BF16)
    s5_at = wm_ref.shape[1]
    ba_at = s5_at + ws_ref.shape[1]
    o_ref[:, :s5_at] = jnp.dot(h, wm_ref[...], preferred_element_type=F32)
    o_ref[:, s5_at:ba_at] = jnp.dot(h, ws_ref[...], preferred_element_type=F32)
    o_ref[:, ba_at:] = jnp.dot(h, wb_ref[...], preferred_element_type=F32)


def _inproj(x, gain, w_all, w_s5, w_ba, layer, *, tm):
    n, d = x.shape
    resident = lambda shape: pl.BlockSpec((None,) + shape, lambda i: (layer, 0, 0), pipeline_mode=pl.Buffered(1))
    return pl.pallas_call(
        _inproj_kernel,
        grid=(n // tm,),
        in_specs=[
            pl.BlockSpec((tm, d), lambda i: (i, 0)),
            pl.BlockSpec((1, d), lambda i: (0, 0)),
            resident((d, SRC_BA)), resident((d, S5_WIDTH)), resident((d, LANE)),
        ],
        out_specs=pl.BlockSpec((tm, Z_WIDTH), lambda i: (i, 0)),
        out_shape=jax.ShapeDtypeStruct((n, Z_WIDTH), F32),
        compiler_params=_params("parallel"),
        name="inproj",
    )(x, gain, w_all, w_s5, w_ba)


def _outproj_kernel(x_ref, ya_ref, yd_ref, ys_ref, w_ref, g_ref, o_ref):
    ys = [ys_ref[j].astype(BF16) for j in range(ys_ref.shape[0])]
    y = jnp.concatenate([ya_ref[...], yd_ref[...]] + ys, axis=-1)
    mixed = jnp.dot(y, w_ref[...], preferred_element_type=F32)
    o_ref[...] = x_ref[...] + _rms(mixed, g_ref[...])


def _outproj(x, ya, yd, ys, w, gain, layer, *, tm):
    n, d = x.shape
    return pl.pallas_call(
        _outproj_kernel,
        grid=(n // tm,),
        in_specs=[
            pl.BlockSpec((tm, d), lambda i: (i, 0)),
            pl.BlockSpec((tm, ya.shape[1]), lambda i: (i, 0)),
            pl.BlockSpec((tm, yd.shape[1]), lambda i: (i, 0)),
            pl.BlockSpec((ys.shape[0], tm, ys.shape[2]), lambda i: (0, i, 0)),
            pl.BlockSpec((None,) + w.shape[1:], lambda i: (layer, 0, 0)),
            pl.BlockSpec((1, d), lambda i: (0, 0)),
        ],
        out_specs=pl.BlockSpec((tm, d), lambda i: (i, 0)),
        out_shape=jax.ShapeDtypeStruct((n, d), F32),
        compiler_params=_params("parallel"),
        name="outproj",
    )(x, ya, yd, ys, w, gain)


def _rope(x, cc, ss):
    return x * cc + pltpu.roll(x, HEAD_DIM // 2, axis=1) * ss


def _attn_kernel(sink_ref, q_ref, kp_ref, kc_ref, vp_ref, vc_ref, ccp_ref, ccc_ref, ssp_ref, ssc_ref, o_ref):
    h = pl.program_id(1)
    t = pl.program_id(2)
    w = WINDOW
    grp = ATTN_GROUP
    nblk = q_ref.shape[0] // w
    ccc = ccc_ref[...]
    ssc = ssc_ref[...]
    q = q_ref[...]
    scale = HEAD_DIM ** -0.5
    qr = [(_rope(q[:, g * HEAD_DIM:(g + 1) * HEAD_DIM], ccc, ssc) * scale).astype(BF16) for g in range(grp)]
    kk = jnp.concatenate([_rope(kp_ref[...], ccp_ref[...], ssp_ref[...]), _rope(kc_ref[...], ccc, ssc)],
                         axis=0).astype(BF16)
    vv = jnp.concatenate([vp_ref[...], vc_ref[...]], axis=0).astype(BF16)
    qi = lax.broadcasted_iota(jnp.int32, (w, 2 * w), 0) + w
    kj = lax.broadcasted_iota(jnp.int32, (w, 2 * w), 1)
    rel = qi - kj
    band = (rel >= 0) & (rel < w)
    first = band & ((kj >= w) | (t > 0))
    row_head = lax.broadcasted_iota(jnp.int32, (grp * w, 1), 0) // w
    sink = jnp.zeros((grp * w, 1), F32)
    for g in range(grp):
        sink = jnp.where(row_head == g, sink_ref[h * grp + g], sink)
    scores = []
    for blk in range(nblk):
        qs = jnp.concatenate([qr[g][blk * w:(blk + 1) * w] for g in range(grp)], axis=0)
        scores.append(_dot_nt(qs, kk[blk * w:(blk + 2) * w]))
    probs = []
    for blk in range(nblk):
        mask = jnp.concatenate([first if blk == 0 else band] * grp, axis=0)
        sc = jnp.where(mask, scores[blk], -jnp.inf)
        m = jnp.maximum(jnp.max(sc, axis=-1, keepdims=True), sink)
        p = jnp.exp(sc - m)
        denom = jnp.sum(p, axis=-1, keepdims=True) + jnp.exp(sink - m)
        probs.append((p / denom).astype(BF16))
    outs = [jnp.dot(probs[blk], vv[blk * w:(blk + 2) * w], preferred_element_type=F32) for blk in range(nblk)]
    for blk in range(nblk):
        o_ref[blk * w:(blk + 1) * w, :] = jnp.concatenate(
            [outs[blk][g * w:(g + 1) * w] for g in range(grp)], axis=1).astype(o_ref.dtype)


def _attention(z3, sinks, cc, ss, *, tile):
    b, s, _ = z3.shape
    w = WINDOW
    nblk = tile // w
    qw = ATTN_GROUP * HEAD_DIM
    kblk = ATTN_WIDTH // HEAD_DIM
    vblk = (ATTN_WIDTH + ATTN_KV_WIDTH) // HEAD_DIM
    prev = lambda t: jnp.maximum(t * nblk - 1, 0)
    return pl.pallas_call(
        _attn_kernel,
        grid=(b, ATTN_KV_HEADS, s // tile),
        in_specs=[
            pl.BlockSpec(memory_space=pltpu.SMEM),
            pl.BlockSpec((None, tile, qw), lambda bi, h, t: (bi, t, h)),
            pl.BlockSpec((None, w, HEAD_DIM), lambda bi, h, t: (bi, prev(t), kblk + h)),
            pl.BlockSpec((None, tile, HEAD_DIM), lambda bi, h, t: (bi, t, kblk + h)),
            pl.BlockSpec((None, w, HEAD_DIM), lambda bi, h, t: (bi, prev(t), vblk + h)),
            pl.BlockSpec((None, tile, HEAD_DIM), lambda bi, h, t: (bi, t, vblk + h)),
            pl.BlockSpec((w, HEAD_DIM), lambda bi, h, t: (prev(t), 0)),
            pl.BlockSpec((tile, HEAD_DIM), lambda bi, h, t: (t, 0)),
            pl.BlockSpec((w, HEAD_DIM), lambda bi, h, t: (prev(t), 0)),
            pl.BlockSpec((tile, HEAD_DIM), lambda bi, h, t: (t, 0)),
        ],
        out_specs=pl.BlockSpec((None, tile, qw), lambda bi, h, t: (bi, t, h)),
        out_shape=jax.ShapeDtypeStruct((b, s, ATTN_WIDTH), BF16),
        compiler_params=_params("parallel", "parallel", "arbitrary"),
        name="swa",
    )(sinks, z3, z3, z3, z3, z3, cc, cc, ss, ss)


def _silu(x):
    return x * _sigmoid(x)


def _softplus(x):
    return jnp.maximum(x, 0.0) + jnp.log(1.0 + jnp.exp(-jnp.abs(x)))


def _dot_nt(a, b, precision=None):
    return lax.dot_general(a, b, (((1,), (1,)), ((), ())), preferred_element_type=F32, precision=precision)


def _dot_tn(a, b, precision=None):
    return lax.dot_general(a, b, (((0,), (0,)), ((), ())), preferred_element_type=F32, precision=precision)


def _dn_kernel(q_ref, k_ref, v_ref, zg_ref, ba_ref, cw_ref, alog_ref, dtb_ref, nw_ref, o_ref,
               ext_ref, qn_ref, kn_ref, vn_ref, gc_ref, gct_ref, beta_ref, s_ref):
    t = pl.program_id(1)
    tile = q_ref.shape[0]
    c = DN_CHUNK
    d = HEAD_DIM
    nh = DN_HEADS
    wdt = nh * d

    @pl.when(t == 0)
    def _():
        ext_ref[0:8, :] = jnp.zeros((8, 3 * wdt), F32)
        s_ref[...] = jnp.zeros_like(s_ref)

    ext_ref[8:8 + tile, 0:wdt] = q_ref[...]
    ext_ref[8:8 + tile, wdt:2 * wdt] = k_ref[...]
    ext_ref[8:8 + tile, 2 * wdt:3 * wdt] = v_ref[...]
    ext = ext_ref[...]
    conv = ext[8:] * cw_ref[DN_CONV - 1:DN_CONV, :]
    for back in range(1, DN_CONV):
        conv = conv + pltpu.roll(ext, back, axis=0)[8:] * cw_ref[DN_CONV - 1 - back:DN_CONV - back, :]
    ext_ref[0:8, :] = ext[tile:tile + 8]
    qkv = _silu(conv)
    for hd in range(nh):
        qh = qkv[:, hd * d:(hd + 1) * d]
        kh = qkv[:, wdt + hd * d:wdt + (hd + 1) * d]
        qn_ref[:, hd * d:(hd + 1) * d] = qh * lax.rsqrt(jnp.sum(qh * qh, axis=-1, keepdims=True) + NORM_EPS) * (d ** -0.5)
        kn_ref[:, hd * d:(hd + 1) * d] = kh * lax.rsqrt(jnp.sum(kh * kh, axis=-1, keepdims=True) + NORM_EPS)
    vn_ref[...] = qkv[:, 2 * wdt:3 * wdt]
    ba = ba_ref[...]
    beta_ref[...] = _sigmoid(ba)
    g = -jnp.exp(alog_ref[...]) * _softplus(ba + dtb_ref[...])
    ti = lax.broadcasted_iota(jnp.int32, (c, c), 0)
    tj = lax.broadcasted_iota(jnp.int32, (c, c), 1)
    csum = (ti >= tj).astype(F32)
    gc_all = jnp.concatenate([jnp.dot(csum, g[ci * c:(ci + 1) * c], preferred_element_type=F32, precision=HIGHEST)
                              for ci in range(tile // c)], axis=0)
    gc_ref[...] = gc_all
    gct_ref[...] = gc_all.T[0:8, :]

    row = lax.broadcasted_iota(jnp.int32, (c, c), 0)
    col = lax.broadcasted_iota(jnp.int32, (c, c), 1)
    causal = row >= col
    strict = row > col
    eye = (row == col).astype(F32)
    nw = nw_ref[...]

    def join_mask(half):
        return (row // (2 * half) == col // (2 * half)) & (row % (2 * half) >= half) & (col % (2 * half) < half)

    def local_stages(chunks, items):
        for ci in chunks:
            r0 = ci * c
            for hd in range(nh):
                lanes = slice(hd * d, (hd + 1) * d)
                q = qn_ref[r0:r0 + c, lanes]
                k = kn_ref[r0:r0 + c, lanes]
                beta = beta_ref[r0:r0 + c, hd:hd + 1]
                gcol = gc_ref[r0:r0 + c, nh + hd:nh + hd + 1]
                grow = gct_ref[nh + hd:nh + hd + 1, r0:r0 + c]
                glast = gcol[c - 1:c, :]
                decay = jnp.exp(jnp.where(causal, gcol - grow, -jnp.inf))
                eg = jnp.exp(gcol)
                kb = k * beta
                items.append(dict(ci=ci, hd=hd, r0=r0, lanes=lanes, decay=decay, kbf=k.astype(BF16),
                                  qbf=q.astype(BF16), kb_bf=kb.astype(BF16),
                                  rhs=jnp.concatenate([vn_ref[r0:r0 + c, lanes] * beta, kb * eg], axis=1).astype(BF16),
                                  qd=(q * eg).astype(BF16), k_dec=(k * jnp.exp(glast - gcol)).astype(BF16),
                                  egl=jnp.exp(glast)))
        for it in items:
            it["kk"] = _dot_nt(it["kb_bf"], it["kbf"])
            it["qk"] = _dot_nt(it["qbf"], it["kbf"])
        yield
        for it in items:
            it["nmat"] = jnp.where(strict, it["kk"] * it["decay"], 0.0)
            it["inv"] = eye - jnp.where(join_mask(1), it["nmat"], 0.0)
            it["attn"] = (it["qk"] * it["decay"]).astype(BF16)
        half = 2
        while half < c:
            for it in items:
                it["inv_bf"] = it["inv"].astype(BF16)
                join = jnp.where(join_mask(half), it["nmat"], 0.0).astype(BF16)
                it["bt"] = jnp.dot(join, it["inv_bf"], preferred_element_type=F32).astype(BF16)
            yield
            for it in items:
                it["upd"] = jnp.dot(it["inv_bf"], it["bt"], preferred_element_type=F32)
            yield
            for it in items:
                it["inv"] = it["inv"] - it["upd"]
            half *= 2
        for it in items:
            uw = jnp.dot(it["inv"].astype(BF16), it["rhs"], preferred_element_type=F32)
            it["u"] = uw[:, :d]
            it["wq"] = jnp.concatenate([uw[:, d:].astype(BF16), it["qd"]], axis=0)
        yield

    def sweep_stages(items, states):
        for ci in sorted({it["ci"] for it in items}):
            group = [it for it in items if it["ci"] == ci]
            for it in group:
                it["ws_qs"] = jnp.dot(it["wq"], states[it["hd"]].astype(BF16), preferred_element_type=F32)
            yield
            for it in group:
                it["v_new"] = (it["u"] - it["ws_qs"][:c]).astype(BF16)
            for it in group:
                it["av"] = jnp.dot(it["attn"], it["v_new"], preferred_element_type=F32)
                it["kv"] = _dot_tn(it["k_dec"], it["v_new"])
            yield
            for it in group:
                states[it["hd"]] = states[it["hd"]] * it["egl"] + it["kv"]
                o = it["ws_qs"][c:] + it["av"]
                o = o * lax.rsqrt(jnp.mean(o * o, axis=-1, keepdims=True) + NORM_EPS) * nw
                o = o * _silu(zg_ref[it["r0"]:it["r0"] + c, it["lanes"]])
                o_ref[it["r0"]:it["r0"] + c, it["lanes"]] = o.astype(o_ref.dtype)

    def interleave(*gens):
        live = list(gens)
        while live:
            for gen in list(live):
                if next(gen, "end") == "end":
                    live.remove(gen)

    states = [s_ref[hd] for hd in range(nh)]
    group_chunks = DN_GROUP_CHUNKS
    groups = [list(range(g0, g0 + group_chunks)) for g0 in range(0, tile // c, group_chunks)]
    ready = []
    interleave(local_stages(groups[0], ready))
    for nxt in groups[1:]:
        upcoming = []
        interleave(local_stages(nxt, upcoming), sweep_stages(ready, states))
        ready = upcoming
    interleave(sweep_stages(ready, states))
    for hd in range(nh):
        s_ref[hd] = states[hd]


def _deltanet(z3, conv_w, alog_pad, dtb_pad, norm_w, *, tile):
    b, s, _ = z3.shape
    wdt = DN_WIDTH
    qblk = (ATTN_WIDTH + 2 * ATTN_KV_WIDTH) // wdt
    seq = lambda off: pl.BlockSpec((None, tile, wdt), lambda bi, t: (bi, t, qblk + off))
    const = lambda shape: pl.BlockSpec(shape, lambda bi, t: (0, 0))
    return pl.pallas_call(
        _dn_kernel,
        grid=(b, s // tile),
        in_specs=[
            seq(0), seq(1), seq(2), seq(3),
            pl.BlockSpec((None, tile, LANE), lambda bi, t: (bi, t, Z_BA_BLOCK)),
            const(conv_w.shape), const((1, LANE)), const((1, LANE)), const((1, HEAD_DIM)),
        ],
        out_specs=pl.BlockSpec((None, tile, wdt), lambda bi, t: (bi, t, 0)),
        out_shape=jax.ShapeDtypeStruct((b, s, wdt), BF16),
        scratch_shapes=[
            pltpu.VMEM((tile + 8, 3 * wdt), F32),
            pltpu.VMEM((tile, wdt), F32),
            pltpu.VMEM((tile, wdt), F32),
            pltpu.VMEM((tile, wdt), F32),
            pltpu.VMEM((tile, LANE), F32),
            pltpu.VMEM((8, tile), F32),
            pltpu.VMEM((tile, LANE), F32),
            pltpu.VMEM((DN_HEADS, HEAD_DIM, HEAD_DIM), F32),
        ],
        compiler_params=_params("parallel", "arbitrary"),
        name="deltanet",
    )(z3, z3, z3, z3, z3, conv_w, alog_pad, dtb_pad, norm_w)


def _s5_prep_kernel(are_ref, aim_ref, ldt_ref, bre_ref, bim_ref, cre_ref, cim_ref,
                    apr_ref, api_ref, bbr_ref, bbi_ref, cbr_ref, cbi_ref):
    lr = are_ref[...]
    li = aim_ref[...]
    dt = jnp.exp(ldt_ref[...])
    step = (lax.broadcasted_iota(jnp.int32, (S5_SEG, 1), 0) + 1).astype(F32)
    mag = jnp.exp(step * (lr * dt))
    ang = step * (li * dt)
    apr_ref[...] = mag * jnp.cos(ang)
    api_ref[...] = mag * jnp.sin(ang)
    m1 = jnp.exp(lr * dt)
    nr = m1 * jnp.cos(li * dt) - 1.0
    ni = m1 * jnp.sin(li * dt)
    den = lr * lr + li * li
    cr = (nr * lr + ni * li) / den
    ci = (ni * lr - nr * li) / den
    br = bre_ref[...]
    bi = bim_ref[...]
    mats = ((bbr_ref, cr * br - ci * bi), (bbi_ref, cr * bi + ci * br), (cbr_ref, cre_ref[...]), (cbi_ref, cim_ref[...]))
    lane_group = lax.broadcasted_iota(jnp.int32, (S5_GROUP_CH, S5_LANES), 1) // S5_STATE
    for g in range(S5_GROUPS):
        for out_ref, val in mats:
            out_ref[g * S5_GROUP_CH:(g + 1) * S5_GROUP_CH, :] = jnp.where(lane_group == g, val, 0.0).astype(BF16)


def _s5_prep(a_re, a_im, log_dt, b_re, b_im, c_re, c_im):
    depth = a_re.shape[0]
    row = lambda a: a.reshape(depth, 1, S5_LANES)
    ldt = jnp.repeat(log_dt, S5_STATE, axis=-1).reshape(depth, 1, S5_LANES)
    b_rows = lambda a: a.reshape(depth, S5_LANES, S5_GROUP_CH).transpose(0, 2, 1)
    c_rows = lambda a: a.transpose(0, 2, 1, 3).reshape(depth, S5_GROUP_CH, S5_LANES)
    shp = jax.ShapeDtypeStruct
    per_layer = lambda rows: pl.BlockSpec((None, rows, S5_LANES), lambda l: (l, 0, 0))
    chan = per_layer(S5_GROUP_CH)
    return pl.pallas_call(
        _s5_prep_kernel,
        grid=(depth,),
        in_specs=[per_layer(1), per_layer(1), per_layer(1), chan, chan, chan, chan],
        out_specs=[per_layer(S5_SEG), per_layer(S5_SEG)] + [per_layer(S5_WIDTH)] * 4,
        out_shape=(shp((depth, S5_SEG, S5_LANES), F32),) * 2 + (shp((depth, S5_WIDTH, S5_LANES), BF16),) * 4,
        name="s5_prep",
    )(row(a_re), row(a_im), ldt, b_rows(b_re), b_rows(b_im), c_rows(c_re), c_rows(c_im))


def _s5_kernel(u0_ref, u1_ref, u2_ref, u3_ref, bre_ref, bim_ref, cre_ref, cim_ref, apr_ref, api_ref, d_ref,
               gw_ref, gb_ref, o_ref, up_ref, xr_ref, xi_ref, xrb_ref, xib_ref, cr_ref, ci_ref, st_ref, y_ref):
    t = pl.program_id(1)
    tile = u0_ref.shape[0]
    seg = tile // 8
    strip = 1024

    @pl.when(t == 0)
    def _():
        st_ref[...] = jnp.zeros_like(st_ref)

    for j, u_ref in enumerate((u0_ref, u1_ref, u2_ref, u3_ref)):
        for r in range(8):
            for m in range(seg // 8):
                up_ref[j, pl.ds(64 * m + r, 8, stride=8), :] = u_ref[r * seg + 8 * m:r * seg + 8 * m + 8, :]
    hw, hl = S5_WIDTH // 2, S5_LANES // 2
    for half in range(2):
        ub = jnp.concatenate([up_ref[2 * half], up_ref[2 * half + 1]], axis=1).astype(BF16)
        rows, cols = slice(half * hw, (half + 1) * hw), slice(half * hl, (half + 1) * hl)
        xr_ref[:, cols] = jnp.dot(ub, bre_ref[rows, cols], preferred_element_type=F32)
        xi_ref[:, cols] = jnp.dot(ub, bim_ref[rows, cols], preferred_element_type=F32)

    for s0 in range(0, S5_LANES, strip):
        lanes = slice(s0, s0 + strip)
        ar = jnp.broadcast_to(apr_ref[0:1, lanes], (8, strip))
        ai = jnp.broadcast_to(api_ref[0:1, lanes], (8, strip))

        def scan(k, carry):
            pr, pi = carry
            r0 = pl.multiple_of(k * 8, 8)
            nr = ar * pr - ai * pi + xr_ref[pl.ds(r0, 8), lanes]
            ni = ar * pi + ai * pr + xi_ref[pl.ds(r0, 8), lanes]
            xr_ref[pl.ds(r0, 8), lanes] = nr
            xi_ref[pl.ds(r0, 8), lanes] = ni
            return nr, ni

        zero = jnp.zeros((8, strip), F32)
        fr, fi = lax.fori_loop(0, seg, scan, (zero, zero), unroll=4)

        a64r = apr_ref[seg - 1:seg, lanes]
        a64i = api_ref[seg - 1:seg, lanes]
        c_r = st_ref[0:1, lanes]
        c_i = st_ref[1:2, lanes]
        for r in range(8):
            cr_ref[r:r + 1, lanes] = c_r
            ci_ref[r:r + 1, lanes] = c_i
            n_r = a64r * c_r - a64i * c_i + fr[r:r + 1]
            n_i = a64r * c_i + a64i * c_r + fi[r:r + 1]
            c_r, c_i = n_r, n_i
        st_ref[0:1, lanes] = c_r
        st_ref[1:2, lanes] = c_i

        cin_r = jnp.concatenate([cr_ref[:, lanes]] * 2, axis=0)
        cin_i = jnp.concatenate([ci_ref[:, lanes]] * 2, axis=0)

        def fix(k2, carry):
            r0 = pl.multiple_of(k2 * 16, 16)
            pw = lambda ref, k: jnp.broadcast_to(ref[pl.ds(k, 1), lanes], (8, strip))
            pr = jnp.concatenate([pw(apr_ref, 2 * k2), pw(apr_ref, 2 * k2 + 1)], axis=0)
            pi = jnp.concatenate([pw(api_ref, 2 * k2), pw(api_ref, 2 * k2 + 1)], axis=0)
            xrb_ref[pl.ds(r0, 16), lanes] = (xr_ref[pl.ds(r0, 16), lanes] + pr * cin_r - pi * cin_i).astype(BF16)
            xib_ref[pl.ds(r0, 16), lanes] = (xi_ref[pl.ds(r0, 16), lanes] + pr * cin_i + pi * cin_r).astype(BF16)
            return carry

        lax.fori_loop(0, seg // 2, fix, 0, unroll=2)

    ys = []
    for half in range(2):
        states, chans = slice(half * hl, (half + 1) * hl), slice(half * hw, (half + 1) * hw)
        ys.append(_dot_nt(xrb_ref[:, states], cre_ref[chans, states])
                  - _dot_nt(xib_ref[:, states], cim_ref[chans, states]))
    y = jnp.concatenate(ys, axis=1)
    y = y + d_ref[...] * jnp.concatenate([up_ref[j] for j in range(S5_WIDTH // LANE)], axis=1)
    y = 0.5 * y * (1.0 + jnp.tanh(math.sqrt(2.0 / math.pi) * (y + 0.044715 * (y * y * y))))
    gate = jnp.dot(y.astype(BF16), gw_ref[...], preferred_element_type=F32) + gb_ref[...]
    y = y * _sigmoid(gate)
    for j in range(S5_WIDTH // LANE):
        y_ref[j] = y[:, j * LANE:(j + 1) * LANE]
        for r in range(8):
            for m in range(seg // 8):
                o_ref[j, r * seg + 8 * m:r * seg + 8 * m + 8, :] = y_ref[j, pl.ds(64 * m + r, 8, stride=8), :]


def _s5(z3, b_re, b_im, c_re, c_im, ap_re, ap_im, d_skip, glu_w, glu_b, layer, *, tile):
    b, s, _ = z3.shape
    nblk = S5_WIDTH // LANE
    ublk = (ATTN_WIDTH + 2 * ATTN_KV_WIDTH + 4 * DN_WIDTH) // LANE
    const = lambda a: pl.BlockSpec(a.shape, lambda bi, t: (0, 0))
    layered = lambda a: pl.BlockSpec((None,) + a.shape[1:], lambda bi, t: (layer, 0, 0))
    ucol = lambda j: pl.BlockSpec((None, tile, LANE), lambda bi, t: (bi, t, ublk + j))
    return pl.pallas_call(
        _s5_kernel,
        grid=(b, s // tile),
        in_specs=[
            ucol(0), ucol(1), ucol(2), ucol(3),
            layered(b_re), layered(b_im), layered(c_re), layered(c_im), layered(ap_re), layered(ap_im),
            const(d_skip), layered(glu_w), const(glu_b),
        ],
        out_specs=pl.BlockSpec((nblk, None, tile, LANE), lambda bi, t: (0, bi, t, 0)),
        out_shape=jax.ShapeDtypeStruct((nblk, b, s, LANE), F32),
        scratch_shapes=[
            pltpu.VMEM((nblk, tile, LANE), F32),
            pltpu.VMEM((tile, S5_LANES), F32),
            pltpu.VMEM((tile, S5_LANES), F32),
            pltpu.VMEM((tile, S5_LANES), BF16),
            pltpu.VMEM((tile, S5_LANES), BF16),
            pltpu.VMEM((8, S5_LANES), F32),
            pltpu.VMEM((8, S5_LANES), F32),
            pltpu.VMEM((8, S5_LANES), F32),
            pltpu.VMEM((nblk, tile, LANE), F32),
        ],
        compiler_params=_params("parallel", "arbitrary"),
        name="s5",
    )(z3, z3, z3, z3, b_re, b_im, c_re, c_im, ap_re, ap_im, d_skip, glu_w, glu_b)


def _rope_tables(seq):
    half = HEAD_DIM // 2
    inv_freq = ROPE_THETA ** (-jnp.arange(half, dtype=F32) / half)
    ang = jnp.arange(seq, dtype=F32)[:, None] * inv_freq[None, :]
    cos, sin = jnp.cos(ang), jnp.sin(ang)
    return jnp.concatenate([cos, cos], axis=-1), jnp.concatenate([-sin, sin], axis=-1)


def _split_w_in(w_in):
    s5_src = SRC_BA + 2 * DN_HEADS
    pad = jnp.zeros(w_in.shape[:-1] + (LANE - 2 * DN_HEADS,), BF16)
    w_b = w_in.astype(BF16)
    w_ba = jnp.concatenate([w_b[..., SRC_BA:s5_src], pad], axis=-1)
    return w_b, w_b[..., s5_src:], w_ba


def _lane_pad(v, offset):
    return jnp.zeros((v.shape[0], 1, LANE), F32).at[:, 0, offset:offset + v.shape[1]].set(v)


def kernel(x, ff1_norm_pre, ff1_w_gate, ff1_w_up, ff1_w_down, ff1_norm_post, mix_norm_pre, w_in,
           attn_sinks, dn_conv_w, dn_a_log, dn_dt_bias, dn_norm_w, s5_a_re, s5_a_im, s5_log_dt,
           s5_b_re, s5_b_im, s5_c_re, s5_c_im, s5_d, s5_glu_w, s5_glu_b, w_out, mix_norm_post,
           ff2_norm_pre, ff2_w_gate, ff2_w_up, ff2_w_down, ff2_norm_post):
    b, s, d = x.shape
    depth = w_in.shape[0]
    n = b * s
    tm = min(PROJ_ROWS, n)
    tm_big = min(FFN_ROWS, n)
    tf = FFN_COLS
    tile = min(SEQ_TILE, s)

    bf = lambda a: a.astype(BF16)
    ff1 = (ff1_w_gate, ff1_w_up, ff1_w_down)
    ff2 = (ff2_w_gate, ff2_w_up, ff2_w_down)
    w_in_parts = _split_w_in(w_in)
    w_out_b = bf(w_out)
    glu_w_b = bf(s5_glu_w)
    ap_re, ap_im, *s5_mats = _s5_prep(s5_a_re, s5_a_im, s5_log_dt, s5_b_re, s5_b_im, s5_c_re, s5_c_im)
    cc, ss = _rope_tables(s)
    alog_pad = _lane_pad(dn_a_log, DN_HEADS)
    dtb_pad = _lane_pad(dn_dt_bias, DN_HEADS)
    row = lambda a, l: a[l].reshape(1, -1)

    xf = x.reshape(n, d)
    for l in range(depth):
        xf = _ffn(xf, row(ff1_norm_pre, l), *ff1, row(ff1_norm_post, l), l, tm=tm_big, tf=tf, tf_head=tf // 2)

        z3 = _inproj(xf, row(mix_norm_pre, l), *w_in_parts, l, tm=tm).reshape(b, s, Z_WIDTH)
        y_attn = _attention(z3, attn_sinks[l], cc, ss, tile=tile)
        y_dn = _deltanet(z3, dn_conv_w[l], alog_pad[l], dtb_pad[l], row(dn_norm_w, l), tile=min(DN_TILE, s))
        y_s5 = _s5(z3, *s5_mats, ap_re, ap_im, row(s5_d, l), glu_w_b, row(s5_glu_b, l), l, tile=min(S5_TILE, s))
        xf = _outproj(xf, y_attn.reshape(n, -1), y_dn.reshape(n, -1), y_s5.reshape(-1, n, LANE),
                      w_out_b, row(mix_norm_post, l), l, tm=tm)

        xf = _ffn(xf, row(ff2_norm_pre, l), *ff2, row(ff2_norm_post, l), l, tm=tm_big, tf=tf, tf_head=tf // 2)
    return xf.reshape(b, s, d)
```
